```python
import math
import jax
import jax.numpy as jnp
from jax import lax
import numpy as np

D_MODEL = 1024
BATCH = 4
SEQ = 4096
DEPTH = 4
DEC_BATCH = 32
DEC_SEQ = 1
PAST_LEN = 8192
PAGE_SIZE = 128

S5_WIDTH = D_MODEL // 2
S5_GROUP = 16
S5_GROUPS = S5_WIDTH // S5_GROUP
S5_STATE = 64
ATT_WIDTH = D_MODEL - S5_WIDTH
HEAD_DIM = 64
N_HEADS = ATT_WIDTH // HEAD_DIM
MOBA_BLOCK = 256
MOBA_TOPK = 3
Q_CHUNK = 32
REL_BUCKETS = 32
REL_MAX_DIST = 1024
POOL_WINDOWS = (2, 4, 8, 16)
POOL_GROUPS = 4
POOL_GW = D_MODEL // POOL_GROUPS
POOL_HIST = max(POOL_WINDOWS) - 1
D_FF = 2816
N_EXPERTS = 8
TOP_K_EXPERTS = 2
D_FF_EXPERT = 3584
N_EVEN = (DEPTH + 1) // 2
N_ODD = DEPTH // 2
EPS = 1e-6

kernel_name = 'hybrid_s5_moba_pool_decoder_step'


def _rmsnorm(x, g):
    xf = x.astype(jnp.float32)
    y = xf * lax.rsqrt(jnp.mean(xf * xf, axis=-1, keepdims=True) + EPS)
    return (y * g.astype(jnp.float32)).astype(x.dtype)


def _swiglu(h, wg, wu, wd):
    return (jax.nn.silu(h @ wg) * (h @ wu)) @ wd


def _t5_bucket(rel):
    n = jnp.maximum(rel, 0)
    max_exact = REL_BUCKETS // 2
    nf = jnp.maximum(n, 1).astype(jnp.float32)
    large = max_exact + (jnp.log(nf / max_exact) / math.log(REL_MAX_DIST / max_exact)
                         * (REL_BUCKETS - max_exact)).astype(jnp.int32)
    large = jnp.minimum(large, REL_BUCKETS - 1)
    return jnp.where(n < max_exact, n, large)


def _s5_mix(u, h0_re, h0_im, a_re, a_im, log_dt, b_re, b_im, c_re, c_im, d_skip, w_glu, b_glu):
    bsz, s, _ = u.shape
    f32 = jnp.float32
    uf = u.astype(f32).reshape(bsz, s, S5_GROUPS, S5_GROUP)
    dt = jnp.exp(log_dt.astype(f32))[:, None]
    ar = a_re.astype(f32)
    ai = a_im.astype(f32)
    mag = jnp.exp(ar * dt)
    lr = mag * jnp.cos(ai * dt)
    li = mag * jnp.sin(ai * dt)
    den = ar * ar + ai * ai
    rr = ((lr - 1.0) * ar + li * ai) / den
    ri = (li * ar - (lr - 1.0) * ai) / den
    br = b_re.astype(f32)
    bi = b_im.astype(f32)
    bbr = rr[..., None] * br - ri[..., None] * bi
    bbi = rr[..., None] * bi + ri[..., None] * br
    bur = jnp.einsum('bsgp,gnp->bsgn', uf, bbr)
    bui = jnp.einsum('bsgp,gnp->bsgn', uf, bbi)
    lam_r = jnp.broadcast_to(lr, bur.shape)
    lam_i = jnp.broadcast_to(li, bur.shape)

    def combine(e1, e2):
        a1r, a1i, b1r, b1i = e1
        a2r, a2i, b2r, b2i = e2
        return (a1r * a2r - a1i * a2i,
                a1r * a2i + a1i * a2r,
                a2r * b1r - a2i * b1i + b2r,
                a2r * b1i + a2i * b1r + b2i)

    pr, pim, hr, hi = lax.associative_scan(combine, (lam_r, lam_i, bur, bui), axis=1)
    h0r = h0_re.astype(f32)[:, None]
    h0i = h0_im.astype(f32)[:, None]
    hr = hr + pr * h0r - pim * h0i
    hi = hi + pr * h0i + pim * h0r
    y = (jnp.einsum('bsgn,gpn->bsgp', hr, c_re.astype(f32))
         - jnp.einsum('bsgn,gpn->bsgp', hi, c_im.astype(f32))
         + d_skip.astype(f32) * uf).reshape(bsz, s, S5_WIDTH)
    z = jax.nn.gelu(y)
    out = z * jax.nn.sigmoid(z @ w_glu.astype(f32) + b_glu.astype(f32))
    return out.astype(u.dtype), hr[:, -1], hi[:, -1]


def _moba_attend(q, k, v, q_off, rel_bias):
    bsz, sq, nh, dh = q.shape
    f32 = jnp.float32
    seq_k = k.shape[1]
    nb = max(-(-seq_k // MOBA_BLOCK), MOBA_TOPK)
    pad = nb * MOBA_BLOCK - seq_k
    kp = jnp.pad(k, ((0, 0), (0, pad), (0, 0), (0, 0)))
    vp = jnp.pad(v, ((0, 0), (0, pad), (0, 0), (0, 0)))
    kb = kp.reshape(bsz, nb, MOBA_BLOCK, nh, dh).transpose(0, 3, 1, 2, 4)
    vb = vp.reshape(bsz, nb, MOBA_BLOCK, nh, dh).transpose(0, 3, 1, 2, 4)
    kmean = jnp.mean(kb.astype(f32), axis=3)
    c = math.gcd(sq, Q_CHUNK)
    nc = sq // c
    qc = q.reshape(bsz, nc, c, nh, dh).transpose(1, 0, 3, 2, 4)
    qpos_all = (q_off + jnp.arange(sq, dtype=jnp.int32)).reshape(nc, c)
    bias_hb = rel_bias.astype(f32).T
    b_ix = jnp.arange(bsz)[:, None, None, None]
    h_ix = jnp.arange(nh)[None, :, None, None]
    scale = dh ** -0.5

    def one_chunk(args):
        qch, qpos = args
        qf = qch.astype(f32)
        own = qpos // MOBA_BLOCK
        gate = jnp.einsum('bhcd,bhnd->bhcn', qf, kmean)
        fully_past = jnp.arange(nb)[None, :] < own[:, None]
        gate = jnp.where(fully_past, gate, -jnp.inf)
        top_s, top_i = lax.top_k(gate, MOBA_TOPK)
        idx = jnp.concatenate(
            [top_i, jnp.broadcast_to(own[None, None, :, None], (bsz, nh, c, 1))], axis=-1)
        ok = jnp.concatenate(
            [jnp.isfinite(top_s), jnp.ones((bsz, nh, c, 1), dtype=bool)], axis=-1)
        kg = kb[b_ix, h_ix, idx].astype(f32)
        vg = vb[b_ix, h_ix, idx].astype(f32)
        kpos = idx[..., None] * MOBA_BLOCK + jnp.arange(MOBA_BLOCK)
        rel = qpos[None, None, :, None, None] - kpos
        mask = ok[..., None] & (rel >= 0)
        bias = bias_hb[h_ix[..., None], _t5_bucket(rel)]
        s = jnp.einsum('bhcd,bhcjkd->bhcjk', qf, kg) * scale + bias
        s = jnp.where(mask, s, -jnp.inf)
        p = jax.nn.softmax(s.reshape(bsz, nh, c, -1), axis=-1).reshape(s.shape)
        o = jnp.einsum('bhcjk,bhcjkd->bhcd', p, vg)
        return o.astype(q.dtype)

    out = lax.map(one_chunk, (qc, qpos_all))
    return out.transpose(1, 0, 3, 2, 4).reshape(bsz, sq, nh * dh)


def _even_mixer(h, past_k, past_v, h0_re, h0_im, rel_bias, w_in, q_norm, k_norm, s5p, w_out):
    bsz, s, _ = h.shape
    proj = h @ w_in
    u = proj[..., :S5_WIDTH]
    q = proj[..., S5_WIDTH:S5_WIDTH + ATT_WIDTH].reshape(bsz, s, N_HEADS, HEAD_DIM)
    k = proj[..., S5_WIDTH + ATT_WIDTH:S5_WIDTH + 2 * ATT_WIDTH].reshape(bsz, s, N_HEADS, HEAD_DIM)
    v = proj[..., S5_WIDTH + 2 * ATT_WIDTH:].reshape(bsz, s, N_HEADS, HEAD_DIM)
    q = _rmsnorm(q, q_norm)
    k = _rmsnorm(k, k_norm)
    s5_out, hr, hi = _s5_mix(u, h0_re, h0_im, *s5p)
    k_all = jnp.concatenate([past_k.astype(k.dtype), k], axis=1)
    v_all = jnp.concatenate([past_v.astype(v.dtype), v], axis=1)
    att = _moba_attend(q, k_all, v_all, past_k.shape[1], rel_bias)
    out = jnp.concatenate([s5_out, att.astype(s5_out.dtype)], axis=-1) @ w_out
    return out, k, v, hr, hi


def _pool_mix(h, hist, offset, w_pool, scale):
    bsz, s, d = h.shape
    f32 = jnp.float32
    xx = jnp.concatenate([hist.astype(h.dtype), h], axis=1)
    pos = offset - POOL_HIST + jnp.arange(POOL_HIST + s)
    valid = (pos >= 0).astype(f32)
    xf = xx.astype(f32) * valid[None, :, None]
    cs = jnp.concatenate([jnp.zeros((bsz, 1, d), f32), jnp.cumsum(xf, axis=1)], axis=1)
    cn = jnp.concatenate([jnp.zeros((1,), f32), jnp.cumsum(valid)])
    end = POOL_HIST + 1 + jnp.arange(s)
    means = []
    for g, w in enumerate(POOL_WINDOWS):
        sl = slice(g * POOL_GW, (g + 1) * POOL_GW)
        tot = cs[:, end, sl] - cs[:, end - w, sl]
        cnt = cn[end] - cn[end - w]
        means.append(tot / cnt[None, :, None])
    pooled = jnp.concatenate(means, axis=-1) - h.astype(f32)
    y = jnp.einsum('bsgc,gcd->bsgd', pooled.reshape(bsz, s, POOL_GROUPS, POOL_GW),
                   w_pool.astype(f32)).reshape(bsz, s, d) * scale.astype(f32)
    return y.astype(h.dtype), xx[:, -POOL_HIST:]


def _moe(h, w_router, b_router, wg, wu, wd):
    f32 = jnp.float32
    logits = h.astype(f32) @ w_router.astype(f32) + b_router.astype(f32)
    top_v, top_i = lax.top_k(logits, TOP_K_EXPERTS)
    gates = jax.nn.softmax(top_v, axis=-1)
    wts = jnp.einsum('bsk,bske->bse', gates, jax.nn.one_hot(top_i, N_EXPERTS, dtype=f32))
    out = jnp.zeros(h.shape, f32)
    for e in range(N_EXPERTS):
        out = out + wts[..., e:e + 1] * _swiglu(h, wg[e], wu[e], wd[e]).astype(f32)
    return out.astype(h.dtype)


def setup_inputs(seed: int = 0) -> dict:
    key = jax.random.key(seed)
    keys = jax.random.split(key, 48)
    cnt = [0]
    f32 = jnp.float32

    def nk():
        kk = keys[cnt[0]]
        cnt[0] += 1
        return kk

    def nrm(shape, scale):
        return jax.random.normal(nk(), shape, f32) * scale

    n_pages = PAST_LEN // PAGE_SIZE
    n_used = DEC_BATCH * n_pages
    n_phys = n_used + n_used // 4
    x_prompt = nrm((BATCH, SEQ, D_MODEL), 1.0)
    x_sample = nrm((DEC_BATCH, DEC_SEQ, D_MODEL), 1.0)
    cache_k = nrm((n_phys, N_EVEN, PAGE_SIZE, N_HEADS, HEAD_DIM), 1.0)
    cache_v = nrm((n_phys, N_EVEN, PAGE_SIZE, N_HEADS, HEAD_DIM), 1.0)
    perm = jax.random.permutation(nk(), n_phys)
    page_table = perm[:n_used].reshape(DEC_BATCH, n_pages).astype(jnp.int32)
    state_s5_re = nrm((DEC_BATCH, N_EVEN, S5_GROUPS, S5_STATE), 0.5)
    state_s5_im = nrm((DEC_BATCH, N_EVEN, S5_GROUPS, S5_STATE), 0.5)
    state_pool = nrm((DEC_BATCH, N_ODD, POOL_HIST, D_MODEL), 1.0)
    rel_bias = nrm((REL_BUCKETS, N_HEADS), 0.5)
    norm_mix_e = 1.0 + nrm((N_EVEN, D_MODEL), 0.05)
    w_in_e = nrm((N_EVEN, D_MODEL, S5_WIDTH + 3 * ATT_WIDTH), D_MODEL ** -0.5)
    q_norm_e = 1.0 + nrm((N_EVEN, HEAD_DIM), 0.05)
    k_norm_e = 1.0 + nrm((N_EVEN, HEAD_DIM), 0.05)
    s5_a_re = -0.5 + nrm((N_EVEN, S5_GROUPS, S5_STATE), 0.01)
    s5_a_im = (jnp.broadcast_to(math.pi * jnp.arange(S5_STATE, dtype=f32), (N_EVEN, S5_GROUPS, S5_STATE))
               + nrm((N_EVEN, S5_GROUPS, S5_STATE), 0.01))
    s5_log_dt = jax.random.uniform(nk(), (N_EVEN, S5_GROUPS), f32, math.log(1e-3), math.log(1e-1))
    s5_b_re = nrm((N_EVEN, S5_GROUPS, S5_STATE, S5_GROUP), (2 * S5_GROUP) ** -0.5)
    s5_b_im = nrm((N_EVEN, S5_GROUPS, S5_STATE, S5_GROUP), (2 * S5_GROUP) ** -0.5)
    s5_c_re = nrm((N_EVEN, S5_GROUPS, S5_GROUP, S5_STATE), (2 * S5_STATE) ** -0.5)
    s5_c_im = nrm((N_EVEN, S5_GROUPS, S5_GROUP, S5_STATE), (2 * S5_STATE) ** -0.5)
    s5_d = nrm((N_EVEN, S5_GROUPS, S5_GROUP), 1.0)
    s5_w_glu = nrm((N_EVEN, S5_WIDTH, S5_WIDTH), S5_WIDTH ** -0.5)
    s5_b_glu = nrm((N_EVEN, S5_WIDTH), 0.01)
    w_out_e = nrm((N_EVEN, S5_WIDTH + ATT_WIDTH, D_MODEL), (S5_WIDTH + ATT_WIDTH) ** -0.5)
    norm_ffn_e = 1.0 + nrm((N_EVEN, D_MODEL), 0.05)
    ffn_w_gate = nrm((N_EVEN, D_MODEL, D_FF), D_MODEL ** -0.5)
    ffn_w_up = nrm((N_EVEN, D_MODEL, D_FF), D_MODEL ** -0.5)
    ffn_w_down = nrm((N_EVEN, D_FF, D_MODEL), D_FF ** -0.5)
    norm_mix_o = 1.0 + nrm((N_ODD, D_MODEL), 0.05)
    pool_w = nrm((N_ODD, POOL_GROUPS, POOL_GW, POOL_GW), POOL_GW ** -0.5)
    pool_scale = 1.0 + nrm((N_ODD, D_MODEL), 0.1)
    norm_ffn_o = 1.0 + nrm((N_ODD, D_MODEL), 0.05)
    router_w = nrm((N_ODD, D_MODEL, N_EXPERTS), D_MODEL ** -0.5)
    router_b = nrm((N_ODD, N_EXPERTS), 0.01)
    moe_w_gate = nrm((N_ODD, N_EXPERTS, D_MODEL, D_FF_EXPERT), D_MODEL ** -0.5)
    moe_w_up = nrm((N_ODD, N_EXPERTS, D_MODEL, D_FF_EXPERT), D_MODEL ** -0.5)
    moe_w_down = nrm((N_ODD, N_EXPERTS, D_FF_EXPERT, D_MODEL), D_FF_EXPERT ** -0.5)
    return {'x_prompt': x_prompt, 'x_sample': x_sample, 'cache_k': cache_k, 'cache_v': cache_v,
            'page_table': page_table, 'state_s5_re': state_s5_re, 'state_s5_im': state_s5_im,
            'state_pool': state_pool, 'rel_bias': rel_bias, 'norm_mix_e': norm_mix_e, 'w_in_e': w_in_e,
            'q_norm_e': q_norm_e, 'k_norm_e': k_norm_e, 's5_a_re': s5_a_re, 's5_a_im': s5_a_im,
            's5_log_dt': s5_log_dt, 's5_b_re': s5_b_re, 's5_b_im': s5_b_im, 's5_c_re': s5_c_re,
            's5_c_im': s5_c_im, 's5_d': s5_d, 's5_w_glu': s5_w_glu, 's5_b_glu': s5_b_glu,
            'w_out_e': w_out_e, 'norm_ffn_e': norm_ffn_e, 'ffn_w_gate': ffn_w_gate, 'ffn_w_up': ffn_w_up,
            'ffn_w_down': ffn_w_down, 'norm_mix_o': norm_mix_o, 'pool_w': pool_w, 'pool_scale': pool_scale,
            'norm_ffn_o': norm_ffn_o, 'router_w': router_w, 'router_b': router_b,
            'moe_w_gate': moe_w_gate, 'moe_w_up': moe_w_up, 'moe_w_down': moe_w_down}


def reference(x_prompt, x_sample, cache_k, cache_v, page_table, state_s5_re, state_s5_im, state_pool,
              rel_bias, norm_mix_e, w_in_e, q_norm_e, k_norm_e, s5_a_re, s5_a_im, s5_log_dt, s5_b_re,
              s5_b_im, s5_c_re, s5_c_im, s5_d, s5_w_glu, s5_b_glu, w_out_e, norm_ffn_e, ffn_w_gate,
              ffn_w_up, ffn_w_down, norm_mix_o, pool_w, pool_scale, norm_ffn_o, router_w, router_b,
              moe_w_gate, moe_w_up, moe_w_down):
    bp = x_prompt.shape[0]
    db = x_sample.shape[0]
    past_len = page_table.shape[1] * cache_k.shape[2]
    xp = x_prompt
    xs = x_sample
    kp_l, vp_l, ks_l, vs_l = [], [], [], []
    s5rp_l, s5ip_l, s5rs_l, s5is_l = [], [], [], []
    poolp_l, pools_l = [], []
    for l in range(DEPTH):
        if l % 2 == 0:
            e = l // 2
            s5p = (s5_a_re[e], s5_a_im[e], s5_log_dt[e], s5_b_re[e], s5_b_im[e], s5_c_re[e],
                   s5_c_im[e], s5_d[e], s5_w_glu[e], s5_b_glu[e])
            empty = jnp.zeros((bp, 0, N_HEADS, HEAD_DIM), xp.dtype)
            zst = jnp.zeros((bp, S5_GROUPS, S5_STATE), jnp.float32)
            op, kp, vp, hrp, hip = _even_mixer(_rmsnorm(xp, norm_mix_e[e]), empty, empty, zst, zst,
                                               rel_bias, w_in_e[e], q_norm_e[e], k_norm_e[e], s5p, w_out_e[e])
            xp = xp + op
            pk = cache_k[page_table, e].reshape(db, past_len, N_HEADS, HEAD_DIM)
            pv = cache_v[page_table, e].reshape(db, past_len, N_HEADS, HEAD_DIM)
            osm, ksm, vsm, hrs, his = _even_mixer(_rmsnorm(xs, norm_mix_e[e]), pk, pv,
                                                  state_s5_re[:, e], state_s5_im[:, e], rel_bias,
                                                  w_in_e[e], q_norm_e[e], k_norm_e[e], s5p, w_out_e[e])
            xs = xs + osm
            kp_l.append(kp)
            vp_l.append(vp)
            ks_l.append(ksm)
            vs_l.append(vsm)
            s5rp_l.append(hrp)
            s5ip_l.append(hip)
            s5rs_l.append(hrs)
            s5is_l.append(his)
            xp = xp + _swiglu(_rmsnorm(xp, norm_ffn_e[e]), ffn_w_gate[e], ffn_w_up[e], ffn_w_down[e])
            xs = xs + _swiglu(_rmsnorm(xs, norm_ffn_e[e]), ffn_w_gate[e], ffn_w_up[e], ffn_w_down[e])
        else:
            o = l // 2
            hist0 = jnp.zeros((bp, POOL_HIST, D_MODEL), xp.dtype)
            yp, hp_new = _pool_mix(_rmsnorm(xp, norm_mix_o[o]), hist0, 0, pool_w[o], pool_scale[o])
            xp = xp + yp
            ys, hs_new = _pool_mix(_rmsnorm(xs, norm_mix_o[o]), state_pool[:, o], past_len,
                                   pool_w[o], pool_scale[o])
            xs = xs + ys
            poolp_l.append(hp_new)
            pools_l.append(hs_new)
            xp = xp + _moe(_rmsnorm(xp, norm_ffn_o[o]), router_w[o], router_b[o],
                           moe_w_gate[o], moe_w_up[o], moe_w_down[o])
            xs = xs + _moe(_rmsnorm(xs, norm_ffn_o[o]), router_w[o], router_b[o],
                           moe_w_gate[o], moe_w_up[o], moe_w_down[o])
    return (xp, xs,
            jnp.stack(kp_l, axis=2), jnp.stack(vp_l, axis=2),
            jnp.stack(ks_l, axis=2), jnp.stack(vs_l, axis=2),
            jnp.stack(s5rp_l, axis=1), jnp.stack(s5ip_l, axis=1),
            jnp.stack(s5rs_l, axis=1), jnp.stack(s5is_l, axis=1),
            jnp.stack(poolp_l, axis=1), jnp.stack(pools_l, axis=1))
```

```python
import functools
import math

import jax
import jax.numpy as jnp
import numpy as np
from jax import lax
from jax.experimental import pallas as pl
from jax.experimental.pallas import tpu as pltpu

F32 = jnp.float32
BF16 = jnp.bfloat16
EPS = 1e-6
NEG = -1e30

S5_GROUP = 16
S5_STATE = 64
HEAD_DIM = 64
MOBA_BLOCK = 256
MOBA_TOPK = 3
REL_BUCKETS = 32
REL_MAX_DIST = 1024
POOL_WINDOWS = (2, 4, 8, 16)
POOL_HIST = max(POOL_WINDOWS) - 1
TOP_K_EXPERTS = 2

LANES = 128
SUBLANES = 8
VMEM_LIMIT = 48 * 1024 * 1024


def _cparams(*sem):
    return pltpu.CompilerParams(dimension_semantics=sem, vmem_limit_bytes=VMEM_LIMIT)


def _rmsnorm(x, g):
    ms = jnp.mean(x * x, axis=-1, keepdims=True)
    return x * lax.rsqrt(ms + EPS) * g


def _split_bf16(x):
    hi = x.astype(BF16)
    lo = (x - hi.astype(F32)).astype(BF16)
    return hi, lo


def _dot(a, b):
    return jnp.dot(a, b, preferred_element_type=F32)


def _dot_nt(a, b):
    return lax.dot_general(a, b, (((1,), (1,)), ((), ())), preferred_element_type=F32)


def _dot3(a, b):
    ah, al = _split_bf16(a)
    bh, bl = _split_bf16(b)
    return _dot(ah, bh) + _dot(ah, bl) + _dot(al, bh)


def _dot3_nt(a, b):
    ah, al = _split_bf16(a)
    bh, bl = _split_bf16(b)
    return _dot_nt(ah, bh) + _dot_nt(ah, bl) + _dot_nt(al, bh)


def _silu(x):
    return x * jax.nn.sigmoid(x)


def _in_proj_kernel(x_ref, g_ref, w_ref, hn_ref, bd_ref, o_ref, h_scr):
    j = pl.program_id(1)

    @pl.when(j == 0)
    def _():
        h_scr[...] = _rmsnorm(x_ref[...], g_ref[...]).astype(BF16)

    y = _dot(h_scr[...], w_ref[...])
    is_qk = jnp.logical_or(j == 1, j == 2)

    @pl.when(is_qk)
    def _():
        hi, lo = _split_bf16(y * y)
        ms = _dot(hi, bd_ref[...]) + _dot(lo, bd_ref[...])
        o_ref[...] = y * lax.rsqrt(ms + EPS) * hn_ref[0]

    @pl.when(jnp.logical_not(is_qk))
    def _():
        o_ref[...] = y


def _in_proj(x, g, w_bf, head_gain, tm):
    m, d = x.shape
    n = w_bf.shape[1]
    wdt = n // 4
    bd = np.kron(np.eye(wdt // HEAD_DIM), np.full((HEAD_DIM, HEAD_DIM), 1.0 / HEAD_DIM))
    bd = jnp.asarray(bd, BF16)
    return pl.pallas_call(
        _in_proj_kernel,
        grid=(m // tm, 4),
        in_specs=[
            pl.BlockSpec((tm, d), lambda i, j: (i, 0)),
            pl.BlockSpec((1, d), lambda i, j: (0, 0)),
            pl.BlockSpec((d, wdt), lambda i, j: (0, j)),
            pl.BlockSpec((1, 1, wdt), lambda i, j: (j, 0, 0)),
            pl.BlockSpec((wdt, wdt), lambda i, j: (0, 0)),
        ],
        out_specs=pl.BlockSpec((tm, wdt), lambda i, j: (i, j)),
        out_shape=jax.ShapeDtypeStruct((m, n), F32),
        scratch_shapes=[pltpu.VMEM((tm, d), BF16)],
        compiler_params=_cparams("parallel", "arbitrary"),
    )(x, g, w_bf, head_gain, bd)


def _mix_ffn_kernel(x_ref, a_ref, b_ref, wo_ref, g_ref, wg_ref, wu_ref, wd_ref, o_ref,
                    x1_scr, h_scr, acc_scr):
    f = pl.program_id(1)
    half = a_ref.shape[1]

    @pl.when(f == 0)
    def _():
        x1 = (x_ref[...] + _dot(a_ref[...], wo_ref[:half, :]) + _dot(b_ref[...], wo_ref[half:, :]))
        x1_scr[...] = x1
        h_scr[...] = _rmsnorm(x1, g_ref[...]).astype(BF16)
        acc_scr[...] = jnp.zeros_like(acc_scr)

    h = h_scr[...]
    act = (_silu(_dot(h, wg_ref[...])) * _dot(h, wu_ref[...])).astype(BF16)
    acc_scr[...] += _dot(act, wd_ref[...])

    @pl.when(f == pl.num_programs(1) - 1)
    def _():
        o_ref[...] = x1_scr[...] + acc_scr[...]


def _mix_ffn(x, a, b, wo_bf, g, wg_bf, wu_bf, wd_bf, tm, tf):
    m, d = x.shape
    half = a.shape[1]
    ff = wg_bf.shape[1]
    return pl.pallas_call(
        _mix_ffn_kernel,
        grid=(m // tm, ff // tf),
        in_specs=[
            pl.BlockSpec((tm, d), lambda i, f: (i, 0)),
            pl.BlockSpec((tm, half), lambda i, f: (i, 0)),
            pl.BlockSpec((tm, half), lambda i, f: (i, 0)),
            pl.BlockSpec((2 * half, d), lambda i, f: (0, 0)),
            pl.BlockSpec((1, d), lambda i, f: (0, 0)),
            pl.BlockSpec((d, tf), lambda i, f: (0, f)),
            pl.BlockSpec((d, tf), lambda i, f: (0, f)),
            pl.BlockSpec((tf, d), lambda i, f: (f, 0)),
        ],
        out_specs=pl.BlockSpec((tm, d), lambda i, f: (i, 0)),
        out_shape=jax.ShapeDtypeStruct((m, d), F32),
        scratch_shapes=[pltpu.VMEM((tm, d), F32), pltpu.VMEM((tm, d), BF16), pltpu.VMEM((tm, d), F32)],
        compiler_params=_cparams("parallel", "arbitrary"),
    )(x, a, b, wo_bf, g, wg_bf, wu_bf, wd_bf)


def _route(h, wr, br, n_exp):
    logits = _dot(h.astype(BF16), wr) + br
    lane = lax.broadcasted_iota(jnp.int32, logits.shape, 1)
    logits = jnp.where(lane < n_exp, logits, NEG)
    m1 = jnp.max(logits, axis=-1, keepdims=True)
    i1 = jnp.min(jnp.where(logits == m1, lane, LANES), axis=-1, keepdims=True)
    rest = jnp.where(lane == i1, NEG, logits)
    m2 = jnp.max(rest, axis=-1, keepdims=True)
    i2 = jnp.min(jnp.where(rest == m2, lane, LANES), axis=-1, keepdims=True)
    e2 = jnp.exp(m2 - m1)
    g1 = 1.0 / (1.0 + e2)
    g2 = e2 / (1.0 + e2)
    return jnp.where(lane == i1, g1, 0.0) + jnp.where(lane == i2, g2, 0.0)


def _moe_dense_kernel(x_ref, g_ref, wr_ref, br_ref, wg_ref, wu_ref, wd_ref, o_ref,
                      h_scr, gate_scr, acc_scr, *, n_exp):
    e = pl.program_id(1)
    f = pl.program_id(2)

    @pl.when(jnp.logical_and(e == 0, f == 0))
    def _():
        h = _rmsnorm(x_ref[...], g_ref[...])
        h_scr[...] = h.astype(BF16)
        gate_scr[...] = _route(h, wr_ref[...], br_ref[...], n_exp)
        acc_scr[...] = jnp.zeros_like(acc_scr)

    h = h_scr[...]
    lane = lax.broadcasted_iota(jnp.int32, gate_scr.shape, 1)
    w_e = jnp.sum(jnp.where(lane == e, gate_scr[...], 0.0), axis=-1, keepdims=True)
    act = (_silu(_dot(h, wg_ref[0])) * _dot(h, wu_ref[0])).astype(BF16)
    acc_scr[...] += w_e * _dot(act, wd_ref[0])

    @pl.when(jnp.logical_and(e == n_exp - 1, f == pl.num_programs(2) - 1))
    def _():
        o_ref[...] = x_ref[...] + acc_scr[...]


def _moe_dense(x, g, wr_pad, br_pad, wg_bf, wu_bf, wd_bf, tm, tf):
    m, d = x.shape
    n_exp, _, ff = wg_bf.shape
    return pl.pallas_call(
        functools.partial(_moe_dense_kernel, n_exp=n_exp),
        grid=(m // tm, n_exp, ff // tf),
        in_specs=[
            pl.BlockSpec((tm, d), lambda i, e, f: (i, 0)),
            pl.BlockSpec((1, d), lambda i, e, f: (0, 0)),
            pl.BlockSpec((d, LANES), lambda i, e, f: (0, 0)),
            pl.BlockSpec((1, LANES), lambda i, e, f: (0, 0)),
            pl.BlockSpec((1, d, tf), lambda i, e, f: (e, 0, f)),
            pl.BlockSpec((1, d, tf), lambda i, e, f: (e, 0, f)),
            pl.BlockSpec((1, tf, d), lambda i, e, f: (e, f, 0)),
        ],
        out_specs=pl.BlockSpec((tm, d), lambda i, e, f: (i, 0)),
        out_shape=jax.ShapeDtypeStruct((m, d), F32),
        scratch_shapes=[pltpu.VMEM((tm, d), BF16), pltpu.VMEM((tm, LANES), F32), pltpu.VMEM((tm, d), F32)],
        compiler_params=_cparams("parallel", "arbitrary", "arbitrary"),
    )(x, g, wr_pad, br_pad, wg_bf, wu_bf, wd_bf)


HALO = 16


def _pool_kernel(x_ref, g_ref, wp_ref, sc_ref, o_ref, hist_ref, hbuf):
    t = pl.program_id(1)
    tm = x_ref.shape[1]
    gw = wp_ref.shape[1]

    @pl.when(t == 0)
    def _():
        hbuf[0:HALO, :] = jnp.zeros((HALO, hbuf.shape[1]), F32)

    @pl.when(t > 0)
    def _():
        hbuf[0:HALO, :] = hbuf[tm:tm + HALO, :]

    x = x_ref[0]
    hn = _rmsnorm(x, g_ref[...])
    hbuf[HALO:HALO + tm, :] = hn
    pos = t * tm + lax.broadcasted_iota(jnp.int32, (tm, 1), 0)
    ys = []
    for gi, w in enumerate(POOL_WINDOWS):
        sl = slice(gi * gw, (gi + 1) * gw)
        tot = hn[:, sl]
        for i in range(1, w):
            tot = tot + hbuf[HALO - i:HALO - i + tm, sl]
        cnt = jnp.minimum(pos + 1, w).astype(F32)
        pooled = tot / cnt - hn[:, sl]
        ys.append(_dot(pooled.astype(BF16), wp_ref[gi]))
    o_ref[0] = x + jnp.concatenate(ys, axis=1) * sc_ref[...]

    @pl.when(t == pl.num_programs(1) - 1)
    def _():
        hist_ref[0] = hbuf[tm:tm + HALO, :]


def _pool_prompt(x, g, wp_bf, sc, tm):
    b, s, d = x.shape
    ng, gw, _ = wp_bf.shape
    return pl.pallas_call(
        _pool_kernel,
        grid=(b, s // tm),
        in_specs=[
            pl.BlockSpec((1, tm, d), lambda i, t: (i, t, 0)),
            pl.BlockSpec((1, d), lambda i, t: (0, 0)),
            pl.BlockSpec((ng, gw, gw), lambda i, t: (0, 0, 0)),
            pl.BlockSpec((1, d), lambda i, t: (0, 0)),
        ],
        out_specs=[
            pl.BlockSpec((1, tm, d), lambda i, t: (i, t, 0)),
            pl.BlockSpec((1, HALO, d), lambda i, t: (i, 0, 0)),
        ],
        out_shape=[jax.ShapeDtypeStruct((b, s, d), F32), jax.ShapeDtypeStruct((b, HALO, d), F32)],
        scratch_shapes=[pltpu.VMEM((HALO + tm, d), F32)],
        compiler_params=_cparams("parallel", "arbitrary"),
    )(x, g, wp_bf, sc)


def _pool_step_kernel(x_ref, g_ref, hist_ref, wp_ref, sc_ref, o_ref, hn_ref, *, n_valid):
    gw = wp_ref.shape[1]
    x = x_ref[...]
    hn = _rmsnorm(x, g_ref[...])
    hn_ref[...] = hn
    ys = []
    for gi, w in enumerate(POOL_WINDOWS):
        sl = slice(gi * gw, (gi + 1) * gw)
        tot = hn[:, sl]
        for i in range(1, min(w, n_valid + 1)):
            tot = tot + hist_ref[POOL_HIST - i][:, sl]
        pooled = tot / float(min(w, n_valid + 1)) - hn[:, sl]
        ys.append(_dot(pooled.astype(BF16), wp_ref[gi]))
    o_ref[...] = x + jnp.concatenate(ys, axis=1) * sc_ref[...]


def _pool_step(x, g, hist_t, wp_bf, sc, n_valid):
    b, d = x.shape
    return pl.pallas_call(
        functools.partial(_pool_step_kernel, n_valid=n_valid),
        out_shape=[jax.ShapeDtypeStruct((b, d), F32), jax.ShapeDtypeStruct((b, d), F32)],
        compiler_params=pltpu.CompilerParams(vmem_limit_bytes=VMEM_LIMIT),
    )(x, g, hist_t, wp_bf, sc)


N_POW = SUBLANES


def _s5_prep_kernel(ar_ref, ai_ref, ldt_ref, br_ref, bi_ref, pwr_ref, pwi_ref, bbr_ref, bbi_ref):
    ar = ar_ref[...]
    ai = ai_ref[...]
    dt = jnp.exp(ldt_ref[...])
    mag = jnp.exp(ar * dt)
    lr = mag * jnp.cos(ai * dt)
    li = mag * jnp.sin(ai * dt)
    den = ar * ar + ai * ai
    rr = ((lr - 1.0) * ar + li * ai) / den
    ri = (li * ar - (lr - 1.0) * ai) / den
    br = br_ref[...]
    bi = bi_ref[...]
    bbr_ref[...] = rr * br - ri * bi
    bbi_ref[...] = rr * bi + ri * br
    cr, ci = lr, li
    for k in range(N_POW):
        pwr_ref[k] = cr
        pwi_ref[k] = ci
        cr, ci = cr * lr - ci * li, cr * li + ci * lr


def _s5_prep(a_re, a_im, log_dt, b_re, b_im):
    g, n = a_re.shape
    p = b_re.shape[2]
    f = jax.ShapeDtypeStruct
    return pl.pallas_call(
        _s5_prep_kernel,
        out_shape=[f((N_POW, g, 1, n), F32), f((N_POW, g, 1, n), F32), f((g, p, n), F32), f((g, p, n), F32)],
    )(a_re[:, None, :], a_im[:, None, :], log_dt[:, None, None],
      jnp.transpose(b_re, (0, 2, 1)), jnp.transpose(b_im, (0, 2, 1)))


def _s5_weights(a_re, a_im, log_dt, b_re, b_im, c_re, c_im):
    g, n = a_re.shape
    p = b_re.shape[2]
    gn = g * n
    pwr, pwi, bbr, bbi = _s5_prep(a_re, a_im, log_dt, b_re, b_im)
    pwr = pwr.reshape(N_POW, gn)
    pwi = pwi.reshape(N_POW, gn)
    row = np.arange(SUBLANES)[:, None]
    tabs = []
    for k in (1, 2, 4):
        tabs += [jnp.where(row >= k, pwr[k - 1][None], 0.0), jnp.where(row >= k, pwi[k - 1][None], 0.0)]
    tabs += [pwr, pwi]
    tabs = jnp.stack(tabs)
    lam = jnp.stack([pwr[0], pwi[0]])
    eye = jnp.eye(g, dtype=F32)
    b_dense = [jnp.einsum('gpn,gh->gphn', bb, eye).reshape(g * p, gn) for bb in (bbr, bbi)]
    n_tiles = gn // LANES
    u_per_tile = LANES // n * p
    wb = []
    for j in range(n_tiles):
        c0 = (j * u_per_tile) // LANES * LANES
        wb.append(jnp.concatenate([bd[c0:c0 + LANES, j * LANES:(j + 1) * LANES] for bd in b_dense], axis=1))
    wb = jnp.stack(wb).astype(BF16)
    c_dense = [jnp.einsum('gpn,gh->gnhp', cc, eye).reshape(gn, g * p) for cc in (c_re, -c_im)]
    k_per_tile = LANES // p * n
    wc = []
    for m in range(g * p // LANES):
        wc.append(jnp.concatenate([cd[m * k_per_tile:(m + 1) * k_per_tile, m * LANES:(m + 1) * LANES]
                                   for cd in c_dense], axis=0))
    wc = jnp.stack(wc).astype(BF16)
    return wb, tabs, lam, wc


def _s5_drive(u, wb_ref, bur, bui, row0):
    t = u.shape[0]
    ub = u.astype(BF16)
    n_tiles = wb_ref.shape[0]
    u_per_tile = u.shape[1] // n_tiles
    for j in range(n_tiles):
        c0 = (j * u_per_tile) // LANES * LANES
        r = _dot(ub[:, c0:c0 + LANES], wb_ref[j])
        bur[row0:row0 + t, j * LANES:(j + 1) * LANES] = r[:, :LANES]
        bui[row0:row0 + t, j * LANES:(j + 1) * LANES] = r[:, LANES:]


def _s5_readout(u, hr, hi, wc_ref, d_ref, wgl_ref, bgl_ref):
    hrb = hr.astype(BF16)
    hib = hi.astype(BF16)
    kk = wc_ref.shape[1] // 2
    ys = []
    for m in range(wc_ref.shape[0]):
        ys.append(_dot(hrb[:, m * kk:(m + 1) * kk], wc_ref[m, :kk, :])
                  + _dot(hib[:, m * kk:(m + 1) * kk], wc_ref[m, kk:, :]))
    y = jnp.concatenate(ys, axis=1) + d_ref[...] * u
    z = jax.nn.gelu(y, approximate=True)
    return z * jax.nn.sigmoid(_dot(z.astype(BF16), wgl_ref[...]) + bgl_ref[...])


def _s5_scan_kernel(u_ref, h0_ref, wb_ref, tab_ref, wc_ref, d_ref, wgl_ref, bgl_ref, o_ref, hf_ref, bur, bui):
    t = pl.program_id(1)
    tt = u_ref.shape[1]
    gn = bur.shape[1]
    c0 = SUBLANES

    @pl.when(t == 0)
    def _():
        bur[0:c0, :] = jnp.broadcast_to(h0_ref[0, :, :gn], (c0, gn))
        bui[0:c0, :] = jnp.broadcast_to(h0_ref[0, :, gn:], (c0, gn))

    @pl.when(t > 0)
    def _():
        bur[0:c0, :] = bur[tt:tt + c0, :]
        bui[0:c0, :] = bui[tt:tt + c0, :]

    u = u_ref[0]
    _s5_drive(u, wb_ref, bur, bui, c0)

    def block(b, carry):
        base = pl.multiple_of(b * SUBLANES, SUBLANES)
        cr = bur[pl.ds(base + c0 - 1, 1), :]
        ci = bui[pl.ds(base + c0 - 1, 1), :]
        xr = bur[pl.ds(base + c0, SUBLANES), :]
        xi = bui[pl.ds(base + c0, SUBLANES), :]
        for s, k in enumerate((1, 2, 4)):
            ar = tab_ref[2 * s]
            ai = tab_ref[2 * s + 1]
            sr = pltpu.roll(xr, k, axis=0)
            si = pltpu.roll(xi, k, axis=0)
            xr, xi = xr + ar * sr - ai * si, xi + ar * si + ai * sr
        pr = tab_ref[6]
        pi = tab_ref[7]
        bur[pl.ds(base + c0, SUBLANES), :] = xr + pr * cr - pi * ci
        bui[pl.ds(base + c0, SUBLANES), :] = xi + pr * ci + pi * cr
        return carry

    lax.fori_loop(0, tt // SUBLANES, block, 0)
    o_ref[0] = _s5_readout(u, bur[c0:c0 + tt, :], bui[c0:c0 + tt, :], wc_ref, d_ref, wgl_ref, bgl_ref).astype(BF16)

    @pl.when(t == pl.num_programs(1) - 1)
    def _():
        hf_ref[0] = jnp.concatenate([bur[tt + c0 - 1:tt + c0, :], bui[tt + c0 - 1:tt + c0, :]], axis=1)


def _s5_scan(proj, h0, wb, tabs, wc, dsk, wgl_bf, bgl, tt):
    b, s, _ = proj.shape
    w = dsk.shape[1]
    gn = tabs.shape[2]
    full = lambda a: pl.BlockSpec(a.shape, lambda i, t: (0,) * a.ndim)
    return pl.pallas_call(
        _s5_scan_kernel,
        grid=(b, s // tt),
        in_specs=[
            pl.BlockSpec((1, tt, w), lambda i, t: (i, t, 0)),
            pl.BlockSpec((1, 1, 2 * gn), lambda i, t: (i, 0, 0)),
            full(wb), full(tabs), full(wc), full(dsk), full(wgl_bf), full(bgl),
        ],
        out_specs=[
            pl.BlockSpec((1, tt, w), lambda i, t: (i, t, 0)),
            pl.BlockSpec((1, 1, 2 * gn), lambda i, t: (i, 0, 0)),
        ],
        out_shape=[jax.ShapeDtypeStruct((b, s, w), BF16), jax.ShapeDtypeStruct((b, 1, 2 * gn), F32)],
        scratch_shapes=[pltpu.VMEM((tt + SUBLANES, gn), F32), pltpu.VMEM((tt + SUBLANES, gn), F32)],
        compiler_params=_cparams("parallel", "arbitrary"),
    )(proj, h0, wb, tabs, wc, dsk, wgl_bf, bgl)


def _s5_step_kernel(u_ref, h0r_ref, h0i_ref, wb_ref, lam_ref, wc_ref, d_ref, wgl_ref, bgl_ref,
                    o_ref, hr_ref, hi_ref, bur, bui):
    u = u_ref[...]
    _s5_drive(u, wb_ref, bur, bui, 0)
    lr = lam_ref[0:1, :]
    li = lam_ref[1:2, :]
    h0r = h0r_ref[...]
    h0i = h0i_ref[...]
    hr = bur[...] + lr * h0r - li * h0i
    hi = bui[...] + lr * h0i + li * h0r
    hr_ref[...] = hr
    hi_ref[...] = hi
    o_ref[...] = _s5_readout(u, hr, hi, wc_ref, d_ref, wgl_ref, bgl_ref).astype(BF16)


def _s5_step(proj, h0r, h0i, wb, lam, wc, dsk, wgl_bf, bgl):
    b = proj.shape[0]
    w = dsk.shape[1]
    gn = lam.shape[1]
    full = lambda a: pl.BlockSpec(a.shape, lambda i: (0,) * a.ndim)
    return pl.pallas_call(
        _s5_step_kernel,
        grid=(1,),
        in_specs=[pl.BlockSpec((b, w), lambda i: (0, 0)), full(h0r), full(h0i), full(wb), full(lam), full(wc),
                  full(dsk), full(wgl_bf), full(bgl)],
        out_specs=[pl.BlockSpec((b, w), lambda i: (0, 0)), pl.BlockSpec((b, gn), lambda i: (0, 0)),
                   pl.BlockSpec((b, gn), lambda i: (0, 0))],
        out_shape=[jax.ShapeDtypeStruct((b, w), BF16), jax.ShapeDtypeStruct((b, gn), F32),
                   jax.ShapeDtypeStruct((b, gn), F32)],
        scratch_shapes=[pltpu.VMEM((b, gn), F32), pltpu.VMEM((b, gn), F32)],
        compiler_params=_cparams("arbitrary"),
    )(proj, h0r, h0i, wb, lam, wc, dsk, wgl_bf, bgl)


def _t5_bucket(rel):
    n = jnp.maximum(rel, 0)
    max_exact = REL_BUCKETS // 2
    nf = jnp.maximum(n, 1).astype(F32)
    large = max_exact + (jnp.log(nf / max_exact) / math.log(REL_MAX_DIST / max_exact)
                         * (REL_BUCKETS - max_exact)).astype(jnp.int32)
    large = jnp.minimum(large, REL_BUCKETS - 1)
    return jnp.where(n < max_exact, n, large)


N_DIST = -(-(REL_MAX_DIST + MOBA_BLOCK - 1) // MOBA_BLOCK) + 1


def _prompt_bias_tiles(rel_bias):
    d = np.arange(N_DIST)[:, None, None]
    r = np.arange(MOBA_BLOCK)[None, :, None]
    c = np.arange(MOBA_BLOCK)[None, None, :]
    rel = jnp.asarray(d * MOBA_BLOCK + r - c, jnp.int32)
    tiles = rel_bias.astype(F32).T[:, _t5_bucket(rel)]
    return jnp.where(rel >= 0, tiles, NEG)


def _rank_select(g, n_valid, idx, n_cand, axis):
    cnt = jnp.zeros(g.shape, jnp.int32)
    for jp in range(n_cand):
        gp = lax.slice_in_dim(g, jp, jp + 1, axis=axis)
        beats = jnp.logical_or(gp > g, jnp.logical_and(gp == g, jp < idx))
        cnt = cnt + jnp.where(jnp.logical_and(beats, jp < n_valid), 1, 0)
    return jnp.logical_and(idx < n_valid, cnt < MOBA_TOPK)


def _moba_kernel(q_ref, k_ref, v_ref, bias_ref, o_ref, kf, vf, km, qa, m_scr, acc_scr):
    i = pl.program_id(2)
    blk = MOBA_BLOCK
    nb = k_ref.shape[1] // blk
    scale = HEAD_DIM ** -0.5

    @pl.when(i == 0)
    def _():
        k = k_ref[0]
        kf[:, :LANES] = k.astype(BF16)
        row_blk = lax.broadcasted_iota(jnp.int32, (k.shape[0], LANES), 0) // blk
        lane = lax.broadcasted_iota(jnp.int32, (k.shape[0], LANES), 1)
        kf[:, LANES:] = jnp.where(lane == row_blk, 1.0, 0.0).astype(BF16)
        vf[:, :LANES] = v_ref[0].astype(BF16)
        vf[:, LANES:] = jnp.ones((k.shape[0], LANES), BF16)
        for j in range(nb):
            km[j:j + 1, :] = jnp.mean(k[j * blk:(j + 1) * blk, :], axis=0, keepdims=True)

    own = pl.multiple_of(i * blk, blk)
    outs = []
    for hd in range(LANES // HEAD_DIM):
        lane = lax.broadcasted_iota(jnp.int32, (blk, LANES), 1)
        qm = (jnp.where(lane // HEAD_DIM == hd, q_ref[0], 0.0) * scale).astype(BF16)
        g = _dot_nt(km[...].astype(BF16), qm)
        jrow = lax.broadcasted_iota(jnp.int32, g.shape, 0)
        sel = _rank_select(g, i, jrow, nb, axis=0)
        pen = jnp.where(sel, 0.0, NEG)
        pen = jnp.concatenate([pen, jnp.zeros((LANES - nb, blk), F32)], axis=0)
        qa[:, :LANES] = qm
        qa[:, LANES:] = pen.T.astype(BF16)

        s = _dot_nt(qa[:, :LANES], kf[pl.ds(own, blk), :LANES]) + bias_ref[hd, 0]
        m0 = jnp.max(s, axis=-1, keepdims=True)
        m_scr[...] = m0
        acc_scr[...] = _dot(jnp.exp(s - m0).astype(BF16), vf[pl.ds(own, blk), :])

        def past_block(j, carry):
            r0 = pl.multiple_of(j * blk, blk)
            dist = jnp.minimum(i - j, N_DIST - 1)
            s = _dot_nt(qa[...], kf[pl.ds(r0, blk), :]) + bias_ref[hd, dist]
            m_old = m_scr[...]
            m_new = jnp.maximum(m_old, jnp.max(s, axis=-1, keepdims=True))
            p = jnp.exp(s - m_new)
            acc_scr[...] = jnp.exp(m_old - m_new) * acc_scr[...] + _dot(p.astype(BF16), vf[pl.ds(r0, blk), :])
            m_scr[...] = m_new
            return carry

        lax.fori_loop(0, i, past_block, 0)
        acc = acc_scr[...]
        outs.append(acc[:, :LANES] / acc[:, LANES:])
    lane = lax.broadcasted_iota(jnp.int32, (blk, LANES), 1)
    o = outs[0]
    for hd in range(1, len(outs)):
        o = jnp.where(lane // HEAD_DIM == hd, outs[hd], o)
    o_ref[0] = o.astype(BF16)


def _moba_prompt(proj, bias_tiles):
    b, s, n = proj.shape
    w = n // 4
    hp = w // LANES
    heads_per = LANES // HEAD_DIM
    blk = MOBA_BLOCK
    assert s % blk == 0 and s // blk <= LANES
    return pl.pallas_call(
        _moba_kernel,
        grid=(b, hp, s // blk),
        in_specs=[
            pl.BlockSpec((1, blk, LANES), lambda bi, h, i: (bi, i, hp + h)),
            pl.BlockSpec((1, s, LANES), lambda bi, h, i: (bi, 0, 2 * hp + h)),
            pl.BlockSpec((1, s, LANES), lambda bi, h, i: (bi, 0, 3 * hp + h)),
            pl.BlockSpec((heads_per, N_DIST, blk, blk), lambda bi, h, i: (h, 0, 0, 0)),
        ],
        out_specs=pl.BlockSpec((1, blk, LANES), lambda bi, h, i: (bi, i, h)),
        out_shape=jax.ShapeDtypeStruct((b, s, w), BF16),
        scratch_shapes=[
            pltpu.VMEM((s, 2 * LANES), BF16), pltpu.VMEM((s, 2 * LANES), BF16),
            pltpu.VMEM((s // blk, LANES), F32), pltpu.VMEM((blk, 2 * LANES), BF16),
            pltpu.VMEM((blk, 1), F32), pltpu.VMEM((blk, 2 * LANES), F32),
        ],
        compiler_params=_cparams("parallel", "parallel", "arbitrary"),
    )(proj, proj, proj, bias_tiles)


def _dec_partial_kernel(pt_ref, q_ref, k0_ref, k1_ref, v0_ref, v1_ref, bias_ref, o_ref, ml_ref, ks_ref):
    del pt_ref
    nh = q_ref.shape[1]
    k0 = k0_ref[0, 0]
    k1 = k1_ref[0, 0]
    rows = k0.shape[0] * nh
    kf = jnp.concatenate([k0.reshape(rows, HEAD_DIM), k1.reshape(rows, HEAD_DIM)], axis=0)
    vf = jnp.concatenate([v0_ref[0, 0].reshape(rows, HEAD_DIM), v1_ref[0, 0].reshape(rows, HEAD_DIM)], axis=0)
    q = (q_ref[0] * HEAD_DIM ** -0.5).astype(BF16)
    s = _dot_nt(q, kf.astype(BF16)) + bias_ref[0]
    col = lax.broadcasted_iota(jnp.int32, s.shape, 1)
    head = lax.broadcasted_iota(jnp.int32, s.shape, 0)
    mine = col % nh == head
    s = jnp.where(mine, s, NEG)
    m = jnp.max(s, axis=-1, keepdims=True)
    p = jnp.where(mine, jnp.exp(s - m), 0.0)
    l = jnp.sum(p, axis=-1, keepdims=True)
    o_ref[0, 0] = _dot(p.astype(BF16), vf.astype(BF16))
    lane = lax.broadcasted_iota(jnp.int32, (nh, LANES), 1)
    ml_ref[0, 0] = jnp.where(lane < LANES // 2, m, l)
    ks_ref[0, 0] = jnp.sum(k0, axis=0) + jnp.sum(k1, axis=0)


def _dec_partials(q, cache_k, cache_v, page_table, bias_dec, layer):
    b, nh, dh = q.shape
    page = cache_k.shape[2]
    assert MOBA_BLOCK == 2 * page
    nblk = page_table.shape[1] // 2
    pg = lambda off: pl.BlockSpec((1, 1, page, nh, dh), lambda bi, j, pt: (pt[bi, 2 * j + off], layer, 0, 0, 0))
    out = lambda last: pl.BlockSpec((1, 1, nh, last), lambda bi, j, pt: (bi, j, 0, 0))
    return pl.pallas_call(
        _dec_partial_kernel,
        grid_spec=pltpu.PrefetchScalarGridSpec(
            num_scalar_prefetch=1,
            grid=(b, nblk),
            in_specs=[
                pl.BlockSpec((1, nh, dh), lambda bi, j, pt: (bi, 0, 0)),
                pg(0), pg(1), pg(0), pg(1),
                pl.BlockSpec((1, 1, MOBA_BLOCK * nh), lambda bi, j, pt: (j, 0, 0)),
            ],
            out_specs=[out(dh), out(LANES), out(dh)],
        ),
        out_shape=[jax.ShapeDtypeStruct((b, nblk, nh, dh), F32), jax.ShapeDtypeStruct((b, nblk, nh, LANES), F32),
                   jax.ShapeDtypeStruct((b, nblk, nh, dh), F32)],
        compiler_params=_cparams("parallel", "arbitrary"),
    )(page_table, q, cache_k, cache_k, cache_v, cache_v, bias_dec)


def _dec_combine_kernel(q_ref, kn_ref, vn_ref, b0_ref, ks_ref, o_ref, m_ref, l_ref, out_ref):
    nblk = m_ref.shape[1]
    rnd = lambda a: a.astype(BF16).astype(F32)
    q = rnd(q_ref[...] * HEAD_DIM ** -0.5)
    lane = lax.broadcasted_iota(jnp.int32, m_ref.shape, 1)
    gate = jnp.zeros(m_ref.shape, F32)
    for j in range(nblk):
        kmean = rnd(ks_ref[:, j * HEAD_DIM:(j + 1) * HEAD_DIM] * (1.0 / MOBA_BLOCK))
        gate = jnp.where(lane == j, jnp.sum(q * kmean, axis=-1, keepdims=True), gate)
    sel = _rank_select(gate, nblk, lane, nblk, axis=1)
    s_own = jnp.sum(q * rnd(kn_ref[...]), axis=-1, keepdims=True) + b0_ref[:, 0:1]
    mm = m_ref[...]
    m_all = jnp.maximum(jnp.max(jnp.where(sel, mm, NEG), axis=-1, keepdims=True), s_own)
    wgt = jnp.where(sel, jnp.exp(mm - m_all), 0.0)
    w_own = jnp.exp(s_own - m_all)
    den = jnp.sum(wgt * l_ref[...], axis=-1, keepdims=True) + w_own
    acc = w_own * vn_ref[...]
    for j in range(nblk):
        acc = acc + wgt[:, j:j + 1] * o_ref[:, j * HEAD_DIM:(j + 1) * HEAD_DIM]
    out_ref[...] = acc / den


def _moba_decode(proj_s, cache_k, cache_v, page_table, rel_bias, layer):
    b, n = proj_s.shape
    w = n // 4
    nh = w // HEAD_DIM
    page = cache_k.shape[2]
    past_len = page_table.shape[1] * page
    assert past_len % MOBA_BLOCK == 0
    nblk = past_len // MOBA_BLOCK
    q = proj_s[:, w:2 * w].reshape(b, nh, HEAD_DIM)
    kpos = np.arange(past_len).reshape(nblk, MOBA_BLOCK)
    bias = rel_bias.astype(F32).T[:, _t5_bucket(jnp.asarray(past_len - kpos, jnp.int32))]
    bias_dec = jnp.transpose(bias, (1, 2, 0)).reshape(nblk, 1, MOBA_BLOCK * nh)
    o_p, ml_p, ks_p = _dec_partials(q, cache_k, cache_v, page_table, bias_dec, layer)
    rows = b * nh
    to_rows = lambda a: jnp.transpose(a, (0, 2, 1, 3)).reshape(rows, nblk * a.shape[3])
    m_p = jnp.transpose(ml_p[..., 0], (0, 2, 1)).reshape(rows, nblk)
    l_p = jnp.transpose(ml_p[..., LANES - 1], (0, 2, 1)).reshape(rows, nblk)
    b0 = jnp.broadcast_to(jnp.tile(rel_bias.astype(F32)[_t5_bucket(jnp.zeros((), jnp.int32))], b)[:, None],
                          (rows, LANES))
    att = pl.pallas_call(
        _dec_combine_kernel,
        out_shape=jax.ShapeDtypeStruct((rows, HEAD_DIM), F32),
        compiler_params=pltpu.CompilerParams(vmem_limit_bytes=VMEM_LIMIT),
    )(q.reshape(rows, HEAD_DIM), proj_s[:, 2 * w:3 * w].reshape(rows, HEAD_DIM),
      proj_s[:, 3 * w:].reshape(rows, HEAD_DIM), b0, to_rows(ks_p), to_rows(o_p), m_p, l_p)
    return att.reshape(b, w).astype(BF16)


def _tile(m, pref):
    t = min(m, pref)
    while m % t:
        t //= 2
    return t


def _ff_tile(ff, pref):
    best = LANES
    for t in range(LANES, pref + 1, LANES):
        if ff % t == 0:
            best = t
    return best


def kernel(x_prompt, x_sample, cache_k, cache_v, page_table, state_s5_re, state_s5_im, state_pool, rel_bias,
           norm_mix_e, w_in_e, q_norm_e, k_norm_e, s5_a_re, s5_a_im, s5_log_dt, s5_b_re, s5_b_im, s5_c_re,
           s5_c_im, s5_d, s5_w_glu, s5_b_glu, w_out_e, norm_ffn_e, ffn_w_gate, ffn_w_up, ffn_w_down,
           norm_mix_o, pool_w, pool_scale, norm_ffn_o, router_w, router_b, moe_w_gate, moe_w_up, moe_w_down):
    bp, seq, d = x_prompt.shape
    db = x_sample.shape[0]
    assert x_sample.shape[1] == 1
    depth = norm_mix_e.shape[0] + norm_mix_o.shape[0]
    wdt = w_in_e.shape[2] // 4
    nh = wdt // HEAD_DIM
    past_len = page_table.shape[1] * cache_k.shape[2]
    n_exp = router_w.shape[2]
    gn = s5_a_re.shape[1] * s5_a_re.shape[2]

    mp = bp * seq
    xp = x_prompt.reshape(mp, d)
    xs = x_sample.reshape(db, d)
    tm_p = _tile(mp, 512)
    bias_tiles = _prompt_bias_tiles(rel_bias)
    row = lambda v: v.reshape(1, -1).astype(F32)

    kp_l, vp_l, ks_l, vs_l = [], [], [], []
    s5p_l, s5s_re_l, s5s_im_l = [], [], []
    poolp_l, pools_l = [], []
    for layer in range(depth):
        if layer % 2 == 0:
            e = layer // 2
            w_in = w_in_e[e].astype(BF16)
            ones = jnp.ones((wdt,), F32)
            head_gain = jnp.stack([ones, jnp.tile(q_norm_e[e].astype(F32), nh),
                                   jnp.tile(k_norm_e[e].astype(F32), nh), ones]).reshape(4, 1, wdt)
            wb, tabs, lam, wc = _s5_weights(s5_a_re[e], s5_a_im[e], s5_log_dt[e], s5_b_re[e], s5_b_im[e],
                                            s5_c_re[e], s5_c_im[e])
            dsk = row(s5_d[e])
            wgl = s5_w_glu[e].astype(BF16)
            bgl = row(s5_b_glu[e])
            w_out = w_out_e[e].astype(BF16)
            wg, wu, wd = ffn_w_gate[e].astype(BF16), ffn_w_up[e].astype(BF16), ffn_w_down[e].astype(BF16)
            tf = _ff_tile(wg.shape[1], 1536)

            proj = _in_proj(xp, row(norm_mix_e[e]), w_in, head_gain, tm_p)
            proj3 = proj.reshape(bp, seq, 4 * wdt)
            s5_out, hfin = _s5_scan(proj3, jnp.zeros((bp, 1, 2 * gn), F32), wb, tabs, wc, dsk, wgl, bgl,
                                    _tile(seq, 256))
            att = _moba_prompt(proj3, bias_tiles)
            xp = _mix_ffn(xp, s5_out.reshape(mp, wdt), att.reshape(mp, wdt), w_out, row(norm_ffn_e[e]),
                          wg, wu, wd, tm_p, tf)
            kp_l.append(proj3[:, :, 2 * wdt:3 * wdt].reshape(bp, seq, nh, HEAD_DIM))
            vp_l.append(proj3[:, :, 3 * wdt:].reshape(bp, seq, nh, HEAD_DIM))
            s5p_l.append(hfin.reshape(bp, 2, -1, S5_STATE))

            proj_s = _in_proj(xs, row(norm_mix_e[e]), w_in, head_gain, db)
            s5_out_s, hr_s, hi_s = _s5_step(proj_s, state_s5_re[:, e].reshape(db, gn).astype(F32),
                                            state_s5_im[:, e].reshape(db, gn).astype(F32),
                                            wb, lam, wc, dsk, wgl, bgl)
            att_s = _moba_decode(proj_s, cache_k, cache_v, page_table, rel_bias, e)
            xs = _mix_ffn(xs, s5_out_s, att_s, w_out, row(norm_ffn_e[e]), wg, wu, wd, db, tf)
            ks_l.append(proj_s[:, 2 * wdt:3 * wdt].reshape(db, 1, nh, HEAD_DIM))
            vs_l.append(proj_s[:, 3 * wdt:].reshape(db, 1, nh, HEAD_DIM))
            s5s_re_l.append(hr_s.reshape(db, -1, S5_STATE))
            s5s_im_l.append(hi_s.reshape(db, -1, S5_STATE))
        else:
            o = layer // 2
            wp = pool_w[o].astype(BF16)
            sc = row(pool_scale[o])
            wr = jnp.zeros((d, LANES), BF16).at[:, :n_exp].set(router_w[o].astype(BF16))
            br = jnp.zeros((1, LANES), F32).at[:, :n_exp].set(router_b[o].astype(F32))
            wg, wu, wd = moe_w_gate[o].astype(BF16), moe_w_up[o].astype(BF16), moe_w_down[o].astype(BF16)
            tf = _ff_tile(wg.shape[2], 1792)

            xp3, hist_p = _pool_prompt(xp.reshape(bp, seq, d), row(norm_mix_o[o]), wp, sc, _tile(seq, 512))
            xp = _moe_dense(xp3.reshape(mp, d), row(norm_ffn_o[o]), wr, br, wg, wu, wd, tm_p, tf)
            poolp_l.append(hist_p[:, HALO - POOL_HIST:])

            hist_s = state_pool[:, o].astype(F32)
            xs, hn_s = _pool_step(xs, row(norm_mix_o[o]), jnp.transpose(hist_s, (1, 0, 2)), wp, sc,
                                  min(POOL_HIST, past_len))
            xs = _moe_dense(xs, row(norm_ffn_o[o]), wr, br, wg, wu, wd, db, tf)
            pools_l.append(jnp.concatenate([hist_s[:, 1:], hn_s[:, None]], axis=1))

    s5p = jnp.stack(s5p_l, axis=1)
    return (xp.reshape(bp, seq, d), xs.reshape(db, 1, d),
            jnp.stack(kp_l, axis=2), jnp.stack(vp_l, axis=2),
            jnp.stack(ks_l, axis=2), jnp.stack(vs_l, axis=2),
            s5p[:, :, 0], s5p[:, :, 1],
            jnp.stack(s5s_re_l, axis=1), jnp.stack(s5s_im_l, axis=1),
            jnp.stack(poolp_l, axis=1), jnp.stack(pools_l, axis=1))
```

```python
import functools
import math

import jax
import jax.numpy as jnp
import numpy as np
from jax import lax
from jax.experimental import pallas as pl
from jax.experimental.pallas import tpu as pltpu

F32 = jnp.float32
BF16 = jnp.bfloat16
EPS = 1e-6
NEG = -1e30

S5_GROUP = 16
S5_STATE = 64
HEAD_DIM = 64
MOBA_BLOCK = 256
MOBA_TOPK = 3
REL_BUCKETS = 32
REL_MAX_DIST = 1024
POOL_WINDOWS = (2, 4, 8, 16)
POOL_HIST = max(POOL_WINDOWS) - 1
TOP_K_EXPERTS = 2

LANES = 128
SUBLANES = 8
VMEM_LIMIT = 48 * 1024 * 1024


def _cparams(*sem):
    return pltpu.CompilerParams(dimension_semantics=sem, vmem_limit_bytes=VMEM_LIMIT)


def _rmsnorm(x, g):
    ms = jnp.mean(x * x, axis=-1, keepdims=True)
    return x * lax.rsqrt(ms + EPS) * g


def _split_bf16(x):
    hi = x.astype(BF16)
    lo = (x - hi.astype(F32)).astype(BF16)
    return hi, lo


def _dot(a, b):
    return jnp.dot(a, b, preferred_element_type=F32)


def _dot_nt(a, b):
    return lax.dot_general(a, b, (((1,), (1,)), ((), ())), preferred_element_type=F32)


def _dot3(a, b):
    ah, al = _split_bf16(a)
    bh, bl = _split_bf16(b)
    return _dot(ah, bh) + _dot(ah, bl) + _dot(al, bh)


def _dot3_nt(a, b):
    ah, al = _split_bf16(a)
    bh, bl = _split_bf16(b)
    return _dot_nt(ah, bh) + _dot_nt(ah, bl) + _dot_nt(al, bh)


def _silu(x):
    return x * jax.nn.sigmoid(x)


def _in_proj_kernel(x_ref, g_ref, w_ref, hn_ref, bd_ref, o_ref, h_scr):
    j = pl.program_id(1)

    @pl.when(j == 0)
    def _():
        h_scr[...] = _rmsnorm(x_ref[...], g_ref[...]).astype(BF16)

    y = _dot(h_scr[...], w_ref[...])
    is_qk = jnp.logical_or(j == 1, j == 2)

    @pl.when(is_qk)
    def _():
        hi, lo = _split_bf16(y * y)
        ms = _dot(hi, bd_ref[...]) + _dot(lo, bd_ref[...])
        o_ref[...] = y * lax.rsqrt(ms + EPS) * hn_ref[0]

    @pl.when(jnp.logical_not(is_qk))
    def _():
        o_ref[...] = y


def _in_proj(x, g, w_bf, head_gain, tm):
    m, d = x.shape
    n = w_bf.shape[1]
    wdt = n // 4
    bd = np.kron(np.eye(wdt // HEAD_DIM), np.full((HEAD_DIM, HEAD_DIM), 1.0 / HEAD_DIM))
    bd = jnp.asarray(bd, BF16)
    return pl.pallas_call(
        _in_proj_kernel, name="in_proj",
        grid=(m // tm, 4),
        in_specs=[
            pl.BlockSpec((tm, d), lambda i, j: (i, 0)),
            pl.BlockSpec((1, d), lambda i, j: (0, 0)),
            pl.BlockSpec((d, wdt), lambda i, j: (0, j)),
            pl.BlockSpec((1, 1, wdt), lambda i, j: (j, 0, 0)),
            pl.BlockSpec((wdt, wdt), lambda i, j: (0, 0)),
        ],
        out_specs=pl.BlockSpec((tm, wdt), lambda i, j: (i, j)),
        out_shape=jax.ShapeDtypeStruct((m, n), F32),
        scratch_shapes=[pltpu.VMEM((tm, d), BF16)],
        compiler_params=_cparams("parallel", "arbitrary"),
    )(x, g, w_bf, head_gain, bd)


def _mix_ffn_kernel(x_ref, a_ref, b_ref, wo_ref, g_ref, wg_ref, wu_ref, wd_ref, o_ref,
                    x1_scr, h_scr, acc_scr):
    f = pl.program_id(1)
    half = a_ref.shape[1]

    @pl.when(f == 0)
    def _():
        x1 = (x_ref[...] + _dot(a_ref[...], wo_ref[:half, :]) + _dot(b_ref[...], wo_ref[half:, :]))
        x1_scr[...] = x1
        h_scr[...] = _rmsnorm(x1, g_ref[...]).astype(BF16)
        acc_scr[...] = jnp.zeros_like(acc_scr)

    h = h_scr[...]
    act = (_silu(_dot(h, wg_ref[...])) * _dot(h, wu_ref[...])).astype(BF16)
    acc_scr[...] += _dot(act, wd_ref[...])

    @pl.when(f == pl.num_programs(1) - 1)
    def _():
        o_ref[...] = x1_scr[...] + acc_scr[...]


def _mix_ffn(x, a, b, wo_bf, g, wg_bf, wu_bf, wd_bf, tm, tf):
    m, d = x.shape
    half = a.shape[1]
    ff = wg_bf.shape[1]
    return pl.pallas_call(
        _mix_ffn_kernel, name="mix_ffn",
        grid=(m // tm, ff // tf),
        in_specs=[
            pl.BlockSpec((tm, d), lambda i, f: (i, 0)),
            pl.BlockSpec((tm, half), lambda i, f: (i, 0)),
            pl.BlockSpec((tm, half), lambda i, f: (i, 0)),
            pl.BlockSpec((2 * half, d), lambda i, f: (0, 0)),
            pl.BlockSpec((1, d), lambda i, f: (0, 0)),
            pl.BlockSpec((d, tf), lambda i, f: (0, f)),
            pl.BlockSpec((d, tf), lambda i, f: (0, f)),
            pl.BlockSpec((tf, d), lambda i, f: (f, 0)),
        ],
        out_specs=pl.BlockSpec((tm, d), lambda i, f: (i, 0)),
        out_shape=jax.ShapeDtypeStruct((m, d), F32),
        scratch_shapes=[pltpu.VMEM((tm, d), F32), pltpu.VMEM((tm, d), BF16), pltpu.VMEM((tm, d), F32)],
        compiler_params=_cparams("parallel", "arbitrary"),
    )(x, a, b, wo_bf, g, wg_bf, wu_bf, wd_bf)


def _route(h, wr, br, n_exp):
    logits = _dot(h.astype(BF16), wr) + br
    lane = lax.broadcasted_iota(jnp.int32, logits.shape, 1)
    logits = jnp.where(lane < n_exp, logits, NEG)
    m1 = jnp.max(logits, axis=-1, keepdims=True)
    i1 = jnp.min(jnp.where(logits == m1, lane, LANES), axis=-1, keepdims=True)
    rest = jnp.where(lane == i1, NEG, logits)
    m2 = jnp.max(rest, axis=-1, keepdims=True)
    i2 = jnp.min(jnp.where(rest == m2, lane, LANES), axis=-1, keepdims=True)
    e2 = jnp.exp(m2 - m1)
    g1 = 1.0 / (1.0 + e2)
    g2 = e2 / (1.0 + e2)
    gate = jnp.where(lane == i1, g1, 0.0) + jnp.where(lane == i2, g2, 0.0)
    return gate, jnp.logical_or(lane == i1, lane == i2)


def _moe_dense_kernel(x_ref, g_ref, wr_ref, br_ref, wg_ref, wu_ref, wd_ref, o_ref,
                      h_scr, gate_scr, acc_scr, *, n_exp):
    e = pl.program_id(1)
    f = pl.program_id(2)

    @pl.when(jnp.logical_and(e == 0, f == 0))
    def _():
        h = _rmsnorm(x_ref[...], g_ref[...])
        h_scr[...] = h.astype(BF16)
        gate_scr[...] = _route(h, wr_ref[...], br_ref[...], n_exp)[0]
        acc_scr[...] = jnp.zeros_like(acc_scr)

    h = h_scr[...]
    lane = lax.broadcasted_iota(jnp.int32, gate_scr.shape, 1)
    w_e = jnp.sum(jnp.where(lane == e, gate_scr[...], 0.0), axis=-1, keepdims=True)
    act = (_silu(_dot(h, wg_ref[0])) * _dot(h, wu_ref[0])).astype(BF16)
    acc_scr[...] += w_e * _dot(act, wd_ref[0])

    @pl.when(jnp.logical_and(e == n_exp - 1, f == pl.num_programs(2) - 1))
    def _():
        o_ref[...] = x_ref[...] + acc_scr[...]


def _moe_dense(x, g, wr_pad, br_pad, wg_bf, wu_bf, wd_bf, tm, tf):
    m, d = x.shape
    n_exp, _, ff = wg_bf.shape
    return pl.pallas_call(
        functools.partial(_moe_dense_kernel, n_exp=n_exp), name="moe_dense",
        grid=(m // tm, n_exp, ff // tf),
        in_specs=[
            pl.BlockSpec((tm, d), lambda i, e, f: (i, 0)),
            pl.BlockSpec((1, d), lambda i, e, f: (0, 0)),
            pl.BlockSpec((d, LANES), lambda i, e, f: (0, 0)),
            pl.BlockSpec((1, LANES), lambda i, e, f: (0, 0)),
            pl.BlockSpec((1, d, tf), lambda i, e, f: (e, 0, f)),
            pl.BlockSpec((1, d, tf), lambda i, e, f: (e, 0, f)),
            pl.BlockSpec((1, tf, d), lambda i, e, f: (e, f, 0)),
        ],
        out_specs=pl.BlockSpec((tm, d), lambda i, e, f: (i, 0)),
        out_shape=jax.ShapeDtypeStruct((m, d), F32),
        scratch_shapes=[pltpu.VMEM((tm, d), BF16), pltpu.VMEM((tm, LANES), F32), pltpu.VMEM((tm, d), F32)],
        compiler_params=_cparams("parallel", "arbitrary", "arbitrary"),
    )(x, g, wr_pad, br_pad, wg_bf, wu_bf, wd_bf)


def _moe_route_kernel(x_ref, g_ref, wr_ref, br_ref, h_ref, gate_ref, pos_ref, post_ref, cb_ref, tot_ref, carry,
                      *, n_exp):
    i = pl.program_id(0)
    tm = x_ref.shape[0]

    @pl.when(i == 0)
    def _():
        carry[...] = jnp.zeros_like(carry)

    h = _rmsnorm(x_ref[...], g_ref[...])
    h_ref[...] = h.astype(BF16)
    gate, chosen = _route(h, wr_ref[...], br_ref[...], n_exp)
    gate_ref[...] = gate
    onehot = jnp.where(chosen, 1.0, 0.0)
    r = lax.broadcasted_iota(jnp.int32, (tm, tm), 0)
    c = lax.broadcasted_iota(jnp.int32, (tm, tm), 1)
    before = _dot(jnp.where(c < r, 1.0, 0.0).astype(BF16), onehot.astype(BF16))
    start = carry[...]
    pos = jnp.where(chosen, before + start, -1.0)
    pos_ref[...] = pos
    post_ref[...] = pos.T[:post_ref.shape[0], :]
    cb_ref[0] = start
    total = start + jnp.sum(onehot, axis=0, keepdims=True)
    carry[...] = total
    tot_ref[...] = total


def _moe_route(x, g, wr_bf, br, n_exp, tm):
    m, d = x.shape
    nt = m // tm
    ne8 = -(-n_exp // SUBLANES) * SUBLANES
    f = jax.ShapeDtypeStruct
    return pl.pallas_call(
        functools.partial(_moe_route_kernel, n_exp=n_exp), name="moe_route",
        grid=(nt,),
        in_specs=[
            pl.BlockSpec((tm, d), lambda i: (i, 0)),
            pl.BlockSpec((1, d), lambda i: (0, 0)),
            pl.BlockSpec((d, LANES), lambda i: (0, 0)),
            pl.BlockSpec((1, LANES), lambda i: (0, 0)),
        ],
        out_specs=[
            pl.BlockSpec((tm, d), lambda i: (i, 0)),
            pl.BlockSpec((tm, LANES), lambda i: (i, 0)),
            pl.BlockSpec((tm, LANES), lambda i: (i, 0)),
            pl.BlockSpec((ne8, tm), lambda i: (0, i)),
            pl.BlockSpec((1, 1, LANES), lambda i: (i, 0, 0)),
            pl.BlockSpec((1, LANES), lambda i: (0, 0)),
        ],
        out_shape=[f((m, d), BF16), f((m, LANES), F32), f((m, LANES), F32), f((ne8, m), F32),
                   f((nt, 1, LANES), F32), f((1, LANES), F32)],
        scratch_shapes=[pltpu.VMEM((1, LANES), F32)],
        compiler_params=_cparams("arbitrary"),
    )(x, g, wr_bf, br)


def _moe_gather_kernel(pc_ref, ps_ref, pe_ref, pfirst_ref, pvalid_ref, off_ref, h_ref, post_ref, o_ref, acc):
    k = pl.program_id(0)
    rc, tm = o_ref.shape[0], h_ref.shape[0]

    @pl.when(pfirst_ref[k] == 1)
    def _():
        acc[...] = jnp.zeros_like(acc)

    @pl.when(pvalid_ref[k] == 1)
    def _():
        e = pe_ref[k]
        dest = post_ref[pl.ds(e, 1), :] + off_ref[e].astype(F32)
        rows = (pc_ref[k] * rc + lax.broadcasted_iota(jnp.int32, (rc, tm), 0)).astype(F32)
        acc[...] += _dot(jnp.where(dest == rows, 1.0, 0.0).astype(BF16), h_ref[...])

    o_ref[...] = acc[...].astype(BF16)


def _moe_ffn_kernel(ce_ref, cv_ref, x_ref, wg_ref, wu_ref, wd_ref, y_ref, *, tf):
    c = pl.program_id(0)

    @pl.when(cv_ref[c] == 1)
    def _():
        x = x_ref[...]
        acc = jnp.zeros(y_ref.shape, F32)
        for f0 in range(0, wg_ref.shape[2], tf):
            act = (_silu(_dot(x, wg_ref[0, :, f0:f0 + tf])) * _dot(x, wu_ref[0, :, f0:f0 + tf])).astype(BF16)
            acc = acc + _dot(act, wd_ref[0, f0:f0 + tf, :])
        y_ref[...] = acc

    @pl.when(cv_ref[c] == 0)
    def _():
        y_ref[...] = jnp.zeros_like(y_ref)


def _moe_combine_kernel(qs_ref, qc_ref, qe_ref, qfirst_ref, qvalid_ref, off_ref, x_ref, y_ref, pos_ref, gate_ref,
                        o_ref):
    k = pl.program_id(0)
    tm, rc = x_ref.shape[0], y_ref.shape[0]

    @pl.when(qfirst_ref[k] == 1)
    def _():
        o_ref[...] = x_ref[...]

    @pl.when(qvalid_ref[k] == 1)
    def _():
        e = qe_ref[k]
        lane = lax.broadcasted_iota(jnp.int32, pos_ref.shape, 1)
        pos_e = jnp.sum(jnp.where(lane == e, pos_ref[...], 0.0), axis=-1, keepdims=True)
        gate_e = jnp.sum(jnp.where(lane == e, gate_ref[...], 0.0), axis=-1, keepdims=True)
        dest = pos_e + off_ref[e].astype(F32)
        cols = (qc_ref[k] * rc + lax.broadcasted_iota(jnp.int32, (tm, rc), 1)).astype(F32)
        pick = jnp.where(dest == cols, 1.0, 0.0).astype(BF16)
        hi, lo = _split_bf16(y_ref[...])
        o_ref[...] += gate_e * (_dot(pick, hi) + _dot(pick, lo))


def _pair_list(inter, n_pairs_max):
    nb = inter.shape[1]
    flat = inter.reshape(-1)
    n = jnp.sum(flat.astype(jnp.int32))
    idx = jnp.nonzero(flat, size=n_pairs_max, fill_value=0)[0].astype(jnp.int32)
    k = jnp.arange(n_pairs_max, dtype=jnp.int32)
    idx = jnp.where(k < n, idx, idx[jnp.maximum(n - 1, 0)])
    a, b = idx // nb, idx % nb
    valid = k < n
    first = jnp.logical_and(valid, jnp.logical_or(k == 0, a != jnp.roll(a, 1)))
    return a, b, valid.astype(jnp.int32), first.astype(jnp.int32)


ROW_CHUNK = 256


def _moe_sparse(x, g, wr_bf, br, wg_bf, wu_bf, wd_bf, tm, tf):
    m, d = x.shape
    n_exp, _, ff = wg_bf.shape
    nt = m // tm
    rc = ROW_CHUNK
    h, gate, pos, pos_t, cb, tot = _moe_route(x, g, wr_bf, br, n_exp, tm)

    cnt = tot[0, :n_exp].astype(jnp.int32)
    cbi = jnp.concatenate([cb[:, 0, :n_exp], tot[:, :n_exp]], axis=0).astype(jnp.int32)
    gsz = (cnt + rc - 1) // rc * rc
    ends = jnp.cumsum(gsz)
    off = (ends - gsz).astype(jnp.int32)
    n_chunks = TOP_K_EXPERTS * m // rc + n_exp
    cstart = jnp.arange(n_chunks, dtype=jnp.int32) * rc
    ce = jnp.minimum(jnp.searchsorted(ends, cstart, side='right'), n_exp - 1).astype(jnp.int32)
    cv = cstart < ends[-1]
    per_tile = cbi.T[ce]
    start = off[ce][:, None] + per_tile[:, :-1]
    end = off[ce][:, None] + per_tile[:, 1:]
    inter = (cv[:, None] & (end > start) & (start < cstart[:, None] + rc) & (end > cstart[:, None]))
    n_pairs = n_chunks + n_exp * nt
    pc, ps, pvalid, pfirst = _pair_list(inter, n_pairs)
    qs, qc, qvalid, qfirst = _pair_list(inter.T, n_pairs)

    xs = pl.pallas_call(
        _moe_gather_kernel, name="moe_gather",
        grid_spec=pltpu.PrefetchScalarGridSpec(
            num_scalar_prefetch=6,
            grid=(n_pairs,),
            in_specs=[
                pl.BlockSpec((tm, d), lambda k, pc, ps, *_: (ps[k], 0)),
                pl.BlockSpec((pos_t.shape[0], tm), lambda k, pc, ps, *_: (0, ps[k])),
            ],
            out_specs=pl.BlockSpec((rc, d), lambda k, pc, *_: (pc[k], 0)),
            scratch_shapes=[pltpu.VMEM((rc, d), F32)],
        ),
        out_shape=jax.ShapeDtypeStruct((n_chunks * rc, d), BF16),
        compiler_params=_cparams("arbitrary"),
    )(pc, ps, ce[pc], pfirst, pvalid, off, h, pos_t)

    cvi = cv.astype(jnp.int32)
    wspec = lambda shape: pl.BlockSpec(shape, lambda c, ce, cv: (ce[c], 0, 0), pipeline_mode=pl.Buffered(1))
    ys = pl.pallas_call(
        functools.partial(_moe_ffn_kernel, tf=tf), name="moe_ffn",
        grid_spec=pltpu.PrefetchScalarGridSpec(
            num_scalar_prefetch=2,
            grid=(n_chunks,),
            in_specs=[
                pl.BlockSpec((rc, d), lambda c, ce, cv: (c * cv[c], 0)),
                wspec((1, d, ff)), wspec((1, d, ff)), wspec((1, ff, d)),
            ],
            out_specs=pl.BlockSpec((rc, d), lambda c, ce, cv: (c, 0)),
        ),
        out_shape=jax.ShapeDtypeStruct((n_chunks * rc, d), F32),
        compiler_params=_cparams("arbitrary"),
    )(ce, cvi, xs, wg_bf, wu_bf, wd_bf)

    return pl.pallas_call(
        _moe_combine_kernel, name="moe_combine",
        grid_spec=pltpu.PrefetchScalarGridSpec(
            num_scalar_prefetch=6,
            grid=(n_pairs,),
            in_specs=[
                pl.BlockSpec((tm, d), lambda k, qs, qc, *_: (qs[k], 0)),
                pl.BlockSpec((rc, d), lambda k, qs, qc, *_: (qc[k], 0)),
                pl.BlockSpec((tm, LANES), lambda k, qs, *_: (qs[k], 0)),
                pl.BlockSpec((tm, LANES), lambda k, qs, *_: (qs[k], 0)),
            ],
            out_specs=pl.BlockSpec((tm, d), lambda k, qs, *_: (qs[k], 0)),
        ),
        out_shape=jax.ShapeDtypeStruct((m, d), F32),
        compiler_params=_cparams("arbitrary"),
    )(qs, qc, ce[qc], qfirst, qvalid, off, x, ys, pos, gate)


HALO = 16


def _pool_kernel(x_ref, g_ref, wp_ref, sc_ref, o_ref, hist_ref, hbuf):
    t = pl.program_id(1)
    tm = x_ref.shape[1]
    gw = wp_ref.shape[1]

    @pl.when(t == 0)
    def _():
        hbuf[0:HALO, :] = jnp.zeros((HALO, hbuf.shape[1]), F32)

    @pl.when(t > 0)
    def _():
        hbuf[0:HALO, :] = hbuf[tm:tm + HALO, :]

    x = x_ref[0]
    hn = _rmsnorm(x, g_ref[...])
    hbuf[HALO:HALO + tm, :] = hn
    pos = t * tm + lax.broadcasted_iota(jnp.int32, (tm, 1), 0)
    ys = []
    for gi, w in enumerate(POOL_WINDOWS):
        sl = slice(gi * gw, (gi + 1) * gw)
        tot = hn[:, sl]
        for i in range(1, w):
            tot = tot + hbuf[HALO - i:HALO - i + tm, sl]
        cnt = jnp.minimum(pos + 1, w).astype(F32)
        pooled = tot / cnt - hn[:, sl]
        ys.append(_dot(pooled.astype(BF16), wp_ref[gi]))
    o_ref[0] = x + jnp.concatenate(ys, axis=1) * sc_ref[...]

    @pl.when(t == pl.num_programs(1) - 1)
    def _():
        hist_ref[0] = hbuf[tm:tm + HALO, :]


def _pool_prompt(x, g, wp_bf, sc, tm):
    b, s, d = x.shape
    ng, gw, _ = wp_bf.shape
    return pl.pallas_call(
        _pool_kernel, name="pool_prompt",
        grid=(b, s // tm),
        in_specs=[
            pl.BlockSpec((1, tm, d), lambda i, t: (i, t, 0)),
            pl.BlockSpec((1, d), lambda i, t: (0, 0)),
            pl.BlockSpec((ng, gw, gw), lambda i, t: (0, 0, 0)),
            pl.BlockSpec((1, d), lambda i, t: (0, 0)),
        ],
        out_specs=[
            pl.BlockSpec((1, tm, d), lambda i, t: (i, t, 0)),
            pl.BlockSpec((1, HALO, d), lambda i, t: (i, 0, 0)),
        ],
        out_shape=[jax.ShapeDtypeStruct((b, s, d), F32), jax.ShapeDtypeStruct((b, HALO, d), F32)],
        scratch_shapes=[pltpu.VMEM((HALO + tm, d), F32)],
        compiler_params=_cparams("parallel", "arbitrary"),
    )(x, g, wp_bf, sc)


def _pool_step_kernel(x_ref, g_ref, hist_ref, wp_ref, sc_ref, o_ref, hn_ref, *, n_valid):
    gw = wp_ref.shape[1]
    x = x_ref[...]
    hn = _rmsnorm(x, g_ref[...])
    hn_ref[...] = hn
    ys = []
    for gi, w in enumerate(POOL_WINDOWS):
        sl = slice(gi * gw, (gi + 1) * gw)
        tot = hn[:, sl]
        for i in range(1, min(w, n_valid + 1)):
            tot = tot + hist_ref[POOL_HIST - i][:, sl]
        pooled = tot / float(min(w, n_valid + 1)) - hn[:, sl]
        ys.append(_dot(pooled.astype(BF16), wp_ref[gi]))
    o_ref[...] = x + jnp.concatenate(ys, axis=1) * sc_ref[...]


def _pool_step(x, g, hist_t, wp_bf, sc, n_valid):
    b, d = x.shape
    return pl.pallas_call(
        functools.partial(_pool_step_kernel, n_valid=n_valid), name="pool_step",
        out_shape=[jax.ShapeDtypeStruct((b, d), F32), jax.ShapeDtypeStruct((b, d), F32)],
        compiler_params=pltpu.CompilerParams(vmem_limit_bytes=VMEM_LIMIT),
    )(x, g, hist_t, wp_bf, sc)


N_POW = SUBLANES


def _s5_prep_kernel(ar_ref, ai_ref, ldt_ref, br_ref, bi_ref, pwr_ref, pwi_ref, bbr_ref, bbi_ref):
    ar = ar_ref[...]
    ai = ai_ref[...]
    dt = jnp.exp(ldt_ref[...])
    mag = jnp.exp(ar * dt)
    lr = mag * jnp.cos(ai * dt)
    li = mag * jnp.sin(ai * dt)
    den = ar * ar + ai * ai
    rr = ((lr - 1.0) * ar + li * ai) / den
    ri = (li * ar - (lr - 1.0) * ai) / den
    br = br_ref[...]
    bi = bi_ref[...]
    bbr_ref[...] = rr * br - ri * bi
    bbi_ref[...] = rr * bi + ri * br
    cr, ci = lr, li
    for k in range(N_POW):
        pwr_ref[k] = cr
        pwi_ref[k] = ci
        cr, ci = cr * lr - ci * li, cr * li + ci * lr


def _s5_prep(a_re, a_im, log_dt, b_re, b_im):
    g, n = a_re.shape
    p = b_re.shape[2]
    f = jax.ShapeDtypeStruct
    return pl.pallas_call(
        _s5_prep_kernel, name="s5_prep",
        out_shape=[f((N_POW, g, 1, n), F32), f((N_POW, g, 1, n), F32), f((g, p, n), F32), f((g, p, n), F32)],
    )(a_re[:, None, :], a_im[:, None, :], log_dt[:, None, None],
      jnp.transpose(b_re, (0, 2, 1)), jnp.transpose(b_im, (0, 2, 1)))


def _s5_weights(a_re, a_im, log_dt, b_re, b_im, c_re, c_im):
    g, n = a_re.shape
    p = b_re.shape[2]
    gn = g * n
    pwr, pwi, bbr, bbi = _s5_prep(a_re, a_im, log_dt, b_re, b_im)
    pwr = pwr.reshape(N_POW, gn)
    pwi = pwi.reshape(N_POW, gn)
    row = np.arange(SUBLANES)[:, None]
    tabs = []
    for k in (1, 2, 4):
        tabs += [jnp.where(row >= k, pwr[k - 1][None], 0.0), jnp.where(row >= k, pwi[k - 1][None], 0.0)]
    tabs += [pwr, pwi]
    tabs = jnp.stack(tabs)
    lam = jnp.stack([pwr[0], pwi[0]])
    eye = jnp.eye(g, dtype=F32)
    b_dense = [jnp.einsum('gpn,gh->gphn', bb, eye).reshape(g * p, gn) for bb in (bbr, bbi)]
    n_tiles = gn // LANES
    u_per_tile = LANES // n * p
    wb = []
    for j in range(n_tiles):
        c0 = (j * u_per_tile) // LANES * LANES
        wb.append(jnp.concatenate([bd[c0:c0 + LANES, j * LANES:(j + 1) * LANES] for bd in b_dense], axis=1))
    wb = jnp.stack(wb).astype(BF16)
    c_dense = [jnp.einsum('gpn,gh->gnhp', cc, eye).reshape(gn, g * p) for cc in (c_re, -c_im)]
    k_per_tile = LANES // p * n
    wc = []
    for m in range(g * p // LANES):
        wc.append(jnp.concatenate([cd[m * k_per_tile:(m + 1) * k_per_tile, m * LANES:(m + 1) * LANES]
                                   for cd in c_dense], axis=0))
    wc = jnp.stack(wc).astype(BF16)
    return wb, tabs, lam, wc


def _s5_drive(u, wb_ref, bur, bui, row0):
    t = u.shape[0]
    ub = u.astype(BF16)
    n_tiles = wb_ref.shape[0]
    u_per_tile = u.shape[1] // n_tiles
    for j in range(n_tiles):
        c0 = (j * u_per_tile) // LANES * LANES
        r = _dot(ub[:, c0:c0 + LANES], wb_ref[j])
        bur[row0:row0 + t, j * LANES:(j + 1) * LANES] = r[:, :LANES]
        bui[row0:row0 + t, j * LANES:(j + 1) * LANES] = r[:, LANES:]


def _s5_readout(u, hr, hi, wc_ref, d_ref, wgl_ref, bgl_ref):
    hrb = hr.astype(BF16)
    hib = hi.astype(BF16)
    kk = wc_ref.shape[1] // 2
    ys = []
    for m in range(wc_ref.shape[0]):
        ys.append(_dot(hrb[:, m * kk:(m + 1) * kk], wc_ref[m, :kk, :])
                  + _dot(hib[:, m * kk:(m + 1) * kk], wc_ref[m, kk:, :]))
    y = jnp.concatenate(ys, axis=1) + d_ref[...] * u
    z = jax.nn.gelu(y, approximate=True)
    return z * jax.nn.sigmoid(_dot(z.astype(BF16), wgl_ref[...]) + bgl_ref[...])


def _s5_scan_kernel(u_ref, h0_ref, wb_ref, tab_ref, wc_ref, d_ref, wgl_ref, bgl_ref, o_ref, hf_ref, bur, bui):
    t = pl.program_id(1)
    tt = u_ref.shape[1]
    gn = bur.shape[1]
    c0 = SUBLANES

    @pl.when(t == 0)
    def _():
        bur[0:c0, :] = jnp.broadcast_to(h0_ref[0, :, :gn], (c0, gn))
        bui[0:c0, :] = jnp.broadcast_to(h0_ref[0, :, gn:], (c0, gn))

    @pl.when(t > 0)
    def _():
        bur[0:c0, :] = bur[tt:tt + c0, :]
        bui[0:c0, :] = bui[tt:tt + c0, :]

    u = u_ref[0]
    _s5_drive(u, wb_ref, bur, bui, c0)

    def block(b, carry):
        base = pl.multiple_of(b * SUBLANES, SUBLANES)
        cr = bur[pl.ds(base + c0 - 1, 1), :]
        ci = bui[pl.ds(base + c0 - 1, 1), :]
        xr = bur[pl.ds(base + c0, SUBLANES), :]
        xi = bui[pl.ds(base + c0, SUBLANES), :]
        for s, k in enumerate((1, 2, 4)):
            ar = tab_ref[2 * s]
            ai = tab_ref[2 * s + 1]
            sr = pltpu.roll(xr, k, axis=0)
            si = pltpu.roll(xi, k, axis=0)
            xr, xi = xr + ar * sr - ai * si, xi + ar * si + ai * sr
        pr = tab_ref[6]
        pi = tab_ref[7]
        bur[pl.ds(base + c0, SUBLANES), :] = xr + pr * cr - pi * ci
        bui[pl.ds(base + c0, SUBLANES), :] = xi + pr * ci + pi * cr
        return carry

    lax.fori_loop(0, tt // SUBLANES, block, 0)
    o_ref[0] = _s5_readout(u, bur[c0:c0 + tt, :], bui[c0:c0 + tt, :], wc_ref, d_ref, wgl_ref, bgl_ref).astype(BF16)

    @pl.when(t == pl.num_programs(1) - 1)
    def _():
        hf_ref[0] = jnp.concatenate([bur[tt + c0 - 1:tt + c0, :], bui[tt + c0 - 1:tt + c0, :]], axis=1)


def _s5_scan(proj, h0, wb, tabs, wc, dsk, wgl_bf, bgl, tt):
    b, s, _ = proj.shape
    w = dsk.shape[1]
    gn = tabs.shape[2]
    full = lambda a: pl.BlockSpec(a.shape, lambda i, t: (0,) * a.ndim)
    return pl.pallas_call(
        _s5_scan_kernel, name="s5_scan",
        grid=(b, s // tt),
        in_specs=[
            pl.BlockSpec((1, tt, w), lambda i, t: (i, t, 0)),
            pl.BlockSpec((1, 1, 2 * gn), lambda i, t: (i, 0, 0)),
            full(wb), full(tabs), full(wc), full(dsk), full(wgl_bf), full(bgl),
        ],
        out_specs=[
            pl.BlockSpec((1, tt, w), lambda i, t: (i, t, 0)),
            pl.BlockSpec((1, 1, 2 * gn), lambda i, t: (i, 0, 0)),
        ],
        out_shape=[jax.ShapeDtypeStruct((b, s, w), BF16), jax.ShapeDtypeStruct((b, 1, 2 * gn), F32)],
        scratch_shapes=[pltpu.VMEM((tt + SUBLANES, gn), F32), pltpu.VMEM((tt + SUBLANES, gn), F32)],
        compiler_params=_cparams("parallel", "arbitrary"),
    )(proj, h0, wb, tabs, wc, dsk, wgl_bf, bgl)


def _s5_step_kernel(u_ref, h0r_ref, h0i_ref, wb_ref, lam_ref, wc_ref, d_ref, wgl_ref, bgl_ref,
                    o_ref, hr_ref, hi_ref, bur, bui):
    u = u_ref[...]
    _s5_drive(u, wb_ref, bur, bui, 0)
    lr = lam_ref[0:1, :]
    li = lam_ref[1:2, :]
    h0r = h0r_ref[...]
    h0i = h0i_ref[...]
    hr = bur[...] + lr * h0r - li * h0i
    hi = bui[...] + lr * h0i + li * h0r
    hr_ref[...] = hr
    hi_ref[...] = hi
    o_ref[...] = _s5_readout(u, hr, hi, wc_ref, d_ref, wgl_ref, bgl_ref).astype(BF16)


def _s5_step(proj, h0r, h0i, wb, lam, wc, dsk, wgl_bf, bgl):
    b = proj.shape[0]
    w = dsk.shape[1]
    gn = lam.shape[1]
    full = lambda a: pl.BlockSpec(a.shape, lambda i: (0,) * a.ndim)
    return pl.pallas_call(
        _s5_step_kernel, name="s5_step",
        grid=(1,),
        in_specs=[pl.BlockSpec((b, w), lambda i: (0, 0)), full(h0r), full(h0i), full(wb), full(lam), full(wc),
                  full(dsk), full(wgl_bf), full(bgl)],
        out_specs=[pl.BlockSpec((b, w), lambda i: (0, 0)), pl.BlockSpec((b, gn), lambda i: (0, 0)),
                   pl.BlockSpec((b, gn), lambda i: (0, 0))],
        out_shape=[jax.ShapeDtypeStruct((b, w), BF16), jax.ShapeDtypeStruct((b, gn), F32),
                   jax.ShapeDtypeStruct((b, gn), F32)],
        scratch_shapes=[pltpu.VMEM((b, gn), F32), pltpu.VMEM((b, gn), F32)],
        compiler_params=_cparams("arbitrary"),
    )(proj, h0r, h0i, wb, lam, wc, dsk, wgl_bf, bgl)


def _t5_bucket(rel):
    n = jnp.maximum(rel, 0)
    max_exact = REL_BUCKETS // 2
    nf = jnp.maximum(n, 1).astype(F32)
    large = max_exact + (jnp.log(nf / max_exact) / math.log(REL_MAX_DIST / max_exact)
                         * (REL_BUCKETS - max_exact)).astype(jnp.int32)
    large = jnp.minimum(large, REL_BUCKETS - 1)
    return jnp.where(n < max_exact, n, large)


N_DIST = -(-(REL_MAX_DIST + MOBA_BLOCK - 1) // MOBA_BLOCK) + 1


def _prompt_bias_tiles(rel_bias):
    blk = MOBA_BLOCK
    span = 2 * blk - 1
    rel = jnp.asarray(np.arange(-(blk - 1), N_DIST * blk), jnp.int32)
    line = jnp.where(rel >= 0, rel_bias.astype(F32).T[:, _t5_bucket(rel)], NEG)
    diag = jnp.stack([line[:, d * blk:d * blk + span] for d in range(N_DIST)], axis=1)
    rep = jnp.tile(diag, (1, 1, blk + 1))[..., :blk * (span + 1)]
    tiles = rep.reshape(diag.shape[0], N_DIST, blk, span + 1)[..., :blk]
    return tiles[..., ::-1]


def _rank_select(g, n_valid, idx, n_cand, axis):
    cnt = jnp.zeros(g.shape, jnp.int32)
    for jp in range(n_cand):
        gp = lax.slice_in_dim(g, jp, jp + 1, axis=axis)
        beats = jnp.logical_or(gp > g, jnp.logical_and(gp == g, jp < idx))
        cnt = cnt + jnp.where(jnp.logical_and(beats, jp < n_valid), 1, 0)
    return jnp.logical_and(idx < n_valid, cnt < MOBA_TOPK)


def _moba_kernel(q_ref, k_ref, v_ref, bias_ref, o_ref, kft, vb, km, qa, sbuf, red, stat, acc_scr):
    i = pl.program_id(2)
    blk = MOBA_BLOCK
    nb = k_ref.shape[1] // blk
    n_hd = LANES // HEAD_DIM
    scale = HEAD_DIM ** -0.5

    @pl.when(i == 0)
    def _():
        feat = lax.broadcasted_iota(jnp.int32, (LANES, blk), 0)
        for j in range(nb):
            kj = k_ref[0, j * blk:(j + 1) * blk, :]
            kft[j, :LANES, :] = kj.T.astype(BF16)
            kft[j, LANES:, :] = jnp.where(feat == j, 1.0, 0.0).astype(BF16)
            km[j:j + 1, :] = jnp.mean(kj, axis=0, keepdims=True)
        vb[...] = v_ref[0].astype(BF16)

    lane = lax.broadcasted_iota(jnp.int32, (blk, LANES), 1)
    wide = lambda a: jnp.concatenate([a, a], axis=1)
    rep = lambda a: jnp.broadcast_to(a, (blk, LANES))
    for hd in range(n_hd):
        qm = (jnp.where(lane // HEAD_DIM == hd, q_ref[0], 0.0) * scale).astype(BF16)
        g = _dot_nt(km[...].astype(BF16), qm)
        jrow = lax.broadcasted_iota(jnp.int32, g.shape, 0)
        keep = jnp.logical_or(_rank_select(g, i, jrow, nb, axis=0), jrow == i)
        pen = jnp.concatenate([jnp.where(keep, 0.0, NEG), jnp.zeros((LANES - nb, blk), F32)], axis=0)
        qa[hd, :, :LANES] = qm
        qa[hd, :, LANES:] = pen.T.astype(BF16)
        red[hd] = jnp.full((blk, LANES), NEG, F32)

    def sweep(step):
        def pair(t, carry):
            step(2 * t)
            step(2 * t + 1)
            return carry

        lax.fori_loop(0, (i + 1) // 2, pair, 0)

        @pl.when(i % 2 == 0)
        def _():
            step(i)

    def scores(j):
        dist = jnp.minimum(i - j, N_DIST - 1)
        kt = kft[j]
        for hd in range(n_hd):
            s = _dot(qa[hd], kt) + bias_ref[hd, dist]
            sbuf[hd, j] = s
            red[hd] = jnp.maximum(red[hd], jnp.maximum(s[:, :LANES], s[:, LANES:]))

    sweep(scores)
    for hd in range(n_hd):
        stat[hd] = rep(jnp.max(red[hd], axis=-1, keepdims=True))
        red[hd] = jnp.zeros((blk, LANES), F32)

    def exps(j):
        for hd in range(n_hd):
            e = jnp.exp(sbuf[hd, j] - wide(stat[hd]))
            sbuf[hd, j] = e
            red[hd] += e[:, :LANES] + e[:, LANES:]

    sweep(exps)
    for hd in range(n_hd):
        stat[hd] = rep(1.0 / jnp.sum(red[hd], axis=-1, keepdims=True))
        acc_scr[hd] = jnp.zeros((blk, LANES), F32)

    def weighted(j):
        vj = vb[pl.ds(pl.multiple_of(j * blk, blk), blk), :]
        for hd in range(n_hd):
            p = (sbuf[hd, j] * wide(stat[hd])).astype(BF16)
            acc_scr[hd] += _dot(p, vj)

    sweep(weighted)
    o = acc_scr[0]
    for hd in range(1, n_hd):
        o = jnp.where(lane // HEAD_DIM == hd, acc_scr[hd], o)
    o_ref[0] = o.astype(BF16)


def _moba_prompt(proj, bias_tiles):
    b, s, n = proj.shape
    w = n // 4
    hp = w // LANES
    heads_per = LANES // HEAD_DIM
    blk = MOBA_BLOCK
    assert s % blk == 0 and s // blk <= LANES
    return pl.pallas_call(
        _moba_kernel, name="moba_prompt",
        grid=(b, hp, s // blk),
        in_specs=[
            pl.BlockSpec((1, blk, LANES), lambda bi, h, i: (bi, i, hp + h)),
            pl.BlockSpec((1, s, LANES), lambda bi, h, i: (bi, 0, 2 * hp + h)),
            pl.BlockSpec((1, s, LANES), lambda bi, h, i: (bi, 0, 3 * hp + h)),
            pl.BlockSpec((heads_per, N_DIST, blk, blk), lambda bi, h, i: (h, 0, 0, 0)),
        ],
        out_specs=pl.BlockSpec((1, blk, LANES), lambda bi, h, i: (bi, i, h)),
        out_shape=jax.ShapeDtypeStruct((b, s, w), BF16),
        scratch_shapes=[
            pltpu.VMEM((s // blk, 2 * LANES, blk), BF16), pltpu.VMEM((s, LANES), BF16),
            pltpu.VMEM((s // blk, LANES), F32), pltpu.VMEM((heads_per, blk, 2 * LANES), BF16),
            pltpu.VMEM((heads_per, s // blk, blk, blk), F32), pltpu.VMEM((heads_per, blk, LANES), F32),
            pltpu.VMEM((heads_per, blk, LANES), F32), pltpu.VMEM((heads_per, blk, LANES), F32),
        ],
        compiler_params=_cparams("parallel", "parallel", "arbitrary"),
    )(proj, proj, proj, bias_tiles)


def _rnd(a):
    return a.astype(BF16).astype(F32)


DEC_GROUP = 4
PAGES_PER_BLOCK = 2


def _dec_scores_kernel(pt_ref, q_ref, *refs):
    del pt_ref
    k_refs, (bias_ref, s_ref, ks_ref) = refs[:-3], refs[-3:]
    nh = s_ref.shape[2]
    hd = nh * HEAD_DIM
    page = k_refs[0].shape[-1]
    q = (q_ref[0] * HEAD_DIM ** -0.5).astype(BF16)
    ones = jnp.ones((SUBLANES, page), BF16)
    for g in range(s_ref.shape[1]):
        kk = [r[0, 0].reshape(hd, page) for r in k_refs[PAGES_PER_BLOCK * g:PAGES_PER_BLOCK * (g + 1)]]
        s_ref[0, g] = jnp.concatenate([_dot(q, k.astype(BF16)) for k in kk], axis=1) + bias_ref[g]
        hi, lo = _split_bf16(kk[0] + kk[1])
        ks_ref[0, g] = (_dot_nt(ones, hi) + _dot_nt(ones, lo))[0:1]


def _dec_scores(qx, cache_kt, page_table, bias_dec, layer):
    b, nh, hd = qx.shape
    page = cache_kt.shape[4]
    assert MOBA_BLOCK == PAGES_PER_BLOCK * page
    nblk = page_table.shape[1] // PAGES_PER_BLOCK
    grp = math.gcd(DEC_GROUP, nblk)
    n_pg = PAGES_PER_BLOCK * grp
    pg = lambda t: pl.BlockSpec((1, 1, nh, HEAD_DIM, page),
                                lambda bi, j, pt: (pt[bi, n_pg * j + t], layer, 0, 0, 0))
    out = lambda rows, last: pl.BlockSpec((1, grp, rows, last), lambda bi, j, pt: (bi, j, 0, 0))
    return pl.pallas_call(
        _dec_scores_kernel, name="moba_dec_scores",
        grid_spec=pltpu.PrefetchScalarGridSpec(
            num_scalar_prefetch=1,
            grid=(b, nblk // grp),
            in_specs=[pl.BlockSpec((1, nh, hd), lambda bi, j, pt: (bi, 0, 0))]
            + [pg(t) for t in range(n_pg)]
            + [pl.BlockSpec((grp, nh, MOBA_BLOCK), lambda bi, j, pt: (j, 0, 0))],
            out_specs=[out(nh, MOBA_BLOCK), out(1, hd)],
        ),
        out_shape=[jax.ShapeDtypeStruct((b, nblk, nh, MOBA_BLOCK), F32), jax.ShapeDtypeStruct((b, nblk, 1, hd), F32)],
        compiler_params=_cparams("parallel", "arbitrary"),
    )(page_table, qx, *([cache_kt] * n_pg), bias_dec)


def _dec_select_kernel(q_ref, kn_ref, b0_ref, ks_ref, s_ref, p_ref, pown_ref, sel_ref):
    nblk = sel_ref.shape[1]
    blk = MOBA_BLOCK
    q = _rnd(q_ref[...] * HEAD_DIM ** -0.5)
    lane = lax.broadcasted_iota(jnp.int32, sel_ref.shape, 1)
    gate = jnp.zeros(sel_ref.shape, F32)
    for j in range(nblk):
        kmean = _rnd(ks_ref[:, j * HEAD_DIM:(j + 1) * HEAD_DIM] * (1.0 / blk))
        gate = jnp.where(lane == j, jnp.sum(q * kmean, axis=-1, keepdims=True), gate)
    sel = _rank_select(gate, nblk, lane, nblk, axis=1)
    sel_ref[...] = jnp.where(sel, 1.0, 0.0)
    s_own = jnp.sum(q * _rnd(kn_ref[...]), axis=-1, keepdims=True) + b0_ref[:, 0:1]
    m = s_own
    for j in range(nblk):
        mj = jnp.max(s_ref[:, j * blk:(j + 1) * blk], axis=-1, keepdims=True)
        m = jnp.maximum(m, jnp.where(sel[:, j:j + 1], mj, NEG))
    e_own = jnp.exp(s_own - m)
    den = e_own
    for j in range(nblk):
        e = jnp.where(sel[:, j:j + 1], jnp.exp(s_ref[:, j * blk:(j + 1) * blk] - m), 0.0)
        p_ref[:, j * blk:(j + 1) * blk] = e
        den = den + jnp.sum(e, axis=-1, keepdims=True)
    inv = 1.0 / den
    for j in range(nblk):
        p_ref[:, j * blk:(j + 1) * blk] = _rnd(p_ref[:, j * blk:(j + 1) * blk] * inv)
    pown_ref[...] = jnp.broadcast_to(_rnd(e_own * inv), pown_ref.shape)


def _dec_pv_kernel(pt_ref, src_ref, need_ref, p_ref, *refs):
    del pt_ref, src_ref
    v_refs, (pown_ref, vn_ref, o_ref) = refs[:-3], refs[-3:]
    bi = pl.program_id(0)
    j = pl.program_id(1)
    grp = p_ref.shape[1]
    nh = p_ref.shape[2]
    hd = nh * HEAD_DIM
    page = v_refs[0].shape[-1]

    @pl.when(j == 0)
    def _():
        o_ref[0] = pown_ref[0] * _rnd(vn_ref[0])

    for g in range(grp):
        @pl.when(need_ref[bi, j * grp + g] == 1)
        def _(g=g):
            pb = p_ref[0, g].astype(BF16)
            vv = [r[0, 0].reshape(hd, page).astype(BF16) for r in v_refs[PAGES_PER_BLOCK * g:PAGES_PER_BLOCK * (g + 1)]]
            o_all = _dot_nt(pb[:, :page], vv[0]) + _dot_nt(pb[:, page:], vv[1])
            col = lax.broadcasted_iota(jnp.int32, o_all.shape, 1)
            row = lax.broadcasted_iota(jnp.int32, o_all.shape, 0)
            o_ref[0] += jnp.sum(jnp.where(col // HEAD_DIM == row, o_all, 0.0), axis=0, keepdims=True)


def _moba_decode(proj_s, cache_kt, cache_vt, page_table, rel_bias, layer):
    b, n = proj_s.shape
    w = n // 4
    nh = w // HEAD_DIM
    page = cache_kt.shape[4]
    past_len = page_table.shape[1] * page
    assert past_len % MOBA_BLOCK == 0
    nblk = past_len // MOBA_BLOCK
    q = proj_s[:, w:2 * w].reshape(b, nh, HEAD_DIM)
    qx = jnp.einsum('bhd,hg->bhgd', q, jnp.eye(nh, dtype=F32)).reshape(b, nh, w)
    kpos = np.arange(past_len).reshape(nblk, MOBA_BLOCK)
    bias = rel_bias.astype(F32).T[:, _t5_bucket(jnp.asarray(past_len - kpos, jnp.int32))]
    s_p, ks_p = _dec_scores(qx, cache_kt, page_table, jnp.transpose(bias, (1, 0, 2)), layer)
    rows = b * nh
    blk = MOBA_BLOCK
    to_rows = lambda a, last: jnp.transpose(a.reshape(b, nblk, nh, last), (0, 2, 1, 3)).reshape(rows, nblk * last)
    b0 = jnp.broadcast_to(jnp.tile(rel_bias.astype(F32)[_t5_bucket(jnp.zeros((), jnp.int32))], b)[:, None],
                          (rows, LANES))
    f = jax.ShapeDtypeStruct
    p_rows, p_own, sel = pl.pallas_call(
        _dec_select_kernel, name="moba_dec_select",
        out_shape=[f((rows, nblk * blk), F32), f((rows, LANES), F32), f((rows, nblk), F32)],
        compiler_params=pltpu.CompilerParams(vmem_limit_bytes=VMEM_LIMIT),
    )(q.reshape(rows, HEAD_DIM), proj_s[:, 2 * w:3 * w].reshape(rows, HEAD_DIM), b0,
      to_rows(ks_p, HEAD_DIM), to_rows(s_p, blk))

    grp = math.gcd(DEC_GROUP, nblk)
    need = (jnp.max(sel.reshape(b, nh, nblk), axis=1) > 0.5).reshape(b, nblk // grp, grp)
    blk_id = jnp.arange(nblk, dtype=jnp.int32).reshape(1, nblk // grp, grp)
    last_needed = lax.cummax(jnp.where(need, blk_id, -1), axis=1)
    first_needed = jnp.min(jnp.where(need, blk_id, nblk), axis=1, keepdims=True)
    fallback = jnp.where(first_needed < nblk, first_needed, blk_id[:, :1])
    src = jnp.where(last_needed < 0, fallback, last_needed).astype(jnp.int32).reshape(b, nblk)
    p_blk = jnp.transpose(p_rows.reshape(b, nh, nblk, blk), (0, 2, 1, 3))
    p_own_x = jnp.repeat(p_own[:, 0].reshape(b, nh), HEAD_DIM, axis=1).reshape(b, 1, w)
    page_spec = lambda g, off: pl.BlockSpec(
        (1, 1, nh, HEAD_DIM, page),
        lambda bi, j, pt, sr, nd: (pt[bi, PAGES_PER_BLOCK * sr[bi, j * grp + g] + off], layer, 0, 0, 0))
    seq_spec = pl.BlockSpec((1, 1, w), lambda bi, j, pt, sr, nd: (bi, 0, 0))
    att = pl.pallas_call(
        _dec_pv_kernel, name="moba_dec_pv",
        grid_spec=pltpu.PrefetchScalarGridSpec(
            num_scalar_prefetch=3,
            grid=(b, nblk // grp),
            in_specs=[pl.BlockSpec((1, grp, nh, blk), lambda bi, j, pt, sr, nd: (bi, j, 0, 0))]
            + [page_spec(g, off) for g in range(grp) for off in range(PAGES_PER_BLOCK)]
            + [seq_spec, seq_spec],
            out_specs=seq_spec,
        ),
        out_shape=f((b, 1, w), F32),
        compiler_params=_cparams("parallel", "arbitrary"),
    )(page_table, src, need.reshape(b, nblk).astype(jnp.int32), p_blk,
      *([cache_vt] * (grp * PAGES_PER_BLOCK)), p_own_x, proj_s[:, 3 * w:].reshape(b, 1, w))
    return att.reshape(b, w).astype(BF16)


def _tile(m, pref):
    t = min(m, pref)
    while m % t:
        t //= 2
    return t


def _ff_tile(ff, pref):
    best = LANES
    for t in range(LANES, pref + 1, LANES):
        if ff % t == 0:
            best = t
    return best


def kernel(x_prompt, x_sample, cache_k, cache_v, page_table, state_s5_re, state_s5_im, state_pool, rel_bias,
           norm_mix_e, w_in_e, q_norm_e, k_norm_e, s5_a_re, s5_a_im, s5_log_dt, s5_b_re, s5_b_im, s5_c_re,
           s5_c_im, s5_d, s5_w_glu, s5_b_glu, w_out_e, norm_ffn_e, ffn_w_gate, ffn_w_up, ffn_w_down,
           norm_mix_o, pool_w, pool_scale, norm_ffn_o, router_w, router_b, moe_w_gate, moe_w_up, moe_w_down):
    bp, seq, d = x_prompt.shape
    db = x_sample.shape[0]
    assert x_sample.shape[1] == 1
    depth = norm_mix_e.shape[0] + norm_mix_o.shape[0]
    wdt = w_in_e.shape[2] // 4
    nh = wdt // HEAD_DIM
    past_len = page_table.shape[1] * cache_k.shape[2]
    n_exp = router_w.shape[2]
    gn = s5_a_re.shape[1] * s5_a_re.shape[2]

    mp = bp * seq
    xp = x_prompt.reshape(mp, d)
    xs = x_sample.reshape(db, d)
    tm_p = _tile(mp, 512)
    bias_tiles = _prompt_bias_tiles(rel_bias)
    row = lambda v: v.reshape(1, -1).astype(F32)
    cache_kt = jnp.transpose(cache_k, (0, 1, 3, 4, 2))
    cache_vt = jnp.transpose(cache_v, (0, 1, 3, 4, 2))

    kp_l, vp_l, ks_l, vs_l = [], [], [], []
    s5p_l, s5s_re_l, s5s_im_l = [], [], []
    poolp_l, pools_l = [], []
    for layer in range(depth):
        if layer % 2 == 0:
            e = layer // 2
            w_in = w_in_e[e].astype(BF16)
            ones = jnp.ones((wdt,), F32)
            head_gain = jnp.stack([ones, jnp.tile(q_norm_e[e].astype(F32), nh),
                                   jnp.tile(k_norm_e[e].astype(F32), nh), ones]).reshape(4, 1, wdt)
            wb, tabs, lam, wc = _s5_weights(s5_a_re[e], s5_a_im[e], s5_log_dt[e], s5_b_re[e], s5_b_im[e],
                                            s5_c_re[e], s5_c_im[e])
            dsk = row(s5_d[e])
            wgl = s5_w_glu[e].astype(BF16)
            bgl = row(s5_b_glu[e])
            w_out = w_out_e[e].astype(BF16)
            wg, wu, wd = ffn_w_gate[e].astype(BF16), ffn_w_up[e].astype(BF16), ffn_w_down[e].astype(BF16)
            tf = _ff_tile(wg.shape[1], 1536)

            proj = _in_proj(xp, row(norm_mix_e[e]), w_in, head_gain, tm_p)
            proj3 = proj.reshape(bp, seq, 4 * wdt)
            s5_out, hfin = _s5_scan(proj3, jnp.zeros((bp, 1, 2 * gn), F32), wb, tabs, wc, dsk, wgl, bgl,
                                    _tile(seq, 256))
            att = _moba_prompt(proj3, bias_tiles)
            xp = _mix_ffn(xp, s5_out.reshape(mp, wdt), att.reshape(mp, wdt), w_out, row(norm_ffn_e[e]),
                          wg, wu, wd, tm_p, tf)
            kp_l.append(proj3[:, :, 2 * wdt:3 * wdt].reshape(bp, seq, nh, HEAD_DIM))
            vp_l.append(proj3[:, :, 3 * wdt:].reshape(bp, seq, nh, HEAD_DIM))
            s5p_l.append(hfin.reshape(bp, 2, -1, S5_STATE))

            proj_s = _in_proj(xs, row(norm_mix_e[e]), w_in, head_gain, db)
            s5_out_s, hr_s, hi_s = _s5_step(proj_s, state_s5_re[:, e].reshape(db, gn).astype(F32),
                                            state_s5_im[:, e].reshape(db, gn).astype(F32),
                                            wb, lam, wc, dsk, wgl, bgl)
            att_s = _moba_decode(proj_s, cache_kt, cache_vt, page_table, rel_bias, e)
            xs = _mix_ffn(xs, s5_out_s, att_s, w_out, row(norm_ffn_e[e]), wg, wu, wd, db, tf)
            ks_l.append(proj_s[:, 2 * wdt:3 * wdt].reshape(db, 1, nh, HEAD_DIM))
            vs_l.append(proj_s[:, 3 * wdt:].reshape(db, 1, nh, HEAD_DIM))
            s5s_re_l.append(hr_s.reshape(db, -1, S5_STATE))
            s5s_im_l.append(hi_s.reshape(db, -1, S5_STATE))
        else:
            o = layer // 2
            wp = pool_w[o].astype(BF16)
            sc = row(pool_scale[o])
            wr = jnp.zeros((d, LANES), BF16).at[:, :n_exp].set(router_w[o].astype(BF16))
            br = jnp.zeros((1, LANES), F32).at[:, :n_exp].set(router_b[o].astype(F32))
            wg, wu, wd = moe_w_gate[o].astype(BF16), moe_w_up[o].astype(BF16), moe_w_down[o].astype(BF16)
            tf = _ff_tile(wg.shape[2], 1792)

            xp3, hist_p = _pool_prompt(xp.reshape(bp, seq, d), row(norm_mix_o[o]), wp, sc, _tile(seq, 512))
            xp = _moe_sparse(xp3.reshape(mp, d), row(norm_ffn_o[o]), wr, br, wg, wu, wd, tm_p, tf // 2)
            poolp_l.append(hist_p[:, HALO - POOL_HIST:])

            hist_s = state_pool[:, o].astype(F32)
            xs, hn_s = _pool_step(xs, row(norm_mix_o[o]), jnp.transpose(hist_s, (1, 0, 2)), wp, sc,
                                  min(POOL_HIST, past_len))
            xs = _moe_dense(xs, row(norm_ffn_o[o]), wr, br, wg, wu, wd, db, tf)
            pools_l.append(jnp.concatenate([hist_s[:, 1:], hn_s[:, None]], axis=1))

    s5p = jnp.stack(s5p_l, axis=1)
    return (xp.reshape(bp, seq, d), xs.reshape(db, 1, d),
            jnp.stack(kp_l, axis=2), jnp.stack(vp_l, axis=2),
            jnp.stack(ks_l, axis=2), jnp.stack(vs_l, axis=2),
            s5p[:, :, 0], s5p[:, :, 1],
            jnp.stack(s5s_re_l, axis=1), jnp.stack(s5s_im_l, axis=1),
            jnp.stack(poolp_l, axis=1), jnp.stack(pools_l, axis=1))
```

```python
import functools
import math

import jax
import jax.numpy as jnp
import numpy as np
from jax import lax
from jax.experimental import pallas as pl
from jax.experimental.pallas import tpu as pltpu

F32 = jnp.float32
BF16 = jnp.bfloat16
EPS = 1e-6
NEG = -1e30

S5_GROUP = 16
S5_STATE = 64
HEAD_DIM = 64
MOBA_BLOCK = 256
MOBA_TOPK = 3
REL_BUCKETS = 32
REL_MAX_DIST = 1024
POOL_WINDOWS = (2, 4, 8, 16)
POOL_HIST = max(POOL_WINDOWS) - 1
TOP_K_EXPERTS = 2

LANES = 128
SUBLANES = 8
VMEM_LIMIT = 48 * 1024 * 1024


def _cparams(*sem):
    return pltpu.CompilerParams(dimension_semantics=sem, vmem_limit_bytes=VMEM_LIMIT)


def _rmsnorm(x, g):
    ms = jnp.mean(x * x, axis=-1, keepdims=True)
    return x * lax.rsqrt(ms + EPS) * g


def _split_bf16(x):
    hi = x.astype(BF16)
    lo = (x - hi.astype(F32)).astype(BF16)
    return hi, lo


def _dot(a, b):
    return jnp.dot(a, b, preferred_element_type=F32)


def _dot_nt(a, b):
    return lax.dot_general(a, b, (((1,), (1,)), ((), ())), preferred_element_type=F32)


def _dot3(a, b):
    ah, al = _split_bf16(a)
    bh, bl = _split_bf16(b)
    return _dot(ah, bh) + _dot(ah, bl) + _dot(al, bh)


def _dot3_nt(a, b):
    ah, al = _split_bf16(a)
    bh, bl = _split_bf16(b)
    return _dot_nt(ah, bh) + _dot_nt(ah, bl) + _dot_nt(al, bh)


def _silu(x):
    return x * jax.nn.sigmoid(x)


def _in_proj_kernel(x_ref, g_ref, w_ref, hn_ref, bd_ref, o_ref, h_scr):
    j = pl.program_id(1)

    @pl.when(j == 0)
    def _():
        h_scr[...] = _rmsnorm(x_ref[...], g_ref[...]).astype(BF16)

    y = _dot(h_scr[...], w_ref[...])
    is_qk = jnp.logical_or(j == 1, j == 2)

    @pl.when(is_qk)
    def _():
        hi, lo = _split_bf16(y * y)
        ms = _dot(hi, bd_ref[...]) + _dot(lo, bd_ref[...])
        o_ref[...] = y * lax.rsqrt(ms + EPS) * hn_ref[0]

    @pl.when(jnp.logical_not(is_qk))
    def _():
        o_ref[...] = y


def _in_proj(x, g, w_bf, head_gain, tm):
    m, d = x.shape
    n = w_bf.shape[1]
    wdt = n // 4
    bd = np.kron(np.eye(wdt // HEAD_DIM), np.full((HEAD_DIM, HEAD_DIM), 1.0 / HEAD_DIM))
    bd = jnp.asarray(bd, BF16)
    return pl.pallas_call(
        _in_proj_kernel, name="in_proj",
        grid=(m // tm, 4),
        in_specs=[
            pl.BlockSpec((tm, d), lambda i, j: (i, 0)),
            pl.BlockSpec((1, d), lambda i, j: (0, 0)),
            pl.BlockSpec((d, wdt), lambda i, j: (0, j)),
            pl.BlockSpec((1, 1, wdt), lambda i, j: (j, 0, 0)),
            pl.BlockSpec((wdt, wdt), lambda i, j: (0, 0)),
        ],
        out_specs=pl.BlockSpec((tm, wdt), lambda i, j: (i, j)),
        out_shape=jax.ShapeDtypeStruct((m, n), F32),
        scratch_shapes=[pltpu.VMEM((tm, d), BF16)],
        compiler_params=_cparams("parallel", "arbitrary"),
    )(x, g, w_bf, head_gain, bd)


def _mix_ffn_kernel(x_ref, a_ref, b_ref, wo_ref, g_ref, wg_ref, wu_ref, wd_ref, o_ref,
                    x1_scr, h_scr, acc_scr):
    f = pl.program_id(1)
    half = a_ref.shape[1]

    @pl.when(f == 0)
    def _():
        x1 = (x_ref[...] + _dot(a_ref[...], wo_ref[:half, :]) + _dot(b_ref[...], wo_ref[half:, :]))
        x1_scr[...] = x1
        h_scr[...] = _rmsnorm(x1, g_ref[...]).astype(BF16)
        acc_scr[...] = jnp.zeros_like(acc_scr)

    h = h_scr[...]
    act = (_silu(_dot(h, wg_ref[...])) * _dot(h, wu_ref[...])).astype(BF16)
    acc_scr[...] += _dot(act, wd_ref[...])

    @pl.when(f == pl.num_programs(1) - 1)
    def _():
        o_ref[...] = x1_scr[...] + acc_scr[...]


def _mix_ffn(x, a, b, wo_bf, g, wg_bf, wu_bf, wd_bf, tm, tf):
    m, d = x.shape
    half = a.shape[1]
    ff = wg_bf.shape[1]
    return pl.pallas_call(
        _mix_ffn_kernel, name="mix_ffn",
        grid=(m // tm, ff // tf),
        in_specs=[
            pl.BlockSpec((tm, d), lambda i, f: (i, 0)),
            pl.BlockSpec((tm, half), lambda i, f: (i, 0)),
            pl.BlockSpec((tm, half), lambda i, f: (i, 0)),
            pl.BlockSpec((2 * half, d), lambda i, f: (0, 0)),
            pl.BlockSpec((1, d), lambda i, f: (0, 0)),
            pl.BlockSpec((d, tf), lambda i, f: (0, f)),
            pl.BlockSpec((d, tf), lambda i, f: (0, f)),
            pl.BlockSpec((tf, d), lambda i, f: (f, 0)),
        ],
        out_specs=pl.BlockSpec((tm, d), lambda i, f: (i, 0)),
        out_shape=jax.ShapeDtypeStruct((m, d), F32),
        scratch_shapes=[pltpu.VMEM((tm, d), F32), pltpu.VMEM((tm, d), BF16), pltpu.VMEM((tm, d), F32)],
        compiler_params=_cparams("parallel", "arbitrary"),
    )(x, a, b, wo_bf, g, wg_bf, wu_bf, wd_bf)


def _route(h, wr, br, n_exp):
    logits = _dot(h.astype(BF16), wr) + br
    lane = lax.broadcasted_iota(jnp.int32, logits.shape, 1)
    logits = jnp.where(lane < n_exp, logits, NEG)
    m1 = jnp.max(logits, axis=-1, keepdims=True)
    i1 = jnp.min(jnp.where(logits == m1, lane, LANES), axis=-1, keepdims=True)
    rest = jnp.where(lane == i1, NEG, logits)
    m2 = jnp.max(rest, axis=-1, keepdims=True)
    i2 = jnp.min(jnp.where(rest == m2, lane, LANES), axis=-1, keepdims=True)
    e2 = jnp.exp(m2 - m1)
    g1 = 1.0 / (1.0 + e2)
    g2 = e2 / (1.0 + e2)
    gate = jnp.where(lane == i1, g1, 0.0) + jnp.where(lane == i2, g2, 0.0)
    return gate, jnp.logical_or(lane == i1, lane == i2)


def _moe_dense_kernel(x_ref, g_ref, wr_ref, br_ref, wg_ref, wu_ref, wd_ref, o_ref,
                      h_scr, gate_scr, acc_scr, *, n_exp):
    e = pl.program_id(1)
    f = pl.program_id(2)

    @pl.when(jnp.logical_and(e == 0, f == 0))
    def _():
        h = _rmsnorm(x_ref[...], g_ref[...])
        h_scr[...] = h.astype(BF16)
        gate_scr[...] = _route(h, wr_ref[...], br_ref[...], n_exp)[0]
        acc_scr[...] = jnp.zeros_like(acc_scr)

    h = h_scr[...]
    lane = lax.broadcasted_iota(jnp.int32, gate_scr.shape, 1)
    w_e = jnp.sum(jnp.where(lane == e, gate_scr[...], 0.0), axis=-1, keepdims=True)
    act = (_silu(_dot(h, wg_ref[0])) * _dot(h, wu_ref[0])).astype(BF16)
    acc_scr[...] += w_e * _dot(act, wd_ref[0])

    @pl.when(jnp.logical_and(e == n_exp - 1, f == pl.num_programs(2) - 1))
    def _():
        o_ref[...] = x_ref[...] + acc_scr[...]


def _moe_dense(x, g, wr_pad, br_pad, wg_bf, wu_bf, wd_bf, tm, tf):
    m, d = x.shape
    n_exp, _, ff = wg_bf.shape
    return pl.pallas_call(
        functools.partial(_moe_dense_kernel, n_exp=n_exp), name="moe_dense",
        grid=(m // tm, n_exp, ff // tf),
        in_specs=[
            pl.BlockSpec((tm, d), lambda i, e, f: (i, 0)),
            pl.BlockSpec((1, d), lambda i, e, f: (0, 0)),
            pl.BlockSpec((d, LANES), lambda i, e, f: (0, 0)),
            pl.BlockSpec((1, LANES), lambda i, e, f: (0, 0)),
            pl.BlockSpec((1, d, tf), lambda i, e, f: (e, 0, f)),
            pl.BlockSpec((1, d, tf), lambda i, e, f: (e, 0, f)),
            pl.BlockSpec((1, tf, d), lambda i, e, f: (e, f, 0)),
        ],
        out_specs=pl.BlockSpec((tm, d), lambda i, e, f: (i, 0)),
        out_shape=jax.ShapeDtypeStruct((m, d), F32),
        scratch_shapes=[pltpu.VMEM((tm, d), BF16), pltpu.VMEM((tm, LANES), F32), pltpu.VMEM((tm, d), F32)],
        compiler_params=_cparams("parallel", "arbitrary", "arbitrary"),
    )(x, g, wr_pad, br_pad, wg_bf, wu_bf, wd_bf)


def _moe_route_kernel(x_ref, g_ref, wr_ref, br_ref, h_ref, gate_ref, pos_ref, post_ref, cb_ref, tot_ref, carry,
                      *, n_exp):
    i = pl.program_id(0)
    tm = x_ref.shape[0]

    @pl.when(i == 0)
    def _():
        carry[...] = jnp.zeros_like(carry)

    h = _rmsnorm(x_ref[...], g_ref[...])
    h_ref[...] = h.astype(BF16)
    gate, chosen = _route(h, wr_ref[...], br_ref[...], n_exp)
    gate_ref[...] = gate
    onehot = jnp.where(chosen, 1.0, 0.0)
    r = lax.broadcasted_iota(jnp.int32, (tm, tm), 0)
    c = lax.broadcasted_iota(jnp.int32, (tm, tm), 1)
    before = _dot(jnp.where(c < r, 1.0, 0.0).astype(BF16), onehot.astype(BF16))
    start = carry[...]
    pos = jnp.where(chosen, before + start, -1.0)
    pos_ref[...] = pos
    post_ref[...] = pos.T[:post_ref.shape[0], :]
    cb_ref[0] = start
    total = start + jnp.sum(onehot, axis=0, keepdims=True)
    carry[...] = total
    tot_ref[...] = total


def _moe_route(x, g, wr_bf, br, n_exp, tm):
    m, d = x.shape
    nt = m // tm
    ne8 = -(-n_exp // SUBLANES) * SUBLANES
    f = jax.ShapeDtypeStruct
    return pl.pallas_call(
        functools.partial(_moe_route_kernel, n_exp=n_exp), name="moe_route",
        grid=(nt,),
        in_specs=[
            pl.BlockSpec((tm, d), lambda i: (i, 0)),
            pl.BlockSpec((1, d), lambda i: (0, 0)),
            pl.BlockSpec((d, LANES), lambda i: (0, 0)),
            pl.BlockSpec((1, LANES), lambda i: (0, 0)),
        ],
        out_specs=[
            pl.BlockSpec((tm, d), lambda i: (i, 0)),
            pl.BlockSpec((tm, LANES), lambda i: (i, 0)),
            pl.BlockSpec((tm, LANES), lambda i: (i, 0)),
            pl.BlockSpec((ne8, tm), lambda i: (0, i)),
            pl.BlockSpec((1, 1, LANES), lambda i: (i, 0, 0)),
            pl.BlockSpec((1, LANES), lambda i: (0, 0)),
        ],
        out_shape=[f((m, d), BF16), f((m, LANES), F32), f((m, LANES), F32), f((ne8, m), F32),
                   f((nt, 1, LANES), F32), f((1, LANES), F32)],
        scratch_shapes=[pltpu.VMEM((1, LANES), F32)],
        compiler_params=_cparams("arbitrary"),
    )(x, g, wr_bf, br)


def _moe_gather_kernel(pc_ref, ps_ref, pe_ref, pfirst_ref, pvalid_ref, off_ref, h_ref, post_ref, o_ref, acc):
    k = pl.program_id(0)
    rc, tm = o_ref.shape[0], h_ref.shape[0]

    @pl.when(pfirst_ref[k] == 1)
    def _():
        acc[...] = jnp.zeros_like(acc)

    @pl.when(pvalid_ref[k] == 1)
    def _():
        e = pe_ref[k]
        dest = post_ref[pl.ds(e, 1), :] + off_ref[e].astype(F32)
        rows = (pc_ref[k] * rc + lax.broadcasted_iota(jnp.int32, (rc, tm), 0)).astype(F32)
        acc[...] += _dot(jnp.where(dest == rows, 1.0, 0.0).astype(BF16), h_ref[...])

    o_ref[...] = acc[...].astype(BF16)


def _moe_ffn_kernel(ce_ref, cv_ref, x_ref, wg_ref, wu_ref, wd_ref, y_ref, *, tf):
    c = pl.program_id(0)

    @pl.when(cv_ref[c] == 1)
    def _():
        x = x_ref[...]
        acc = jnp.zeros(y_ref.shape, F32)
        for f0 in range(0, wg_ref.shape[2], tf):
            act = (_silu(_dot(x, wg_ref[0, :, f0:f0 + tf])) * _dot(x, wu_ref[0, :, f0:f0 + tf])).astype(BF16)
            acc = acc + _dot(act, wd_ref[0, f0:f0 + tf, :])
        y_ref[...] = acc

    @pl.when(cv_ref[c] == 0)
    def _():
        y_ref[...] = jnp.zeros_like(y_ref)


def _moe_combine_kernel(qs_ref, qc_ref, qe_ref, qfirst_ref, qvalid_ref, off_ref, x_ref, y_ref, pos_ref, gate_ref,
                        o_ref):
    k = pl.program_id(0)
    tm, rc = x_ref.shape[0], y_ref.shape[0]

    @pl.when(qfirst_ref[k] == 1)
    def _():
        o_ref[...] = x_ref[...]

    @pl.when(qvalid_ref[k] == 1)
    def _():
        e = qe_ref[k]
        lane = lax.broadcasted_iota(jnp.int32, pos_ref.shape, 1)
        pos_e = jnp.sum(jnp.where(lane == e, pos_ref[...], 0.0), axis=-1, keepdims=True)
        gate_e = jnp.sum(jnp.where(lane == e, gate_ref[...], 0.0), axis=-1, keepdims=True)
        dest = pos_e + off_ref[e].astype(F32)
        cols = (qc_ref[k] * rc + lax.broadcasted_iota(jnp.int32, (tm, rc), 1)).astype(F32)
        pick = jnp.where(dest == cols, 1.0, 0.0).astype(BF16)
        hi, lo = _split_bf16(y_ref[...])
        o_ref[...] += gate_e * (_dot(pick, hi) + _dot(pick, lo))


def _pair_list(inter, n_pairs_max):
    nb = inter.shape[1]
    flat = inter.reshape(-1)
    n = jnp.sum(flat.astype(jnp.int32))
    idx = jnp.nonzero(flat, size=n_pairs_max, fill_value=0)[0].astype(jnp.int32)
    k = jnp.arange(n_pairs_max, dtype=jnp.int32)
    idx = jnp.where(k < n, idx, idx[jnp.maximum(n - 1, 0)])
    a, b = idx // nb, idx % nb
    valid = k < n
    first = jnp.logical_and(valid, jnp.logical_or(k == 0, a != jnp.roll(a, 1)))
    return a, b, valid.astype(jnp.int32), first.astype(jnp.int32)


ROW_CHUNK = 256


def _moe_sparse(x, g, wr_bf, br, wg_bf, wu_bf, wd_bf, tm, tf):
    m, d = x.shape
    n_exp, _, ff = wg_bf.shape
    nt = m // tm
    rc = ROW_CHUNK
    h, gate, pos, pos_t, cb, tot = _moe_route(x, g, wr_bf, br, n_exp, tm)

    cnt = tot[0, :n_exp].astype(jnp.int32)
    cbi = jnp.concatenate([cb[:, 0, :n_exp], tot[:, :n_exp]], axis=0).astype(jnp.int32)
    gsz = (cnt + rc - 1) // rc * rc
    ends = jnp.cumsum(gsz)
    off = (ends - gsz).astype(jnp.int32)
    n_chunks = TOP_K_EXPERTS * m // rc + n_exp
    cstart = jnp.arange(n_chunks, dtype=jnp.int32) * rc
    ce = jnp.minimum(jnp.searchsorted(ends, cstart, side='right'), n_exp - 1).astype(jnp.int32)
    cv = cstart < ends[-1]
    per_tile = cbi.T[ce]
    start = off[ce][:, None] + per_tile[:, :-1]
    end = off[ce][:, None] + per_tile[:, 1:]
    inter = (cv[:, None] & (end > start) & (start < cstart[:, None] + rc) & (end > cstart[:, None]))
    n_pairs = n_chunks + n_exp * nt
    unused = jnp.logical_and(jnp.logical_not(cv)[:, None], jnp.arange(nt)[None, :] == 0)
    pc, ps, plisted, pfirst = _pair_list(jnp.logical_or(inter, unused), n_pairs + n_exp)
    pvalid = plisted * inter[pc, ps].astype(jnp.int32)
    qs, qc, qvalid, qfirst = _pair_list(inter.T, n_pairs)

    xs = pl.pallas_call(
        _moe_gather_kernel, name="moe_gather",
        grid_spec=pltpu.PrefetchScalarGridSpec(
            num_scalar_prefetch=6,
            grid=(n_pairs + n_exp,),
            in_specs=[
                pl.BlockSpec((tm, d), lambda k, pc, ps, *_: (ps[k], 0)),
                pl.BlockSpec((pos_t.shape[0], tm), lambda k, pc, ps, *_: (0, ps[k])),
            ],
            out_specs=pl.BlockSpec((rc, d), lambda k, pc, *_: (pc[k], 0)),
            scratch_shapes=[pltpu.VMEM((rc, d), F32)],
        ),
        out_shape=jax.ShapeDtypeStruct((n_chunks * rc, d), BF16),
        compiler_params=_cparams("arbitrary"),
    )(pc, ps, ce[pc], pfirst, pvalid, off, h, pos_t)

    cvi = cv.astype(jnp.int32)
    wspec = lambda shape: pl.BlockSpec(shape, lambda c, ce, cv: (ce[c], 0, 0), pipeline_mode=pl.Buffered(1))
    ys = pl.pallas_call(
        functools.partial(_moe_ffn_kernel, tf=tf), name="moe_ffn",
        grid_spec=pltpu.PrefetchScalarGridSpec(
            num_scalar_prefetch=2,
            grid=(n_chunks,),
            in_specs=[
                pl.BlockSpec((rc, d), lambda c, ce, cv: (c * cv[c], 0)),
                wspec((1, d, ff)), wspec((1, d, ff)), wspec((1, ff, d)),
            ],
            out_specs=pl.BlockSpec((rc, d), lambda c, ce, cv: (c, 0)),
        ),
        out_shape=jax.ShapeDtypeStruct((n_chunks * rc, d), F32),
        compiler_params=_cparams("arbitrary"),
    )(ce, cvi, xs, wg_bf, wu_bf, wd_bf)

    return pl.pallas_call(
        _moe_combine_kernel, name="moe_combine",
        grid_spec=pltpu.PrefetchScalarGridSpec(
            num_scalar_prefetch=6,
            grid=(n_pairs,),
            in_specs=[
                pl.BlockSpec((tm, d), lambda k, qs, qc, *_: (qs[k], 0)),
                pl.BlockSpec((rc, d), lambda k, qs, qc, *_: (qc[k], 0)),
                pl.BlockSpec((tm, LANES), lambda k, qs, *_: (qs[k], 0)),
                pl.BlockSpec((tm, LANES), lambda k, qs, *_: (qs[k], 0)),
            ],
            out_specs=pl.BlockSpec((tm, d), lambda k, qs, *_: (qs[k], 0)),
        ),
        out_shape=jax.ShapeDtypeStruct((m, d), F32),
        compiler_params=_cparams("arbitrary"),
    )(qs, qc, ce[qc], qfirst, qvalid, off, x, ys, pos, gate)


HALO = 16


def _pool_kernel(x_ref, g_ref, wp_ref, sc_ref, o_ref, hist_ref, hbuf):
    t = pl.program_id(1)
    tm = x_ref.shape[1]
    gw = wp_ref.shape[1]

    @pl.when(t == 0)
    def _():
        hbuf[0:HALO, :] = jnp.zeros((HALO, hbuf.shape[1]), F32)

    @pl.when(t > 0)
    def _():
        hbuf[0:HALO, :] = hbuf[tm:tm + HALO, :]

    x = x_ref[0]
    hn = _rmsnorm(x, g_ref[...])
    hbuf[HALO:HALO + tm, :] = hn
    pos = t * tm + lax.broadcasted_iota(jnp.int32, (tm, 1), 0)
    ys = []
    for gi, w in enumerate(POOL_WINDOWS):
        sl = slice(gi * gw, (gi + 1) * gw)
        tot = hn[:, sl]
        for i in range(1, w):
            tot = tot + hbuf[HALO - i:HALO - i + tm, sl]
        cnt = jnp.minimum(pos + 1, w).astype(F32)
        pooled = tot / cnt - hn[:, sl]
        ys.append(_dot(pooled.astype(BF16), wp_ref[gi]))
    o_ref[0] = x + jnp.concatenate(ys, axis=1) * sc_ref[...]

    @pl.when(t == pl.num_programs(1) - 1)
    def _():
        hist_ref[0] = hbuf[tm:tm + HALO, :]


def _pool_prompt(x, g, wp_bf, sc, tm):
    b, s, d = x.shape
    ng, gw, _ = wp_bf.shape
    return pl.pallas_call(
        _pool_kernel, name="pool_prompt",
        grid=(b, s // tm),
        in_specs=[
            pl.BlockSpec((1, tm, d), lambda i, t: (i, t, 0)),
            pl.BlockSpec((1, d), lambda i, t: (0, 0)),
            pl.BlockSpec((ng, gw, gw), lambda i, t: (0, 0, 0)),
            pl.BlockSpec((1, d), lambda i, t: (0, 0)),
        ],
        out_specs=[
            pl.BlockSpec((1, tm, d), lambda i, t: (i, t, 0)),
            pl.BlockSpec((1, HALO, d), lambda i, t: (i, 0, 0)),
        ],
        out_shape=[jax.ShapeDtypeStruct((b, s, d), F32), jax.ShapeDtypeStruct((b, HALO, d), F32)],
        scratch_shapes=[pltpu.VMEM((HALO + tm, d), F32)],
        compiler_params=_cparams("parallel", "arbitrary"),
    )(x, g, wp_bf, sc)


def _pool_step_kernel(x_ref, g_ref, hist_ref, wp_ref, sc_ref, o_ref, hn_ref, *, n_valid):
    gw = wp_ref.shape[1]
    x = x_ref[...]
    hn = _rmsnorm(x, g_ref[...])
    hn_ref[...] = hn
    ys = []
    for gi, w in enumerate(POOL_WINDOWS):
        sl = slice(gi * gw, (gi + 1) * gw)
        tot = hn[:, sl]
        for i in range(1, min(w, n_valid + 1)):
            tot = tot + hist_ref[POOL_HIST - i][:, sl]
        pooled = tot / float(min(w, n_valid + 1)) - hn[:, sl]
        ys.append(_dot(pooled.astype(BF16), wp_ref[gi]))
    o_ref[...] = x + jnp.concatenate(ys, axis=1) * sc_ref[...]


def _pool_step(x, g, hist_t, wp_bf, sc, n_valid):
    b, d = x.shape
    return pl.pallas_call(
        functools.partial(_pool_step_kernel, n_valid=n_valid), name="pool_step",
        out_shape=[jax.ShapeDtypeStruct((b, d), F32), jax.ShapeDtypeStruct((b, d), F32)],
        compiler_params=pltpu.CompilerParams(vmem_limit_bytes=VMEM_LIMIT),
    )(x, g, hist_t, wp_bf, sc)


N_POW = SUBLANES


def _s5_prep_kernel(ar_ref, ai_ref, ldt_ref, br_ref, bi_ref, pwr_ref, pwi_ref, bbr_ref, bbi_ref):
    ar = ar_ref[...]
    ai = ai_ref[...]
    dt = jnp.exp(ldt_ref[...])
    mag = jnp.exp(ar * dt)
    lr = mag * jnp.cos(ai * dt)
    li = mag * jnp.sin(ai * dt)
    den = ar * ar + ai * ai
    rr = ((lr - 1.0) * ar + li * ai) / den
    ri = (li * ar - (lr - 1.0) * ai) / den
    br = br_ref[...]
    bi = bi_ref[...]
    bbr_ref[...] = rr * br - ri * bi
    bbi_ref[...] = rr * bi + ri * br
    cr, ci = lr, li
    for k in range(N_POW):
        pwr_ref[k] = cr
        pwi_ref[k] = ci
        cr, ci = cr * lr - ci * li, cr * li + ci * lr


def _s5_prep(a_re, a_im, log_dt, b_re, b_im):
    g, n = a_re.shape
    p = b_re.shape[2]
    f = jax.ShapeDtypeStruct
    return pl.pallas_call(
        _s5_prep_kernel, name="s5_prep",
        out_shape=[f((N_POW, g, 1, n), F32), f((N_POW, g, 1, n), F32), f((g, p, n), F32), f((g, p, n), F32)],
    )(a_re[:, None, :], a_im[:, None, :], log_dt[:, None, None],
      jnp.transpose(b_re, (0, 2, 1)), jnp.transpose(b_im, (0, 2, 1)))


def _s5_weights(a_re, a_im, log_dt, b_re, b_im, c_re, c_im):
    g, n = a_re.shape
    p = b_re.shape[2]
    gn = g * n
    pwr, pwi, bbr, bbi = _s5_prep(a_re, a_im, log_dt, b_re, b_im)
    pwr = pwr.reshape(N_POW, gn)
    pwi = pwi.reshape(N_POW, gn)
    row = np.arange(SUBLANES)[:, None]
    tabs = []
    for k in (1, 2, 4):
        tabs += [jnp.where(row >= k, pwr[k - 1][None], 0.0), jnp.where(row >= k, pwi[k - 1][None], 0.0)]
    tabs += [pwr, pwi]
    tabs = jnp.stack(tabs)
    lam = jnp.stack([pwr[0], pwi[0]])
    eye = jnp.eye(g, dtype=F32)
    b_dense = [jnp.einsum('gpn,gh->gphn', bb, eye).reshape(g * p, gn) for bb in (bbr, bbi)]
    n_tiles = gn // LANES
    u_per_tile = LANES // n * p
    wb = []
    for j in range(n_tiles):
        c0 = (j * u_per_tile) // LANES * LANES
        wb.append(jnp.concatenate([bd[c0:c0 + LANES, j * LANES:(j + 1) * LANES] for bd in b_dense], axis=1))
    wb = jnp.stack(wb).astype(BF16)
    c_dense = [jnp.einsum('gpn,gh->gnhp', cc, eye).reshape(gn, g * p) for cc in (c_re, -c_im)]
    k_per_tile = LANES // p * n
    wc = []
    for m in range(g * p // LANES):
        wc.append(jnp.concatenate([cd[m * k_per_tile:(m + 1) * k_per_tile, m * LANES:(m + 1) * LANES]
                                   for cd in c_dense], axis=0))
    wc = jnp.stack(wc).astype(BF16)
    return wb, tabs, lam, wc


def _s5_drive(u, wb_ref, bur, bui, row0):
    t = u.shape[0]
    ub = u.astype(BF16)
    n_tiles = wb_ref.shape[0]
    u_per_tile = u.shape[1] // n_tiles
    for j in range(n_tiles):
        c0 = (j * u_per_tile) // LANES * LANES
        r = _dot(ub[:, c0:c0 + LANES], wb_ref[j])
        bur[row0:row0 + t, j * LANES:(j + 1) * LANES] = r[:, :LANES]
        bui[row0:row0 + t, j * LANES:(j + 1) * LANES] = r[:, LANES:]


def _s5_readout(u, hr, hi, wc_ref, d_ref, wgl_ref, bgl_ref):
    hrb = hr.astype(BF16)
    hib = hi.astype(BF16)
    kk = wc_ref.shape[1] // 2
    ys = []
    for m in range(wc_ref.shape[0]):
        ys.append(_dot(hrb[:, m * kk:(m + 1) * kk], wc_ref[m, :kk, :])
                  + _dot(hib[:, m * kk:(m + 1) * kk], wc_ref[m, kk:, :]))
    y = jnp.concatenate(ys, axis=1) + d_ref[...] * u
    z = jax.nn.gelu(y, approximate=True)
    return z * jax.nn.sigmoid(_dot(z.astype(BF16), wgl_ref[...]) + bgl_ref[...])


def _s5_scan_kernel(u_ref, h0_ref, wb_ref, tab_ref, wc_ref, d_ref, wgl_ref, bgl_ref, o_ref, hf_ref, bur, bui):
    t = pl.program_id(1)
    tt = u_ref.shape[1]
    gn = bur.shape[1]
    c0 = SUBLANES

    @pl.when(t == 0)
    def _():
        bur[0:c0, :] = jnp.broadcast_to(h0_ref[0, :, :gn], (c0, gn))
        bui[0:c0, :] = jnp.broadcast_to(h0_ref[0, :, gn:], (c0, gn))

    @pl.when(t > 0)
    def _():
        bur[0:c0, :] = bur[tt:tt + c0, :]
        bui[0:c0, :] = bui[tt:tt + c0, :]

    u = u_ref[0]
    _s5_drive(u, wb_ref, bur, bui, c0)

    def block(b, carry):
        base = pl.multiple_of(b * SUBLANES, SUBLANES)
        cr = bur[pl.ds(base + c0 - 1, 1), :]
        ci = bui[pl.ds(base + c0 - 1, 1), :]
        xr = bur[pl.ds(base + c0, SUBLANES), :]
        xi = bui[pl.ds(base + c0, SUBLANES), :]
        for s, k in enumerate((1, 2, 4)):
            ar = tab_ref[2 * s]
            ai = tab_ref[2 * s + 1]
            sr = pltpu.roll(xr, k, axis=0)
            si = pltpu.roll(xi, k, axis=0)
            xr, xi = xr + ar * sr - ai * si, xi + ar * si + ai * sr
        pr = tab_ref[6]
        pi = tab_ref[7]
        bur[pl.ds(base + c0, SUBLANES), :] = xr + pr * cr - pi * ci
        bui[pl.ds(base + c0, SUBLANES), :] = xi + pr * ci + pi * cr
        return carry

    lax.fori_loop(0, tt // SUBLANES, block, 0)
    o_ref[0] = _s5_readout(u, bur[c0:c0 + tt, :], bui[c0:c0 + tt, :], wc_ref, d_ref, wgl_ref, bgl_ref).astype(BF16)

    @pl.when(t == pl.num_programs(1) - 1)
    def _():
        hf_ref[0] = jnp.concatenate([bur[tt + c0 - 1:tt + c0, :], bui[tt + c0 - 1:tt + c0, :]], axis=1)


def _s5_scan(proj, h0, wb, tabs, wc, dsk, wgl_bf, bgl, tt):
    b, s, _ = proj.shape
    w = dsk.shape[1]
    gn = tabs.shape[2]
    full = lambda a: pl.BlockSpec(a.shape, lambda i, t: (0,) * a.ndim)
    return pl.pallas_call(
        _s5_scan_kernel, name="s5_scan",
        grid=(b, s // tt),
        in_specs=[
            pl.BlockSpec((1, tt, w), lambda i, t: (i, t, 0)),
            pl.BlockSpec((1, 1, 2 * gn), lambda i, t: (i, 0, 0)),
            full(wb), full(tabs), full(wc), full(dsk), full(wgl_bf), full(bgl),
        ],
        out_specs=[
            pl.BlockSpec((1, tt, w), lambda i, t: (i, t, 0)),
            pl.BlockSpec((1, 1, 2 * gn), lambda i, t: (i, 0, 0)),
        ],
        out_shape=[jax.ShapeDtypeStruct((b, s, w), BF16), jax.ShapeDtypeStruct((b, 1, 2 * gn), F32)],
        scratch_shapes=[pltpu.VMEM((tt + SUBLANES, gn), F32), pltpu.VMEM((tt + SUBLANES, gn), F32)],
        compiler_params=_cparams("parallel", "arbitrary"),
    )(proj, h0, wb, tabs, wc, dsk, wgl_bf, bgl)


def _s5_step_kernel(u_ref, h0r_ref, h0i_ref, wb_ref, lam_ref, wc_ref, d_ref, wgl_ref, bgl_ref,
                    o_ref, hr_ref, hi_ref, bur, bui):
    u = u_ref[...]
    _s5_drive(u, wb_ref, bur, bui, 0)
    lr = lam_ref[0:1, :]
    li = lam_ref[1:2, :]
    h0r = h0r_ref[...]
    h0i = h0i_ref[...]
    hr = bur[...] + lr * h0r - li * h0i
    hi = bui[...] + lr * h0i + li * h0r
    hr_ref[...] = hr
    hi_ref[...] = hi
    o_ref[...] = _s5_readout(u, hr, hi, wc_ref, d_ref, wgl_ref, bgl_ref).astype(BF16)


def _s5_step(proj, h0r, h0i, wb, lam, wc, dsk, wgl_bf, bgl):
    b = proj.shape[0]
    w = dsk.shape[1]
    gn = lam.shape[1]
    full = lambda a: pl.BlockSpec(a.shape, lambda i: (0,) * a.ndim)
    return pl.pallas_call(
        _s5_step_kernel, name="s5_step",
        grid=(1,),
        in_specs=[pl.BlockSpec((b, w), lambda i: (0, 0)), full(h0r), full(h0i), full(wb), full(lam), full(wc),
                  full(dsk), full(wgl_bf), full(bgl)],
        out_specs=[pl.BlockSpec((b, w), lambda i: (0, 0)), pl.BlockSpec((b, gn), lambda i: (0, 0)),
                   pl.BlockSpec((b, gn), lambda i: (0, 0))],
        out_shape=[jax.ShapeDtypeStruct((b, w), BF16), jax.ShapeDtypeStruct((b, gn), F32),
                   jax.ShapeDtypeStruct((b, gn), F32)],
        scratch_shapes=[pltpu.VMEM((b, gn), F32), pltpu.VMEM((b, gn), F32)],
        compiler_params=_cparams("arbitrary"),
    )(proj, h0r, h0i, wb, lam, wc, dsk, wgl_bf, bgl)


def _t5_bucket(rel):
    n = jnp.maximum(rel, 0)
    max_exact = REL_BUCKETS // 2
    nf = jnp.maximum(n, 1).astype(F32)
    large = max_exact + (jnp.log(nf / max_exact) / math.log(REL_MAX_DIST / max_exact)
                         * (REL_BUCKETS - max_exact)).astype(jnp.int32)
    large = jnp.minimum(large, REL_BUCKETS - 1)
    return jnp.where(n < max_exact, n, large)


N_DIST = -(-(REL_MAX_DIST + MOBA_BLOCK - 1) // MOBA_BLOCK) + 1


def _prompt_bias_tiles(rel_bias):
    blk = MOBA_BLOCK
    span = 2 * blk - 1
    rel = jnp.asarray(np.arange(-(blk - 1), N_DIST * blk), jnp.int32)
    line = jnp.where(rel >= 0, rel_bias.astype(F32).T[:, _t5_bucket(rel)], NEG)
    diag = jnp.stack([line[:, d * blk:d * blk + span] for d in range(N_DIST)], axis=1)
    line_rc = jnp.roll(diag[..., ::-1], -(blk - 1), axis=-1)
    rep = jnp.tile(line_rc, (1, 1, blk))[..., :blk * (span - 1)]
    return rep.reshape(diag.shape[0], N_DIST, blk, span - 1)[..., :blk]


def _rank_select(g, n_valid, idx, n_cand, axis):
    cnt = jnp.zeros(g.shape, jnp.int32)
    for jp in range(n_cand):
        gp = lax.slice_in_dim(g, jp, jp + 1, axis=axis)
        beats = jnp.logical_or(gp > g, jnp.logical_and(gp == g, jp < idx))
        cnt = cnt + jnp.where(jnp.logical_and(beats, jp < n_valid), 1, 0)
    return jnp.logical_and(idx < n_valid, cnt < MOBA_TOPK)


SWEEP_UNROLL = 4


def _moba_kernel(q_ref, k_ref, v_ref, bias_ref, o_ref, kft, vb, km, qa, sbuf, red, stat, acc_scr):
    i = pl.program_id(2)
    blk = MOBA_BLOCK
    nb = k_ref.shape[1] // blk
    n_hd = LANES // HEAD_DIM
    scale = HEAD_DIM ** -0.5

    @pl.when(i == 0)
    def _():
        feat = lax.broadcasted_iota(jnp.int32, (LANES, blk), 0)
        for j in range(nb):
            kj = k_ref[0, j * blk:(j + 1) * blk, :]
            kft[j, :LANES, :] = kj.T.astype(BF16)
            kft[j, LANES:, :] = jnp.where(feat == j, 1.0, 0.0).astype(BF16)
            km[j:j + 1, :] = jnp.mean(kj, axis=0, keepdims=True)
        vb[...] = v_ref[0].astype(BF16)

    lane = lax.broadcasted_iota(jnp.int32, (blk, LANES), 1)
    wide = lambda a: jnp.concatenate([a, a], axis=1)
    rep = lambda a: jnp.broadcast_to(a, (blk, LANES))
    for hd in range(n_hd):
        qm = (jnp.where(lane // HEAD_DIM == hd, q_ref[0], 0.0) * scale).astype(BF16)
        g = _dot_nt(km[...].astype(BF16), qm)
        jrow = lax.broadcasted_iota(jnp.int32, g.shape, 0)
        keep = jnp.logical_or(_rank_select(g, i, jrow, nb, axis=0), jrow == i)
        pen = jnp.concatenate([jnp.where(keep, 0.0, NEG), jnp.zeros((LANES - nb, blk), F32)], axis=0)
        qa[hd, :, :LANES] = qm
        qa[hd, :, LANES:] = pen.T.astype(BF16)
        red[hd] = jnp.full((blk, LANES), NEG, F32)

    def sweep(step):
        def group(t, carry):
            for u in range(SWEEP_UNROLL):
                step(SWEEP_UNROLL * t + u)
            return carry

        def single(j, carry):
            step(j)
            return carry

        n_grouped = (i + 1) // SWEEP_UNROLL
        lax.fori_loop(0, n_grouped, group, 0)
        lax.fori_loop(n_grouped * SWEEP_UNROLL, i + 1, single, 0)

    def scores(j):
        dist = jnp.minimum(i - j, N_DIST - 1)
        kt = kft[j]
        for hd in range(n_hd):
            s = _dot(qa[hd], kt) + bias_ref[hd, dist]
            sbuf[hd, j] = s
            red[hd] = jnp.maximum(red[hd], jnp.maximum(s[:, :LANES], s[:, LANES:]))

    sweep(scores)
    for hd in range(n_hd):
        stat[hd] = rep(jnp.max(red[hd], axis=-1, keepdims=True))
        red[hd] = jnp.zeros((blk, LANES), F32)

    def exps(j):
        for hd in range(n_hd):
            e = jnp.exp(sbuf[hd, j] - wide(stat[hd]))
            sbuf[hd, j] = e
            red[hd] += e[:, :LANES] + e[:, LANES:]

    sweep(exps)
    for hd in range(n_hd):
        stat[hd] = rep(1.0 / jnp.sum(red[hd], axis=-1, keepdims=True))
        acc_scr[hd] = jnp.zeros((blk, LANES), F32)

    def weighted(j):
        vj = vb[pl.ds(pl.multiple_of(j * blk, blk), blk), :]
        for hd in range(n_hd):
            p = (sbuf[hd, j] * wide(stat[hd])).astype(BF16)
            acc_scr[hd] += _dot(p, vj)

    sweep(weighted)
    o = acc_scr[0]
    for hd in range(1, n_hd):
        o = jnp.where(lane // HEAD_DIM == hd, acc_scr[hd], o)
    o_ref[0] = o.astype(BF16)


def _moba_prompt(proj, bias_tiles):
    b, s, n = proj.shape
    w = n // 4
    hp = w // LANES
    heads_per = LANES // HEAD_DIM
    blk = MOBA_BLOCK
    assert s % blk == 0 and s // blk <= LANES
    return pl.pallas_call(
        _moba_kernel, name="moba_prompt",
        grid=(b, hp, s // blk),
        in_specs=[
            pl.BlockSpec((1, blk, LANES), lambda bi, h, i: (bi, i, hp + h)),
            pl.BlockSpec((1, s, LANES), lambda bi, h, i: (bi, 0, 2 * hp + h)),
            pl.BlockSpec((1, s, LANES), lambda bi, h, i: (bi, 0, 3 * hp + h)),
            pl.BlockSpec((heads_per, N_DIST, blk, blk), lambda bi, h, i: (h, 0, 0, 0)),
        ],
        out_specs=pl.BlockSpec((1, blk, LANES), lambda bi, h, i: (bi, i, h)),
        out_shape=jax.ShapeDtypeStruct((b, s, w), BF16),
        scratch_shapes=[
            pltpu.VMEM((s // blk, 2 * LANES, blk), BF16), pltpu.VMEM((s, LANES), BF16),
            pltpu.VMEM((s // blk, LANES), F32), pltpu.VMEM((heads_per, blk, 2 * LANES), BF16),
            pltpu.VMEM((heads_per, s // blk, blk, blk), F32), pltpu.VMEM((heads_per, blk, LANES), F32),
            pltpu.VMEM((heads_per, blk, LANES), F32), pltpu.VMEM((heads_per, blk, LANES), F32),
        ],
        compiler_params=_cparams("parallel", "parallel", "arbitrary"),
    )(proj, proj, proj, bias_tiles)


def _rnd(a):
    return a.astype(BF16).astype(F32)


DEC_GROUP = 8
PAGES_PER_BLOCK = 2


def _dec_scores_kernel(pt_ref, q_ref, *refs):
    del pt_ref
    k_refs, (bias_ref, s_ref, ks_ref) = refs[:-3], refs[-3:]
    nh = s_ref.shape[2]
    hd = nh * HEAD_DIM
    page = k_refs[0].shape[-1]
    q = (q_ref[0] * HEAD_DIM ** -0.5).astype(BF16)
    ones = jnp.ones((SUBLANES, page), BF16)
    for g in range(s_ref.shape[1]):
        kk = [r[0, 0].reshape(hd, page) for r in k_refs[PAGES_PER_BLOCK * g:PAGES_PER_BLOCK * (g + 1)]]
        s_ref[0, g] = jnp.concatenate([_dot(q, k.astype(BF16)) for k in kk], axis=1) + bias_ref[g]
        hi, lo = _split_bf16(kk[0] + kk[1])
        ks_ref[0, g] = (_dot_nt(ones, hi) + _dot_nt(ones, lo))[0:1]


def _dec_scores(qx, cache_kt, page_table, bias_dec, layer):
    b, nh, hd = qx.shape
    page = cache_kt.shape[4]
    assert MOBA_BLOCK == PAGES_PER_BLOCK * page
    nblk = page_table.shape[1] // PAGES_PER_BLOCK
    grp = math.gcd(DEC_GROUP, nblk)
    n_pg = PAGES_PER_BLOCK * grp
    pg = lambda t: pl.BlockSpec((1, 1, nh, HEAD_DIM, page),
                                lambda bi, j, pt: (pt[bi, n_pg * j + t], layer, 0, 0, 0))
    out = lambda rows, last: pl.BlockSpec((1, grp, rows, last), lambda bi, j, pt: (bi, j, 0, 0))
    return pl.pallas_call(
        _dec_scores_kernel, name="moba_dec_scores",
        grid_spec=pltpu.PrefetchScalarGridSpec(
            num_scalar_prefetch=1,
            grid=(b, nblk // grp),
            in_specs=[pl.BlockSpec((1, nh, hd), lambda bi, j, pt: (bi, 0, 0))]
            + [pg(t) for t in range(n_pg)]
            + [pl.BlockSpec((grp, nh, MOBA_BLOCK), lambda bi, j, pt: (j, 0, 0))],
            out_specs=[out(nh, MOBA_BLOCK), out(1, hd)],
        ),
        out_shape=[jax.ShapeDtypeStruct((b, nblk, nh, MOBA_BLOCK), F32), jax.ShapeDtypeStruct((b, nblk, 1, hd), F32)],
        compiler_params=_cparams("parallel", "arbitrary"),
    )(page_table, qx, *([cache_kt] * n_pg), bias_dec)


def _dec_select_kernel(q_ref, kn_ref, b0_ref, ks_ref, s_ref, p_ref, pown_ref, sel_ref):
    nblk = sel_ref.shape[1]
    blk = MOBA_BLOCK
    q = _rnd(q_ref[...] * HEAD_DIM ** -0.5)
    lane = lax.broadcasted_iota(jnp.int32, sel_ref.shape, 1)
    gate = jnp.zeros(sel_ref.shape, F32)
    for j in range(nblk):
        kmean = _rnd(ks_ref[:, j * HEAD_DIM:(j + 1) * HEAD_DIM] * (1.0 / blk))
        gate = jnp.where(lane == j, jnp.sum(q * kmean, axis=-1, keepdims=True), gate)
    sel = _rank_select(gate, nblk, lane, nblk, axis=1)
    sel_ref[...] = jnp.where(sel, 1.0, 0.0)
    s_own = jnp.sum(q * _rnd(kn_ref[...]), axis=-1, keepdims=True) + b0_ref[:, 0:1]
    m = s_own
    for j in range(nblk):
        mj = jnp.max(s_ref[:, j * blk:(j + 1) * blk], axis=-1, keepdims=True)
        m = jnp.maximum(m, jnp.where(sel[:, j:j + 1], mj, NEG))
    e_own = jnp.exp(s_own - m)
    den = e_own
    for j in range(nblk):
        e = jnp.where(sel[:, j:j + 1], jnp.exp(s_ref[:, j * blk:(j + 1) * blk] - m), 0.0)
        p_ref[:, j * blk:(j + 1) * blk] = e
        den = den + jnp.sum(e, axis=-1, keepdims=True)
    inv = 1.0 / den
    for j in range(nblk):
        p_ref[:, j * blk:(j + 1) * blk] = _rnd(p_ref[:, j * blk:(j + 1) * blk] * inv)
    pown_ref[...] = jnp.broadcast_to(_rnd(e_own * inv), pown_ref.shape)


def _dec_pv_kernel(pt_ref, src_ref, need_ref, p_ref, *refs):
    del pt_ref, src_ref
    v_refs, (pown_ref, vn_ref, o_ref) = refs[:-3], refs[-3:]
    bi = pl.program_id(0)
    j = pl.program_id(1)
    grp = p_ref.shape[1]
    nh = p_ref.shape[2]
    hd = nh * HEAD_DIM
    page = v_refs[0].shape[-1]

    @pl.when(j == 0)
    def _():
        o_ref[0] = pown_ref[0] * _rnd(vn_ref[0])

    for g in range(grp):
        @pl.when(need_ref[bi, j * grp + g] == 1)
        def _(g=g):
            pb = p_ref[0, g].astype(BF16)
            vv = [r[0, 0].reshape(hd, page).astype(BF16) for r in v_refs[PAGES_PER_BLOCK * g:PAGES_PER_BLOCK * (g + 1)]]
            o_all = _dot_nt(pb[:, :page], vv[0]) + _dot_nt(pb[:, page:], vv[1])
            col = lax.broadcasted_iota(jnp.int32, o_all.shape, 1)
            row = lax.broadcasted_iota(jnp.int32, o_all.shape, 0)
            o_ref[0] += jnp.sum(jnp.where(col // HEAD_DIM == row, o_all, 0.0), axis=0, keepdims=True)


def _moba_decode(proj_s, cache_kt, cache_vt, page_table, rel_bias, layer):
    b, n = proj_s.shape
    w = n // 4
    nh = w // HEAD_DIM
    page = cache_kt.shape[4]
    past_len = page_table.shape[1] * page
    assert past_len % MOBA_BLOCK == 0
    nblk = past_len // MOBA_BLOCK
    q = proj_s[:, w:2 * w].reshape(b, nh, HEAD_DIM)
    qx = jnp.einsum('bhd,hg->bhgd', q, jnp.eye(nh, dtype=F32)).reshape(b, nh, w)
    kpos = np.arange(past_len).reshape(nblk, MOBA_BLOCK)
    bias = rel_bias.astype(F32).T[:, _t5_bucket(jnp.asarray(past_len - kpos, jnp.int32))]
    s_p, ks_p = _dec_scores(qx, cache_kt, page_table, jnp.transpose(bias, (1, 0, 2)), layer)
    rows = b * nh
    blk = MOBA_BLOCK
    to_rows = lambda a, last: jnp.transpose(a.reshape(b, nblk, nh, last), (0, 2, 1, 3)).reshape(rows, nblk * last)
    b0 = jnp.broadcast_to(jnp.tile(rel_bias.astype(F32)[_t5_bucket(jnp.zeros((), jnp.int32))], b)[:, None],
                          (rows, LANES))
    f = jax.ShapeDtypeStruct
    p_rows, p_own, sel = pl.pallas_call(
        _dec_select_kernel, name="moba_dec_select",
        out_shape=[f((rows, nblk * blk), F32), f((rows, LANES), F32), f((rows, nblk), F32)],
        compiler_params=pltpu.CompilerParams(vmem_limit_bytes=VMEM_LIMIT),
    )(q.reshape(rows, HEAD_DIM), proj_s[:, 2 * w:3 * w].reshape(rows, HEAD_DIM), b0,
      to_rows(ks_p, HEAD_DIM), to_rows(s_p, blk))

    grp = math.gcd(DEC_GROUP, nblk)
    need = (jnp.max(sel.reshape(b, nh, nblk), axis=1) > 0.5).reshape(b, nblk // grp, grp)
    blk_id = jnp.arange(nblk, dtype=jnp.int32).reshape(1, nblk // grp, grp)
    last_needed = lax.cummax(jnp.where(need, blk_id, -1), axis=1)
    first_needed = jnp.min(jnp.where(need, blk_id, nblk), axis=1, keepdims=True)
    fallback = jnp.where(first_needed < nblk, first_needed, blk_id[:, :1])
    src = jnp.where(last_needed < 0, fallback, last_needed).astype(jnp.int32).reshape(b, nblk)
    p_blk = jnp.transpose(p_rows.reshape(b, nh, nblk, blk), (0, 2, 1, 3))
    p_own_x = jnp.repeat(p_own[:, 0].reshape(b, nh), HEAD_DIM, axis=1).reshape(b, 1, w)
    page_spec = lambda g, off: pl.BlockSpec(
        (1, 1, nh, HEAD_DIM, page),
        lambda bi, j, pt, sr, nd: (pt[bi, PAGES_PER_BLOCK * sr[bi, j * grp + g] + off], layer, 0, 0, 0))
    seq_spec = pl.BlockSpec((1, 1, w), lambda bi, j, pt, sr, nd: (bi, 0, 0))
    att = pl.pallas_call(
        _dec_pv_kernel, name="moba_dec_pv",
        grid_spec=pltpu.PrefetchScalarGridSpec(
            num_scalar_prefetch=3,
            grid=(b, nblk // grp),
            in_specs=[pl.BlockSpec((1, grp, nh, blk), lambda bi, j, pt, sr, nd: (bi, j, 0, 0))]
            + [page_spec(g, off) for g in range(grp) for off in range(PAGES_PER_BLOCK)]
            + [seq_spec, seq_spec],
            out_specs=seq_spec,
        ),
        out_shape=f((b, 1, w), F32),
        compiler_params=_cparams("parallel", "arbitrary"),
    )(page_table, src, need.reshape(b, nblk).astype(jnp.int32), p_blk,
      *([cache_vt] * (grp * PAGES_PER_BLOCK)), p_own_x, proj_s[:, 3 * w:].reshape(b, 1, w))
    return att.reshape(b, w).astype(BF16)


def _tile(m, pref):
    t = min(m, pref)
    while m % t:
        t //= 2
    return t


def _ff_tile(ff, pref):
    best = LANES
    for t in range(LANES, pref + 1, LANES):
        if ff % t == 0:
            best = t
    return best


def kernel(x_prompt, x_sample, cache_k, cache_v, page_table, state_s5_re, state_s5_im, state_pool, rel_bias,
           norm_mix_e, w_in_e, q_norm_e, k_norm_e, s5_a_re, s5_a_im, s5_log_dt, s5_b_re, s5_b_im, s5_c_re,
           s5_c_im, s5_d, s5_w_glu, s5_b_glu, w_out_e, norm_ffn_e, ffn_w_gate, ffn_w_up, ffn_w_down,
           norm_mix_o, pool_w, pool_scale, norm_ffn_o, router_w, router_b, moe_w_gate, moe_w_up, moe_w_down):
    bp, seq, d = x_prompt.shape
    db = x_sample.shape[0]
    assert x_sample.shape[1] == 1
    depth = norm_mix_e.shape[0] + norm_mix_o.shape[0]
    wdt = w_in_e.shape[2] // 4
    nh = wdt // HEAD_DIM
    past_len = page_table.shape[1] * cache_k.shape[2]
    n_exp = router_w.shape[2]
    gn = s5_a_re.shape[1] * s5_a_re.shape[2]

    mp = bp * seq
    xp = x_prompt.reshape(mp, d)
    xs = x_sample.reshape(db, d)
    tm_p = _tile(mp, 512)
    bias_tiles = _prompt_bias_tiles(rel_bias)
    row = lambda v: v.reshape(1, -1).astype(F32)
    cache_kt = jnp.transpose(cache_k, (0, 1, 3, 4, 2))
    cache_vt = jnp.transpose(cache_v, (0, 1, 3, 4, 2))

    kp_l, vp_l, ks_l, vs_l = [], [], [], []
    s5p_l, s5s_re_l, s5s_im_l = [], [], []
    poolp_l, pools_l = [], []
    for layer in range(depth):
        if layer % 2 == 0:
            e = layer // 2
            w_in = w_in_e[e].astype(BF16)
            ones = jnp.ones((wdt,), F32)
            head_gain = jnp.stack([ones, jnp.tile(q_norm_e[e].astype(F32), nh),
                                   jnp.tile(k_norm_e[e].astype(F32), nh), ones]).reshape(4, 1, wdt)
            wb, tabs, lam, wc = _s5_weights(s5_a_re[e], s5_a_im[e], s5_log_dt[e], s5_b_re[e], s5_b_im[e],
                                            s5_c_re[e], s5_c_im[e])
            dsk = row(s5_d[e])
            wgl = s5_w_glu[e].astype(BF16)
            bgl = row(s5_b_glu[e])
            w_out = w_out_e[e].astype(BF16)
            wg, wu, wd = ffn_w_gate[e].astype(BF16), ffn_w_up[e].astype(BF16), ffn_w_down[e].astype(BF16)
            tf = _ff_tile(wg.shape[1], 1536)

            proj = _in_proj(xp, row(norm_mix_e[e]), w_in, head_gain, tm_p)
            proj3 = proj.reshape(bp, seq, 4 * wdt)
            s5_out, hfin = _s5_scan(proj3, jnp.zeros((bp, 1, 2 * gn), F32), wb, tabs, wc, dsk, wgl, bgl,
                                    _tile(seq, 256))
            att = _moba_prompt(proj3, bias_tiles)
            xp = _mix_ffn(xp, s5_out.reshape(mp, wdt), att.reshape(mp, wdt), w_out, row(norm_ffn_e[e]),
                          wg, wu, wd, tm_p, tf)
            kp_l.append(proj3[:, :, 2 * wdt:3 * wdt].reshape(bp, seq, nh, HEAD_DIM))
            vp_l.append(proj3[:, :, 3 * wdt:].reshape(bp, seq, nh, HEAD_DIM))
            s5p_l.append(hfin.reshape(bp, 2, -1, S5_STATE))

            proj_s = _in_proj(xs, row(norm_mix_e[e]), w_in, head_gain, db)
            s5_out_s, hr_s, hi_s = _s5_step(proj_s, state_s5_re[:, e].reshape(db, gn).astype(F32),
                                            state_s5_im[:, e].reshape(db, gn).astype(F32),
                                            wb, lam, wc, dsk, wgl, bgl)
            att_s = _moba_decode(proj_s, cache_kt, cache_vt, page_table, rel_bias, e)
            xs = _mix_ffn(xs, s5_out_s, att_s, w_out, row(norm_ffn_e[e]), wg, wu, wd, db, tf)
            ks_l.append(proj_s[:, 2 * wdt:3 * wdt].reshape(db, 1, nh, HEAD_DIM))
            vs_l.append(proj_s[:, 3 * wdt:].reshape(db, 1, nh, HEAD_DIM))
            s5s_re_l.append(hr_s.reshape(db, -1, S5_STATE))
            s5s_im_l.append(hi_s.reshape(db, -1, S5_STATE))
        else:
            o = layer // 2
            wp = pool_w[o].astype(BF16)
            sc = row(pool_scale[o])
            wr = jnp.zeros((d, LANES), BF16).at[:, :n_exp].set(router_w[o].astype(BF16))
            br = jnp.zeros((1, LANES), F32).at[:, :n_exp].set(router_b[o].astype(F32))
            wg, wu, wd = moe_w_gate[o].astype(BF16), moe_w_up[o].astype(BF16), moe_w_down[o].astype(BF16)
            tf = _ff_tile(wg.shape[2], 1792)

            xp3, hist_p = _pool_prompt(xp.reshape(bp, seq, d), row(norm_mix_o[o]), wp, sc, _tile(seq, 512))
            xp = _moe_sparse(xp3.reshape(mp, d), row(norm_ffn_o[o]), wr, br, wg, wu, wd, tm_p, tf // 2)
            poolp_l.append(hist_p[:, HALO - POOL_HIST:])

            hist_s = state_pool[:, o].astype(F32)
            xs, hn_s = _pool_step(xs, row(norm_mix_o[o]), jnp.transpose(hist_s, (1, 0, 2)), wp, sc,
                                  min(POOL_HIST, past_len))
            xs = _moe_dense(xs, row(norm_ffn_o[o]), wr, br, wg, wu, wd, db, tf)
            pools_l.append(jnp.concatenate([hist_s[:, 1:], hn_s[:, None]], axis=1))

    s5p = jnp.stack(s5p_l, axis=1)
    return (xp.reshape(bp, seq, d), xs.reshape(db, 1, d),
            jnp.stack(kp_l, axis=2), jnp.stack(vp_l, axis=2),
            jnp.stack(ks_l, axis=2), jnp.stack(vs_l, axis=2),
            s5p[:, :, 0], s5p[:, :, 1],
            jnp.stack(s5s_re_l, axis=1), jnp.stack(s5s_im_l, axis=1),
            jnp.stack(poolp_l, axis=1), jnp.stack(pools_l, axis=1))
```

```python
import functools
import math

import jax
import jax.numpy as jnp
import numpy as np
from jax import lax
from jax.experimental import pallas as pl
from jax.experimental.pallas import tpu as pltpu

F32 = jnp.float32
BF16 = jnp.bfloat16
EPS = 1e-6
NEG = -1e30

S5_GROUP = 16
S5_STATE = 64
HEAD_DIM = 64
MOBA_BLOCK = 256
MOBA_TOPK = 3
REL_BUCKETS = 32
REL_MAX_DIST = 1024
POOL_WINDOWS = (2, 4, 8, 16)
POOL_HIST = max(POOL_WINDOWS) - 1
TOP_K_EXPERTS = 2

LANES = 128
SUBLANES = 8
VMEM_LIMIT = 48 * 1024 * 1024


def _cparams(*sem):
    return pltpu.CompilerParams(dimension_semantics=sem, vmem_limit_bytes=VMEM_LIMIT)


def _rmsnorm(x, g):
    ms = jnp.mean(x * x, axis=-1, keepdims=True)
    return x * lax.rsqrt(ms + EPS) * g


def _split_bf16(x):
    hi = x.astype(BF16)
    lo = (x - hi.astype(F32)).astype(BF16)
    return hi, lo


def _dot(a, b):
    return jnp.dot(a, b, preferred_element_type=F32)


def _dot_nt(a, b):
    return lax.dot_general(a, b, (((1,), (1,)), ((), ())), preferred_element_type=F32)


def _dot3(a, b):
    ah, al = _split_bf16(a)
    bh, bl = _split_bf16(b)
    return _dot(ah, bh) + _dot(ah, bl) + _dot(al, bh)


def _dot3_nt(a, b):
    ah, al = _split_bf16(a)
    bh, bl = _split_bf16(b)
    return _dot_nt(ah, bh) + _dot_nt(ah, bl) + _dot_nt(al, bh)


def _silu(x):
    return x * jax.nn.sigmoid(x)


def _in_proj_kernel(x_ref, g_ref, w_ref, hn_ref, bd_ref, k_in, v_in, uq_ref, k_ref, v_ref, h_scr):
    del k_in, v_in
    j = pl.program_id(1)

    @pl.when(j == 0)
    def _():
        h_scr[...] = _rmsnorm(x_ref[...], g_ref[...]).astype(BF16)

    y = _dot(h_scr[...], w_ref[...])

    def head_normed():
        hi, lo = _split_bf16(y * y)
        ms = _dot(hi, bd_ref[...]) + _dot(lo, bd_ref[...])
        return y * lax.rsqrt(ms + EPS) * hn_ref[0]

    @pl.when(j == 0)
    def _():
        uq_ref[...] = y

    @pl.when(j == 1)
    def _():
        uq_ref[...] = head_normed()

    @pl.when(j == 2)
    def _():
        k_ref[...] = head_normed()

    @pl.when(j == 3)
    def _():
        v_ref[...] = y


def _in_proj(x, g, w_bf, head_gain, k_all, v_all, layer, tm):
    m, d = x.shape
    wdt = w_bf.shape[1] // 4
    bd = np.kron(np.eye(wdt // HEAD_DIM), np.full((HEAD_DIM, HEAD_DIM), 1.0 / HEAD_DIM))
    bd = jnp.asarray(bd, BF16)
    kv_spec = pl.BlockSpec((tm, wdt), lambda i, j: (i, layer))
    return pl.pallas_call(
        _in_proj_kernel, name="in_proj",
        grid=(m // tm, 4),
        in_specs=[
            pl.BlockSpec((tm, d), lambda i, j: (i, 0)),
            pl.BlockSpec((1, d), lambda i, j: (0, 0)),
            pl.BlockSpec((d, wdt), lambda i, j: (0, j)),
            pl.BlockSpec((1, 1, wdt), lambda i, j: (j, 0, 0)),
            pl.BlockSpec((wdt, wdt), lambda i, j: (0, 0)),
            pl.BlockSpec(memory_space=pl.ANY),
            pl.BlockSpec(memory_space=pl.ANY),
        ],
        out_specs=[pl.BlockSpec((tm, wdt), lambda i, j: (i, jnp.minimum(j, 1))), kv_spec, kv_spec],
        out_shape=[jax.ShapeDtypeStruct((m, 2 * wdt), F32), jax.ShapeDtypeStruct(k_all.shape, F32),
                   jax.ShapeDtypeStruct(v_all.shape, F32)],
        input_output_aliases={5: 1, 6: 2},
        scratch_shapes=[pltpu.VMEM((tm, d), BF16)],
        compiler_params=_cparams("parallel", "arbitrary"),
    )(x, g, w_bf, head_gain, bd, k_all, v_all)


def _mix_ffn_kernel(x_ref, a_ref, b_ref, wo_ref, g_ref, wg_ref, wu_ref, wd_ref, o_ref,
                    x1_scr, h_scr, acc_scr):
    f = pl.program_id(1)
    half = a_ref.shape[1]

    @pl.when(f == 0)
    def _():
        x1 = (x_ref[...] + _dot(a_ref[...], wo_ref[:half, :]) + _dot(b_ref[...], wo_ref[half:, :]))
        x1_scr[...] = x1
        h_scr[...] = _rmsnorm(x1, g_ref[...]).astype(BF16)
        acc_scr[...] = jnp.zeros_like(acc_scr)

    h = h_scr[...]
    act = (_silu(_dot(h, wg_ref[...])) * _dot(h, wu_ref[...])).astype(BF16)
    acc_scr[...] += _dot(act, wd_ref[...])

    @pl.when(f == pl.num_programs(1) - 1)
    def _():
        o_ref[...] = x1_scr[...] + acc_scr[...]


def _mix_ffn(x, a, b, wo_bf, g, wg_bf, wu_bf, wd_bf, tm, tf):
    m, d = x.shape
    half = a.shape[1]
    ff = wg_bf.shape[1]
    return pl.pallas_call(
        _mix_ffn_kernel, name="mix_ffn",
        grid=(m // tm, ff // tf),
        in_specs=[
            pl.BlockSpec((tm, d), lambda i, f: (i, 0)),
            pl.BlockSpec((tm, half), lambda i, f: (i, 0)),
            pl.BlockSpec((tm, half), lambda i, f: (i, 0)),
            pl.BlockSpec((2 * half, d), lambda i, f: (0, 0)),
            pl.BlockSpec((1, d), lambda i, f: (0, 0)),
            pl.BlockSpec((d, tf), lambda i, f: (0, f)),
            pl.BlockSpec((d, tf), lambda i, f: (0, f)),
            pl.BlockSpec((tf, d), lambda i, f: (f, 0)),
        ],
        out_specs=pl.BlockSpec((tm, d), lambda i, f: (i, 0)),
        out_shape=jax.ShapeDtypeStruct((m, d), F32),
        scratch_shapes=[pltpu.VMEM((tm, d), F32), pltpu.VMEM((tm, d), BF16), pltpu.VMEM((tm, d), F32)],
        compiler_params=_cparams("parallel", "arbitrary"),
    )(x, a, b, wo_bf, g, wg_bf, wu_bf, wd_bf)


def _route(h, wr, br, n_exp):
    logits = _dot(h.astype(BF16), wr) + br
    lane = lax.broadcasted_iota(jnp.int32, logits.shape, 1)
    logits = jnp.where(lane < n_exp, logits, NEG)
    m1 = jnp.max(logits, axis=-1, keepdims=True)
    i1 = jnp.min(jnp.where(logits == m1, lane, LANES), axis=-1, keepdims=True)
    rest = jnp.where(lane == i1, NEG, logits)
    m2 = jnp.max(rest, axis=-1, keepdims=True)
    i2 = jnp.min(jnp.where(rest == m2, lane, LANES), axis=-1, keepdims=True)
    e2 = jnp.exp(m2 - m1)
    g1 = 1.0 / (1.0 + e2)
    g2 = e2 / (1.0 + e2)
    gate = jnp.where(lane == i1, g1, 0.0) + jnp.where(lane == i2, g2, 0.0)
    return gate, jnp.logical_or(lane == i1, lane == i2)


def _moe_dense_kernel(x_ref, g_ref, wr_ref, br_ref, wg_ref, wu_ref, wd_ref, o_ref,
                      h_scr, gate_scr, acc_scr, *, n_exp):
    e = pl.program_id(1)
    f = pl.program_id(2)

    @pl.when(jnp.logical_and(e == 0, f == 0))
    def _():
        h = _rmsnorm(x_ref[...], g_ref[...])
        h_scr[...] = h.astype(BF16)
        gate_scr[...] = _route(h, wr_ref[...], br_ref[...], n_exp)[0]
        acc_scr[...] = jnp.zeros_like(acc_scr)

    h = h_scr[...]
    lane = lax.broadcasted_iota(jnp.int32, gate_scr.shape, 1)
    w_e = jnp.sum(jnp.where(lane == e, gate_scr[...], 0.0), axis=-1, keepdims=True)
    act = (_silu(_dot(h, wg_ref[0])) * _dot(h, wu_ref[0])).astype(BF16)
    acc_scr[...] += w_e * _dot(act, wd_ref[0])

    @pl.when(jnp.logical_and(e == n_exp - 1, f == pl.num_programs(2) - 1))
    def _():
        o_ref[...] = x_ref[...] + acc_scr[...]


def _moe_dense(x, g, wr_pad, br_pad, wg_bf, wu_bf, wd_bf, tm, tf):
    m, d = x.shape
    n_exp, _, ff = wg_bf.shape
    return pl.pallas_call(
        functools.partial(_moe_dense_kernel, n_exp=n_exp), name="moe_dense",
        grid=(m // tm, n_exp, ff // tf),
        in_specs=[
            pl.BlockSpec((tm, d), lambda i, e, f: (i, 0)),
            pl.BlockSpec((1, d), lambda i, e, f: (0, 0)),
            pl.BlockSpec((d, LANES), lambda i, e, f: (0, 0)),
            pl.BlockSpec((1, LANES), lambda i, e, f: (0, 0)),
            pl.BlockSpec((1, d, tf), lambda i, e, f: (e, 0, f)),
            pl.BlockSpec((1, d, tf), lambda i, e, f: (e, 0, f)),
            pl.BlockSpec((1, tf, d), lambda i, e, f: (e, f, 0)),
        ],
        out_specs=pl.BlockSpec((tm, d), lambda i, e, f: (i, 0)),
        out_shape=jax.ShapeDtypeStruct((m, d), F32),
        scratch_shapes=[pltpu.VMEM((tm, d), BF16), pltpu.VMEM((tm, LANES), F32), pltpu.VMEM((tm, d), F32)],
        compiler_params=_cparams("parallel", "arbitrary", "arbitrary"),
    )(x, g, wr_pad, br_pad, wg_bf, wu_bf, wd_bf)


def _moe_route_kernel(x_ref, g_ref, wr_ref, br_ref, h_ref, gate_ref, pos_ref, post_ref, cb_ref, tot_ref, carry,
                      *, n_exp):
    i = pl.program_id(0)
    tm = x_ref.shape[0]

    @pl.when(i == 0)
    def _():
        carry[...] = jnp.zeros_like(carry)

    h = _rmsnorm(x_ref[...], g_ref[...])
    h_ref[...] = h.astype(BF16)
    gate, chosen = _route(h, wr_ref[...], br_ref[...], n_exp)
    gate_ref[...] = gate
    onehot = jnp.where(chosen, 1.0, 0.0)
    r = lax.broadcasted_iota(jnp.int32, (tm, tm), 0)
    c = lax.broadcasted_iota(jnp.int32, (tm, tm), 1)
    before = _dot(jnp.where(c < r, 1.0, 0.0).astype(BF16), onehot.astype(BF16))
    start = carry[...]
    pos = jnp.where(chosen, before + start, -1.0)
    pos_ref[...] = pos
    post_ref[...] = pos.T[:post_ref.shape[0], :]
    cb_ref[0] = start
    total = start + jnp.sum(onehot, axis=0, keepdims=True)
    carry[...] = total
    tot_ref[...] = total


def _moe_route(x, g, wr_bf, br, n_exp, tm):
    m, d = x.shape
    nt = m // tm
    ne8 = -(-n_exp // SUBLANES) * SUBLANES
    f = jax.ShapeDtypeStruct
    return pl.pallas_call(
        functools.partial(_moe_route_kernel, n_exp=n_exp), name="moe_route",
        grid=(nt,),
        in_specs=[
            pl.BlockSpec((tm, d), lambda i: (i, 0)),
            pl.BlockSpec((1, d), lambda i: (0, 0)),
            pl.BlockSpec((d, LANES), lambda i: (0, 0)),
            pl.BlockSpec((1, LANES), lambda i: (0, 0)),
        ],
        out_specs=[
            pl.BlockSpec((tm, d), lambda i: (i, 0)),
            pl.BlockSpec((tm, LANES), lambda i: (i, 0)),
            pl.BlockSpec((tm, LANES), lambda i: (i, 0)),
            pl.BlockSpec((ne8, tm), lambda i: (0, i)),
            pl.BlockSpec((1, 1, LANES), lambda i: (i, 0, 0)),
            pl.BlockSpec((1, LANES), lambda i: (0, 0)),
        ],
        out_shape=[f((m, d), BF16), f((m, LANES), F32), f((m, LANES), F32), f((ne8, m), F32),
                   f((nt, 1, LANES), F32), f((1, LANES), F32)],
        scratch_shapes=[pltpu.VMEM((1, LANES), F32)],
        compiler_params=_cparams("arbitrary"),
    )(x, g, wr_bf, br)


def _moe_gather_kernel(pc_ref, ps_ref, pe_ref, pfirst_ref, pvalid_ref, off_ref, h_ref, post_ref, o_ref, acc):
    k = pl.program_id(0)
    rc, tm = o_ref.shape[0], h_ref.shape[0]

    @pl.when(pfirst_ref[k] == 1)
    def _():
        acc[...] = jnp.zeros_like(acc)

    @pl.when(pvalid_ref[k] == 1)
    def _():
        e = pe_ref[k]
        dest = post_ref[pl.ds(e, 1), :] + off_ref[e].astype(F32)
        rows = (pc_ref[k] * rc + lax.broadcasted_iota(jnp.int32, (rc, tm), 0)).astype(F32)
        acc[...] += _dot(jnp.where(dest == rows, 1.0, 0.0).astype(BF16), h_ref[...])

    o_ref[...] = acc[...].astype(BF16)


def _moe_ffn_kernel(ce_ref, cv_ref, x_ref, wg_ref, wu_ref, wd_ref, y_ref, *, tf):
    c = pl.program_id(0)

    @pl.when(cv_ref[c] == 1)
    def _():
        x = x_ref[...]
        acc = jnp.zeros(y_ref.shape, F32)
        for f0 in range(0, wg_ref.shape[2], tf):
            act = (_silu(_dot(x, wg_ref[0, :, f0:f0 + tf])) * _dot(x, wu_ref[0, :, f0:f0 + tf])).astype(BF16)
            acc = acc + _dot(act, wd_ref[0, f0:f0 + tf, :])
        y_ref[...] = acc.astype(BF16)

    @pl.when(cv_ref[c] == 0)
    def _():
        y_ref[...] = jnp.zeros_like(y_ref)


def _moe_combine_kernel(qs_ref, qc_ref, qe_ref, qfirst_ref, qvalid_ref, off_ref, x_ref, y_ref, pos_ref, gate_ref,
                        o_ref):
    k = pl.program_id(0)
    tm, rc = x_ref.shape[0], y_ref.shape[0]

    @pl.when(qfirst_ref[k] == 1)
    def _():
        o_ref[...] = x_ref[...]

    @pl.when(qvalid_ref[k] == 1)
    def _():
        e = qe_ref[k]
        lane = lax.broadcasted_iota(jnp.int32, pos_ref.shape, 1)
        pos_e = jnp.sum(jnp.where(lane == e, pos_ref[...], 0.0), axis=-1, keepdims=True)
        gate_e = jnp.sum(jnp.where(lane == e, gate_ref[...], 0.0), axis=-1, keepdims=True)
        dest = pos_e + off_ref[e].astype(F32)
        cols = (qc_ref[k] * rc + lax.broadcasted_iota(jnp.int32, (tm, rc), 1)).astype(F32)
        pick = jnp.where(dest == cols, 1.0, 0.0).astype(BF16)
        o_ref[...] += gate_e * _dot(pick, y_ref[...])


def _pair_list(inter, n_pairs_max):
    nb = inter.shape[1]
    flat = inter.reshape(-1)
    n = jnp.sum(flat.astype(jnp.int32))
    idx = jnp.nonzero(flat, size=n_pairs_max, fill_value=0)[0].astype(jnp.int32)
    k = jnp.arange(n_pairs_max, dtype=jnp.int32)
    idx = jnp.where(k < n, idx, idx[jnp.maximum(n - 1, 0)])
    a, b = idx // nb, idx % nb
    valid = k < n
    first = jnp.logical_and(valid, jnp.logical_or(k == 0, a != jnp.roll(a, 1)))
    return a, b, valid.astype(jnp.int32), first.astype(jnp.int32)


ROW_CHUNK = 256


def _moe_sparse(x, g, wr_bf, br, wg_bf, wu_bf, wd_bf, tm, tf):
    m, d = x.shape
    n_exp, _, ff = wg_bf.shape
    nt = m // tm
    rc = ROW_CHUNK
    h, gate, pos, pos_t, cb, tot = _moe_route(x, g, wr_bf, br, n_exp, tm)

    cnt = tot[0, :n_exp].astype(jnp.int32)
    cbi = jnp.concatenate([cb[:, 0, :n_exp], tot[:, :n_exp]], axis=0).astype(jnp.int32)
    gsz = (cnt + rc - 1) // rc * rc
    ends = jnp.cumsum(gsz)
    off = (ends - gsz).astype(jnp.int32)
    n_chunks = TOP_K_EXPERTS * m // rc + n_exp
    cstart = jnp.arange(n_chunks, dtype=jnp.int32) * rc
    ce = jnp.minimum(jnp.searchsorted(ends, cstart, side='right'), n_exp - 1).astype(jnp.int32)
    cv = cstart < ends[-1]
    per_tile = cbi.T[ce]
    start = off[ce][:, None] + per_tile[:, :-1]
    end = off[ce][:, None] + per_tile[:, 1:]
    inter = (cv[:, None] & (end > start) & (start < cstart[:, None] + rc) & (end > cstart[:, None]))
    n_pairs = n_chunks + n_exp * nt
    unused = jnp.logical_and(jnp.logical_not(cv)[:, None], jnp.arange(nt)[None, :] == 0)
    pc, ps, plisted, pfirst = _pair_list(jnp.logical_or(inter, unused), n_pairs + n_exp)
    pvalid = plisted * inter[pc, ps].astype(jnp.int32)
    qs, qc, qvalid, qfirst = _pair_list(inter.T, n_pairs)

    xs = pl.pallas_call(
        _moe_gather_kernel, name="moe_gather",
        grid_spec=pltpu.PrefetchScalarGridSpec(
            num_scalar_prefetch=6,
            grid=(n_pairs + n_exp,),
            in_specs=[
                pl.BlockSpec((tm, d), lambda k, pc, ps, *_: (ps[k], 0)),
                pl.BlockSpec((pos_t.shape[0], tm), lambda k, pc, ps, *_: (0, ps[k])),
            ],
            out_specs=pl.BlockSpec((rc, d), lambda k, pc, *_: (pc[k], 0)),
            scratch_shapes=[pltpu.VMEM((rc, d), F32)],
        ),
        out_shape=jax.ShapeDtypeStruct((n_chunks * rc, d), BF16),
        compiler_params=_cparams("arbitrary"),
    )(pc, ps, ce[pc], pfirst, pvalid, off, h, pos_t)

    cvi = cv.astype(jnp.int32)
    wspec = lambda shape: pl.BlockSpec(shape, lambda c, ce, cv: (ce[c], 0, 0), pipeline_mode=pl.Buffered(1))
    ys = pl.pallas_call(
        functools.partial(_moe_ffn_kernel, tf=tf), name="moe_ffn",
        grid_spec=pltpu.PrefetchScalarGridSpec(
            num_scalar_prefetch=2,
            grid=(n_chunks,),
            in_specs=[
                pl.BlockSpec((rc, d), lambda c, ce, cv: (c * cv[c], 0)),
                wspec((1, d, ff)), wspec((1, d, ff)), wspec((1, ff, d)),
            ],
            out_specs=pl.BlockSpec((rc, d), lambda c, ce, cv: (c, 0)),
        ),
        out_shape=jax.ShapeDtypeStruct((n_chunks * rc, d), BF16),
        compiler_params=_cparams("arbitrary"),
    )(ce, cvi, xs, wg_bf, wu_bf, wd_bf)

    return pl.pallas_call(
        _moe_combine_kernel, name="moe_combine",
        grid_spec=pltpu.PrefetchScalarGridSpec(
            num_scalar_prefetch=6,
            grid=(n_pairs,),
            in_specs=[
                pl.BlockSpec((tm, d), lambda k, qs, qc, *_: (qs[k], 0)),
                pl.BlockSpec((rc, d), lambda k, qs, qc, *_: (qc[k], 0)),
                pl.BlockSpec((tm, LANES), lambda k, qs, *_: (qs[k], 0)),
                pl.BlockSpec((tm, LANES), lambda k, qs, *_: (qs[k], 0)),
            ],
            out_specs=pl.BlockSpec((tm, d), lambda k, qs, *_: (qs[k], 0)),
        ),
        out_shape=jax.ShapeDtypeStruct((m, d), F32),
        compiler_params=_cparams("arbitrary"),
    )(qs, qc, ce[qc], qfirst, qvalid, off, x, ys, pos, gate)


HALO = 16


def _pool_kernel(x_ref, g_ref, wp_ref, sc_ref, o_ref, hist_ref, hbuf):
    t = pl.program_id(1)
    tm = x_ref.shape[1]
    gw = wp_ref.shape[1]

    @pl.when(t == 0)
    def _():
        hbuf[0:HALO, :] = jnp.zeros((HALO, hbuf.shape[1]), F32)

    @pl.when(t > 0)
    def _():
        hbuf[0:HALO, :] = hbuf[tm:tm + HALO, :]

    x = x_ref[0]
    hn = _rmsnorm(x, g_ref[...])
    hbuf[HALO:HALO + tm, :] = hn
    pos = t * tm + lax.broadcasted_iota(jnp.int32, (tm, 1), 0)
    ys = []
    for gi, w in enumerate(POOL_WINDOWS):
        sl = slice(gi * gw, (gi + 1) * gw)
        tot = hn[:, sl]
        for i in range(1, w):
            tot = tot + hbuf[HALO - i:HALO - i + tm, sl]
        cnt = jnp.minimum(pos + 1, w).astype(F32)
        pooled = tot / cnt - hn[:, sl]
        ys.append(_dot(pooled.astype(BF16), wp_ref[gi]))
    o_ref[0] = x + jnp.concatenate(ys, axis=1) * sc_ref[...]

    @pl.when(t == pl.num_programs(1) - 1)
    def _():
        hist_ref[0] = hbuf[tm:tm + HALO, :]


def _pool_prompt(x, g, wp_bf, sc, tm):
    b, s, d = x.shape
    ng, gw, _ = wp_bf.shape
    return pl.pallas_call(
        _pool_kernel, name="pool_prompt",
        grid=(b, s // tm),
        in_specs=[
            pl.BlockSpec((1, tm, d), lambda i, t: (i, t, 0)),
            pl.BlockSpec((1, d), lambda i, t: (0, 0)),
            pl.BlockSpec((ng, gw, gw), lambda i, t: (0, 0, 0)),
            pl.BlockSpec((1, d), lambda i, t: (0, 0)),
        ],
        out_specs=[
            pl.BlockSpec((1, tm, d), lambda i, t: (i, t, 0)),
            pl.BlockSpec((1, HALO, d), lambda i, t: (i, 0, 0)),
        ],
        out_shape=[jax.ShapeDtypeStruct((b, s, d), F32), jax.ShapeDtypeStruct((b, HALO, d), F32)],
        scratch_shapes=[pltpu.VMEM((HALO + tm, d), F32)],
        compiler_params=_cparams("parallel", "arbitrary"),
    )(x, g, wp_bf, sc)


def _pool_step_kernel(x_ref, g_ref, hist_ref, wp_ref, sc_ref, o_ref, hn_ref, *, n_valid):
    gw = wp_ref.shape[1]
    x = x_ref[...]
    hn = _rmsnorm(x, g_ref[...])
    hn_ref[...] = hn
    ys = []
    for gi, w in enumerate(POOL_WINDOWS):
        sl = slice(gi * gw, (gi + 1) * gw)
        tot = hn[:, sl]
        for i in range(1, min(w, n_valid + 1)):
            tot = tot + hist_ref[POOL_HIST - i][:, sl]
        pooled = tot / float(min(w, n_valid + 1)) - hn[:, sl]
        ys.append(_dot(pooled.astype(BF16), wp_ref[gi]))
    o_ref[...] = x + jnp.concatenate(ys, axis=1) * sc_ref[...]


def _pool_step(x, g, hist_t, wp_bf, sc, n_valid):
    b, d = x.shape
    return pl.pallas_call(
        functools.partial(_pool_step_kernel, n_valid=n_valid), name="pool_step",
        out_shape=[jax.ShapeDtypeStruct((b, d), F32), jax.ShapeDtypeStruct((b, d), F32)],
        compiler_params=pltpu.CompilerParams(vmem_limit_bytes=VMEM_LIMIT),
    )(x, g, hist_t, wp_bf, sc)


N_POW = SUBLANES


def _s5_prep_kernel(ar_ref, ai_ref, ldt_ref, br_ref, bi_ref, pwr_ref, pwi_ref, bbr_ref, bbi_ref):
    ar = ar_ref[...]
    ai = ai_ref[...]
    dt = jnp.exp(ldt_ref[...])
    mag = jnp.exp(ar * dt)
    lr = mag * jnp.cos(ai * dt)
    li = mag * jnp.sin(ai * dt)
    den = ar * ar + ai * ai
    rr = ((lr - 1.0) * ar + li * ai) / den
    ri = (li * ar - (lr - 1.0) * ai) / den
    br = br_ref[...]
    bi = bi_ref[...]
    bbr_ref[...] = rr * br - ri * bi
    bbi_ref[...] = rr * bi + ri * br
    cr, ci = lr, li
    for k in range(N_POW):
        pwr_ref[k] = cr
        pwi_ref[k] = ci
        cr, ci = cr * lr - ci * li, cr * li + ci * lr


def _s5_prep(a_re, a_im, log_dt, b_re, b_im):
    g, n = a_re.shape
    p = b_re.shape[2]
    f = jax.ShapeDtypeStruct
    return pl.pallas_call(
        _s5_prep_kernel, name="s5_prep",
        out_shape=[f((N_POW, g, 1, n), F32), f((N_POW, g, 1, n), F32), f((g, p, n), F32), f((g, p, n), F32)],
    )(a_re[:, None, :], a_im[:, None, :], log_dt[:, None, None],
      jnp.transpose(b_re, (0, 2, 1)), jnp.transpose(b_im, (0, 2, 1)))


def _s5_weights(a_re, a_im, log_dt, b_re, b_im, c_re, c_im):
    g, n = a_re.shape
    p = b_re.shape[2]
    gn = g * n
    pwr, pwi, bbr, bbi = _s5_prep(a_re, a_im, log_dt, b_re, b_im)
    pwr = pwr.reshape(N_POW, gn)
    pwi = pwi.reshape(N_POW, gn)
    row = np.arange(SUBLANES)[:, None]
    tabs = []
    for k in (1, 2, 4):
        tabs += [jnp.where(row >= k, pwr[k - 1][None], 0.0), jnp.where(row >= k, pwi[k - 1][None], 0.0)]
    tabs += [pwr, pwi]
    tabs = jnp.stack(tabs)
    lam = jnp.stack([pwr[0], pwi[0]])
    n_tiles = gn // LANES
    g_tile = LANES // n
    g_chunk = LANES // p
    place = np.zeros((n_tiles, g_chunk, g_tile), np.float32)
    for j in range(n_tiles):
        for i in range(g_tile):
            place[j, (j * g_tile + i) % g_chunk, i] = 1.0
    bb = jnp.stack([bbr, bbi]).reshape(2, n_tiles, g_tile, p, n)
    wb = jnp.einsum('cjipn,jli->jlpcin', bb, place).reshape(n_tiles, LANES, 2 * LANES).astype(BF16)
    n_out = g * p // LANES
    cc = jnp.stack([c_re, -c_im]).reshape(2, n_out, g_chunk, p, n)
    wc = jnp.einsum('cmlpn,lk->mclnkp', cc, np.eye(g_chunk, dtype=np.float32))
    wc = wc.reshape(n_out, 2 * g_chunk * n, LANES).astype(BF16)
    return wb, tabs, lam, wc


def _s5_drive(u, wb_ref, bur, bui, row0):
    t = u.shape[0]
    ub = u.astype(BF16)
    n_tiles = wb_ref.shape[0]
    u_per_tile = u.shape[1] // n_tiles
    for j in range(n_tiles):
        c0 = (j * u_per_tile) // LANES * LANES
        r = _dot(ub[:, c0:c0 + LANES], wb_ref[j])
        bur[row0:row0 + t, j * LANES:(j + 1) * LANES] = r[:, :LANES]
        bui[row0:row0 + t, j * LANES:(j + 1) * LANES] = r[:, LANES:]


def _s5_readout(u, hr, hi, wc_ref, d_ref, wgl_ref, bgl_ref):
    hrb = hr.astype(BF16)
    hib = hi.astype(BF16)
    kk = wc_ref.shape[1] // 2
    ys = []
    for m in range(wc_ref.shape[0]):
        ys.append(_dot(hrb[:, m * kk:(m + 1) * kk], wc_ref[m, :kk, :])
                  + _dot(hib[:, m * kk:(m + 1) * kk], wc_ref[m, kk:, :]))
    y = jnp.concatenate(ys, axis=1) + d_ref[...] * u
    z = jax.nn.gelu(y, approximate=True)
    return z * jax.nn.sigmoid(_dot(z.astype(BF16), wgl_ref[...]) + bgl_ref[...])


def _s5_scan_kernel(u_ref, h0_ref, wb_ref, tab_ref, wc_ref, d_ref, wgl_ref, bgl_ref, o_ref, hf_ref, bur, bui):
    t = pl.program_id(1)
    tt = u_ref.shape[1]
    gn = bur.shape[1]
    c0 = SUBLANES

    @pl.when(t == 0)
    def _():
        bur[0:c0, :] = jnp.broadcast_to(h0_ref[0, :, :gn], (c0, gn))
        bui[0:c0, :] = jnp.broadcast_to(h0_ref[0, :, gn:], (c0, gn))

    @pl.when(t > 0)
    def _():
        bur[0:c0, :] = bur[tt:tt + c0, :]
        bui[0:c0, :] = bui[tt:tt + c0, :]

    u = u_ref[0]
    _s5_drive(u, wb_ref, bur, bui, c0)

    def block(b, carry):
        base = pl.multiple_of(b * SUBLANES, SUBLANES)
        cr = bur[pl.ds(base + c0 - 1, 1), :]
        ci = bui[pl.ds(base + c0 - 1, 1), :]
        xr = bur[pl.ds(base + c0, SUBLANES), :]
        xi = bui[pl.ds(base + c0, SUBLANES), :]
        for s, k in enumerate((1, 2, 4)):
            ar = tab_ref[2 * s]
            ai = tab_ref[2 * s + 1]
            sr = pltpu.roll(xr, k, axis=0)
            si = pltpu.roll(xi, k, axis=0)
            xr, xi = xr + ar * sr - ai * si, xi + ar * si + ai * sr
        pr = tab_ref[6]
        pi = tab_ref[7]
        bur[pl.ds(base + c0, SUBLANES), :] = xr + pr * cr - pi * ci
        bui[pl.ds(base + c0, SUBLANES), :] = xi + pr * ci + pi * cr
        return carry

    lax.fori_loop(0, tt // SUBLANES, block, 0)
    o_ref[0] = _s5_readout(u, bur[c0:c0 + tt, :], bui[c0:c0 + tt, :], wc_ref, d_ref, wgl_ref, bgl_ref).astype(BF16)

    @pl.when(t == pl.num_programs(1) - 1)
    def _():
        hf_ref[0] = jnp.concatenate([bur[tt + c0 - 1:tt + c0, :], bui[tt + c0 - 1:tt + c0, :]], axis=1)


def _s5_scan(proj, h0, wb, tabs, wc, dsk, wgl_bf, bgl, tt):
    b, s, _ = proj.shape
    w = dsk.shape[1]
    gn = tabs.shape[2]
    full = lambda a: pl.BlockSpec(a.shape, lambda i, t: (0,) * a.ndim)
    return pl.pallas_call(
        _s5_scan_kernel, name="s5_scan",
        grid=(b, s // tt),
        in_specs=[
            pl.BlockSpec((1, tt, w), lambda i, t: (i, t, 0)),
            pl.BlockSpec((1, 1, 2 * gn), lambda i, t: (i, 0, 0)),
            full(wb), full(tabs), full(wc), full(dsk), full(wgl_bf), full(bgl),
        ],
        out_specs=[
            pl.BlockSpec((1, tt, w), lambda i, t: (i, t, 0)),
            pl.BlockSpec((1, 1, 2 * gn), lambda i, t: (i, 0, 0)),
        ],
        out_shape=[jax.ShapeDtypeStruct((b, s, w), BF16), jax.ShapeDtypeStruct((b, 1, 2 * gn), F32)],
        scratch_shapes=[pltpu.VMEM((tt + SUBLANES, gn), F32), pltpu.VMEM((tt + SUBLANES, gn), F32)],
        compiler_params=_cparams("parallel", "arbitrary"),
    )(proj, h0, wb, tabs, wc, dsk, wgl_bf, bgl)


def _s5_step_kernel(u_ref, h0r_ref, h0i_ref, wb_ref, lam_ref, wc_ref, d_ref, wgl_ref, bgl_ref,
                    o_ref, hr_ref, hi_ref, bur, bui):
    u = u_ref[...]
    _s5_drive(u, wb_ref, bur, bui, 0)
    lr = lam_ref[0:1, :]
    li = lam_ref[1:2, :]
    h0r = h0r_ref[...]
    h0i = h0i_ref[...]
    hr = bur[...] + lr * h0r - li * h0i
    hi = bui[...] + lr * h0i + li * h0r
    hr_ref[...] = hr
    hi_ref[...] = hi
    o_ref[...] = _s5_readout(u, hr, hi, wc_ref, d_ref, wgl_ref, bgl_ref).astype(BF16)


def _s5_step(proj, h0r, h0i, wb, lam, wc, dsk, wgl_bf, bgl):
    b = proj.shape[0]
    w = dsk.shape[1]
    gn = lam.shape[1]
    full = lambda a: pl.BlockSpec(a.shape, lambda i: (0,) * a.ndim)
    return pl.pallas_call(
        _s5_step_kernel, name="s5_step",
        grid=(1,),
        in_specs=[pl.BlockSpec((b, w), lambda i: (0, 0)), full(h0r), full(h0i), full(wb), full(lam), full(wc),
                  full(dsk), full(wgl_bf), full(bgl)],
        out_specs=[pl.BlockSpec((b, w), lambda i: (0, 0)), pl.BlockSpec((b, gn), lambda i: (0, 0)),
                   pl.BlockSpec((b, gn), lambda i: (0, 0))],
        out_shape=[jax.ShapeDtypeStruct((b, w), BF16), jax.ShapeDtypeStruct((b, gn), F32),
                   jax.ShapeDtypeStruct((b, gn), F32)],
        scratch_shapes=[pltpu.VMEM((b, gn), F32), pltpu.VMEM((b, gn), F32)],
        compiler_params=_cparams("arbitrary"),
    )(proj, h0r, h0i, wb, lam, wc, dsk, wgl_bf, bgl)


def _t5_bucket(rel):
    n = jnp.maximum(rel, 0)
    max_exact = REL_BUCKETS // 2
    nf = jnp.maximum(n, 1).astype(F32)
    large = max_exact + (jnp.log(nf / max_exact) / math.log(REL_MAX_DIST / max_exact)
                         * (REL_BUCKETS - max_exact)).astype(jnp.int32)
    large = jnp.minimum(large, REL_BUCKETS - 1)
    return jnp.where(n < max_exact, n, large)


N_DIST = -(-(REL_MAX_DIST + MOBA_BLOCK - 1) // MOBA_BLOCK) + 1


def _prompt_bias_tiles(rel_bias):
    blk = MOBA_BLOCK
    span = 2 * blk - 1
    rel = jnp.asarray(np.arange(-(blk - 1), N_DIST * blk), jnp.int32)
    line = jnp.where(rel >= 0, rel_bias.astype(F32).T[:, _t5_bucket(rel)], NEG)
    diag = jnp.stack([line[:, d * blk:d * blk + span] for d in range(N_DIST)], axis=1)
    line_rc = jnp.roll(diag[..., ::-1], -(blk - 1), axis=-1)
    rep = jnp.tile(line_rc, (1, 1, blk))[..., :blk * (span - 1)]
    return rep.reshape(diag.shape[0], N_DIST, blk, span - 1)[..., :blk]


def _rank_select(g, n_valid, idx, n_cand, axis):
    cnt = jnp.zeros(g.shape, jnp.int32)
    for jp in range(n_cand):
        gp = lax.slice_in_dim(g, jp, jp + 1, axis=axis)
        beats = jnp.logical_or(gp > g, jnp.logical_and(gp == g, jp < idx))
        cnt = cnt + jnp.where(jnp.logical_and(beats, jp < n_valid), 1, 0)
    return jnp.logical_and(idx < n_valid, cnt < MOBA_TOPK)


SWEEP_UNROLL = 4


def _moba_kernel(q_ref, k_ref, v_ref, bias_ref, o_ref, kft, vb, km, qa, sbuf, red, stat, acc_scr):
    i = pl.program_id(2)
    blk = MOBA_BLOCK
    nb = k_ref.shape[1] // blk
    n_hd = LANES // HEAD_DIM
    scale = HEAD_DIM ** -0.5

    @pl.when(i == 0)
    def _():
        feat = lax.broadcasted_iota(jnp.int32, (LANES, blk), 0)
        for j in range(nb):
            kj = k_ref[0, j * blk:(j + 1) * blk, :]
            kft[j, :LANES, :] = kj.T.astype(BF16)
            kft[j, LANES:, :] = jnp.where(feat == j, 1.0, 0.0).astype(BF16)
            km[j:j + 1, :] = jnp.mean(kj, axis=0, keepdims=True)
        vb[...] = v_ref[0].astype(BF16)

    lane = lax.broadcasted_iota(jnp.int32, (blk, LANES), 1)
    wide = lambda a: jnp.concatenate([a, a], axis=1)
    rep = lambda a: jnp.broadcast_to(a, (blk, LANES))
    for hd in range(n_hd):
        qm = (jnp.where(lane // HEAD_DIM == hd, q_ref[0], 0.0) * scale).astype(BF16)
        g = _dot_nt(km[...].astype(BF16), qm)
        jrow = lax.broadcasted_iota(jnp.int32, g.shape, 0)
        keep = jnp.logical_or(_rank_select(g, i, jrow, nb, axis=0), jrow == i)
        pen = jnp.concatenate([jnp.where(keep, 0.0, NEG), jnp.zeros((LANES - nb, blk), F32)], axis=0)
        qa[hd, :, :LANES] = qm
        qa[hd, :, LANES:] = pen.T.astype(BF16)
        red[hd] = jnp.full((blk, LANES), NEG, F32)

    def sweep(step):
        def group(t, carry):
            for u in range(SWEEP_UNROLL):
                step(SWEEP_UNROLL * t + u)
            return carry

        def single(j, carry):
            step(j)
            return carry

        n_grouped = (i + 1) // SWEEP_UNROLL
        lax.fori_loop(0, n_grouped, group, 0)
        lax.fori_loop(n_grouped * SWEEP_UNROLL, i + 1, single, 0)

    def scores(j):
        dist = jnp.minimum(i - j, N_DIST - 1)
        kt = kft[j]
        for hd in range(n_hd):
            s = _dot(qa[hd], kt) + bias_ref[hd, dist]
            sbuf[hd, j] = s
            red[hd] = jnp.maximum(red[hd], jnp.maximum(s[:, :LANES], s[:, LANES:]))

    sweep(scores)
    for hd in range(n_hd):
        stat[hd] = rep(jnp.max(red[hd], axis=-1, keepdims=True))
        red[hd] = jnp.zeros((blk, LANES), F32)

    def exps(j):
        for hd in range(n_hd):
            e = jnp.exp(sbuf[hd, j] - wide(stat[hd]))
            sbuf[hd, j] = e
            red[hd] += e[:, :LANES] + e[:, LANES:]

    sweep(exps)
    for hd in range(n_hd):
        stat[hd] = rep(1.0 / jnp.sum(red[hd], axis=-1, keepdims=True))
        acc_scr[hd] = jnp.zeros((blk, LANES), F32)

    def weighted(j):
        vj = vb[pl.ds(pl.multiple_of(j * blk, blk), blk), :]
        for hd in range(n_hd):
            p = (sbuf[hd, j] * wide(stat[hd])).astype(BF16)
            acc_scr[hd] += _dot(p, vj)

    sweep(weighted)
    o = acc_scr[0]
    for hd in range(1, n_hd):
        o = jnp.where(lane // HEAD_DIM == hd, acc_scr[hd], o)
    o_ref[0] = o.astype(BF16)


def _moba_prompt(uq, k_all, v_all, layer, bias_tiles):
    b, s, n = uq.shape
    w = n // 2
    hp = w // LANES
    heads_per = LANES // HEAD_DIM
    blk = MOBA_BLOCK
    assert s % blk == 0 and s // blk <= LANES
    return pl.pallas_call(
        _moba_kernel, name="moba_prompt",
        grid=(b, hp, s // blk),
        in_specs=[
            pl.BlockSpec((1, blk, LANES), lambda bi, h, i: (bi, i, hp + h)),
            pl.BlockSpec((1, s, LANES), lambda bi, h, i: (bi, 0, layer * hp + h)),
            pl.BlockSpec((1, s, LANES), lambda bi, h, i: (bi, 0, layer * hp + h)),
            pl.BlockSpec((heads_per, N_DIST, blk, blk), lambda bi, h, i: (h, 0, 0, 0)),
        ],
        out_specs=pl.BlockSpec((1, blk, LANES), lambda bi, h, i: (bi, i, h)),
        out_shape=jax.ShapeDtypeStruct((b, s, w), BF16),
        scratch_shapes=[
            pltpu.VMEM((s // blk, 2 * LANES, blk), BF16), pltpu.VMEM((s, LANES), BF16),
            pltpu.VMEM((s // blk, LANES), F32), pltpu.VMEM((heads_per, blk, 2 * LANES), BF16),
            pltpu.VMEM((heads_per, s // blk, blk, blk), F32), pltpu.VMEM((heads_per, blk, LANES), F32),
            pltpu.VMEM((heads_per, blk, LANES), F32), pltpu.VMEM((heads_per, blk, LANES), F32),
        ],
        compiler_params=_cparams("parallel", "parallel", "arbitrary"),
    )(uq, k_all, v_all, bias_tiles)


def _rnd(a):
    return a.astype(BF16).astype(F32)


DEC_GROUP = 8
PAGES_PER_BLOCK = 2


def _dec_scores_kernel(pt_ref, q_ref, *refs):
    del pt_ref
    k_refs, (bias_ref, s_ref, ks_ref) = refs[:-3], refs[-3:]
    nh = s_ref.shape[2]
    hd = nh * HEAD_DIM
    page = k_refs[0].shape[-1]
    q = (q_ref[0] * HEAD_DIM ** -0.5).astype(BF16)
    ones = jnp.ones((SUBLANES, page), BF16)
    for g in range(s_ref.shape[1]):
        kk = [r[0, 0].reshape(hd, page) for r in k_refs[PAGES_PER_BLOCK * g:PAGES_PER_BLOCK * (g + 1)]]
        s_ref[0, g] = jnp.concatenate([_dot(q, k.astype(BF16)) for k in kk], axis=1) + bias_ref[g]
        hi, lo = _split_bf16(kk[0] + kk[1])
        ks_ref[0, g] = (_dot_nt(ones, hi) + _dot_nt(ones, lo))[0:1]


def _dec_scores(qx, cache_kt, page_table, bias_dec, layer):
    b, nh, hd = qx.shape
    page = cache_kt.shape[4]
    assert MOBA_BLOCK == PAGES_PER_BLOCK * page
    nblk = page_table.shape[1] // PAGES_PER_BLOCK
    grp = math.gcd(DEC_GROUP, nblk)
    n_pg = PAGES_PER_BLOCK * grp
    pg = lambda t: pl.BlockSpec((1, 1, nh, HEAD_DIM, page),
                                lambda bi, j, pt: (pt[bi, n_pg * j + t], layer, 0, 0, 0))
    out = lambda rows, last: pl.BlockSpec((1, grp, rows, last), lambda bi, j, pt: (bi, j, 0, 0))
    return pl.pallas_call(
        _dec_scores_kernel, name="moba_dec_scores",
        grid_spec=pltpu.PrefetchScalarGridSpec(
            num_scalar_prefetch=1,
            grid=(b, nblk // grp),
            in_specs=[pl.BlockSpec((1, nh, hd), lambda bi, j, pt: (bi, 0, 0))]
            + [pg(t) for t in range(n_pg)]
            + [pl.BlockSpec((grp, nh, MOBA_BLOCK), lambda bi, j, pt: (j, 0, 0))],
            out_specs=[out(nh, MOBA_BLOCK), out(1, hd)],
        ),
        out_shape=[jax.ShapeDtypeStruct((b, nblk, nh, MOBA_BLOCK), F32), jax.ShapeDtypeStruct((b, nblk, 1, hd), F32)],
        compiler_params=_cparams("parallel", "arbitrary"),
    )(page_table, qx, *([cache_kt] * n_pg), bias_dec)


def _dec_select_kernel(q_ref, kn_ref, b0_ref, ks_ref, s_ref, p_ref, pown_ref, sel_ref):
    nblk = sel_ref.shape[1]
    blk = MOBA_BLOCK
    q = _rnd(q_ref[...] * HEAD_DIM ** -0.5)
    lane = lax.broadcasted_iota(jnp.int32, sel_ref.shape, 1)
    gate = jnp.zeros(sel_ref.shape, F32)
    for j in range(nblk):
        kmean = _rnd(ks_ref[:, j * HEAD_DIM:(j + 1) * HEAD_DIM] * (1.0 / blk))
        gate = jnp.where(lane == j, jnp.sum(q * kmean, axis=-1, keepdims=True), gate)
    sel = _rank_select(gate, nblk, lane, nblk, axis=1)
    sel_ref[...] = jnp.where(sel, 1.0, 0.0)
    s_own = jnp.sum(q * _rnd(kn_ref[...]), axis=-1, keepdims=True) + b0_ref[:, 0:1]
    m = s_own
    for j in range(nblk):
        mj = jnp.max(s_ref[:, j * blk:(j + 1) * blk], axis=-1, keepdims=True)
        m = jnp.maximum(m, jnp.where(sel[:, j:j + 1], mj, NEG))
    e_own = jnp.exp(s_own - m)
    den = e_own
    for j in range(nblk):
        e = jnp.where(sel[:, j:j + 1], jnp.exp(s_ref[:, j * blk:(j + 1) * blk] - m), 0.0)
        p_ref[:, j * blk:(j + 1) * blk] = e
        den = den + jnp.sum(e, axis=-1, keepdims=True)
    inv = 1.0 / den
    for j in range(nblk):
        p_ref[:, j * blk:(j + 1) * blk] = _rnd(p_ref[:, j * blk:(j + 1) * blk] * inv)
    pown_ref[...] = jnp.broadcast_to(_rnd(e_own * inv), pown_ref.shape)


def _dec_pv_kernel(pt_ref, src_ref, need_ref, p_ref, *refs):
    del pt_ref, src_ref
    v_refs, (pown_ref, vn_ref, o_ref) = refs[:-3], refs[-3:]
    bi = pl.program_id(0)
    j = pl.program_id(1)
    grp = p_ref.shape[1]
    nh = p_ref.shape[2]
    hd = nh * HEAD_DIM
    page = v_refs[0].shape[-1]

    @pl.when(j == 0)
    def _():
        o_ref[0] = pown_ref[0] * _rnd(vn_ref[0])

    for g in range(grp):
        @pl.when(need_ref[bi, j * grp + g] == 1)
        def _(g=g):
            pb = p_ref[0, g].astype(BF16)
            vv = [r[0, 0].reshape(hd, page).astype(BF16) for r in v_refs[PAGES_PER_BLOCK * g:PAGES_PER_BLOCK * (g + 1)]]
            o_all = _dot_nt(pb[:, :page], vv[0]) + _dot_nt(pb[:, page:], vv[1])
            col = lax.broadcasted_iota(jnp.int32, o_all.shape, 1)
            row = lax.broadcasted_iota(jnp.int32, o_all.shape, 0)
            o_ref[0] += jnp.sum(jnp.where(col // HEAD_DIM == row, o_all, 0.0), axis=0, keepdims=True)


def _moba_decode(proj_s, cache_kt, cache_vt, page_table, rel_bias, layer):
    b, n = proj_s.shape
    w = n // 4
    nh = w // HEAD_DIM
    page = cache_kt.shape[4]
    past_len = page_table.shape[1] * page
    assert past_len % MOBA_BLOCK == 0
    nblk = past_len // MOBA_BLOCK
    q = proj_s[:, w:2 * w].reshape(b, nh, HEAD_DIM)
    qx = jnp.einsum('bhd,hg->bhgd', q, jnp.eye(nh, dtype=F32)).reshape(b, nh, w)
    kpos = np.arange(past_len).reshape(nblk, MOBA_BLOCK)
    bias = rel_bias.astype(F32).T[:, _t5_bucket(jnp.asarray(past_len - kpos, jnp.int32))]
    s_p, ks_p = _dec_scores(qx, cache_kt, page_table, jnp.transpose(bias, (1, 0, 2)), layer)
    rows = b * nh
    blk = MOBA_BLOCK
    to_rows = lambda a, last: jnp.transpose(a.reshape(b, nblk, nh, last), (0, 2, 1, 3)).reshape(rows, nblk * last)
    b0 = jnp.broadcast_to(jnp.tile(rel_bias.astype(F32)[_t5_bucket(jnp.zeros((), jnp.int32))], b)[:, None],
                          (rows, LANES))
    f = jax.ShapeDtypeStruct
    p_rows, p_own, sel = pl.pallas_call(
        _dec_select_kernel, name="moba_dec_select",
        out_shape=[f((rows, nblk * blk), F32), f((rows, LANES), F32), f((rows, nblk), F32)],
        compiler_params=pltpu.CompilerParams(vmem_limit_bytes=VMEM_LIMIT),
    )(q.reshape(rows, HEAD_DIM), proj_s[:, 2 * w:3 * w].reshape(rows, HEAD_DIM), b0,
      to_rows(ks_p, HEAD_DIM), to_rows(s_p, blk))

    grp = math.gcd(DEC_GROUP, nblk)
    need = (jnp.max(sel.reshape(b, nh, nblk), axis=1) > 0.5).reshape(b, nblk // grp, grp)
    blk_id = jnp.arange(nblk, dtype=jnp.int32).reshape(1, nblk // grp, grp)
    last_needed = lax.cummax(jnp.where(need, blk_id, -1), axis=1)
    first_needed = jnp.min(jnp.where(need, blk_id, nblk), axis=1, keepdims=True)
    fallback = jnp.where(first_needed < nblk, first_needed, blk_id[:, :1])
    src = jnp.where(last_needed < 0, fallback, last_needed).astype(jnp.int32).reshape(b, nblk)
    p_blk = jnp.transpose(p_rows.reshape(b, nh, nblk, blk), (0, 2, 1, 3))
    p_own_x = jnp.repeat(p_own[:, 0].reshape(b, nh), HEAD_DIM, axis=1).reshape(b, 1, w)
    page_spec = lambda g, off: pl.BlockSpec(
        (1, 1, nh, HEAD_DIM, page),
        lambda bi, j, pt, sr, nd: (pt[bi, PAGES_PER_BLOCK * sr[bi, j * grp + g] + off], layer, 0, 0, 0))
    seq_spec = pl.BlockSpec((1, 1, w), lambda bi, j, pt, sr, nd: (bi, 0, 0))
    att = pl.pallas_call(
        _dec_pv_kernel, name="moba_dec_pv",
        grid_spec=pltpu.PrefetchScalarGridSpec(
            num_scalar_prefetch=3,
            grid=(b, nblk // grp),
            in_specs=[pl.BlockSpec((1, grp, nh, blk), lambda bi, j, pt, sr, nd: (bi, j, 0, 0))]
            + [page_spec(g, off) for g in range(grp) for off in range(PAGES_PER_BLOCK)]
            + [seq_spec, seq_spec],
            out_specs=seq_spec,
        ),
        out_shape=f((b, 1, w), F32),
        compiler_params=_cparams("parallel", "arbitrary"),
    )(page_table, src, need.reshape(b, nblk).astype(jnp.int32), p_blk,
      *([cache_vt] * (grp * PAGES_PER_BLOCK)), p_own_x, proj_s[:, 3 * w:].reshape(b, 1, w))
    return att.reshape(b, w).astype(BF16)


def _tile(m, pref):
    t = min(m, pref)
    while m % t:
        t //= 2
    return t


def _ff_tile(ff, pref):
    best = LANES
    for t in range(LANES, pref + 1, LANES):
        if ff % t == 0:
            best = t
    return best


def kernel(x_prompt, x_sample, cache_k, cache_v, page_table, state_s5_re, state_s5_im, state_pool, rel_bias,
           norm_mix_e, w_in_e, q_norm_e, k_norm_e, s5_a_re, s5_a_im, s5_log_dt, s5_b_re, s5_b_im, s5_c_re,
           s5_c_im, s5_d, s5_w_glu, s5_b_glu, w_out_e, norm_ffn_e, ffn_w_gate, ffn_w_up, ffn_w_down,
           norm_mix_o, pool_w, pool_scale, norm_ffn_o, router_w, router_b, moe_w_gate, moe_w_up, moe_w_down):
    bp, seq, d = x_prompt.shape
    db = x_sample.shape[0]
    assert x_sample.shape[1] == 1
    depth = norm_mix_e.shape[0] + norm_mix_o.shape[0]
    wdt = w_in_e.shape[2] // 4
    nh = wdt // HEAD_DIM
    past_len = page_table.shape[1] * cache_k.shape[2]
    n_exp = router_w.shape[2]
    gn = s5_a_re.shape[1] * s5_a_re.shape[2]

    mp = bp * seq
    xp = x_prompt.reshape(mp, d)
    xs = x_sample.reshape(db, d)
    tm_p = _tile(mp, 512)
    bias_tiles = _prompt_bias_tiles(rel_bias)
    row = lambda v: v.reshape(1, -1).astype(F32)
    cache_kt = jnp.transpose(cache_k, (0, 1, 3, 4, 2))
    cache_vt = jnp.transpose(cache_v, (0, 1, 3, 4, 2))

    n_even = norm_mix_e.shape[0]
    kp_all = jnp.zeros((mp, n_even * wdt), F32)
    vp_all = jnp.zeros((mp, n_even * wdt), F32)
    ks_all = jnp.zeros((db, n_even * wdt), F32)
    vs_all = jnp.zeros((db, n_even * wdt), F32)
    s5p_l, s5s_re_l, s5s_im_l = [], [], []
    poolp_l, pools_l = [], []
    for layer in range(depth):
        if layer % 2 == 0:
            e = layer // 2
            w_in = w_in_e[e].astype(BF16)
            ones = jnp.ones((wdt,), F32)
            head_gain = jnp.stack([ones, jnp.tile(q_norm_e[e].astype(F32), nh),
                                   jnp.tile(k_norm_e[e].astype(F32), nh), ones]).reshape(4, 1, wdt)
            wb, tabs, lam, wc = _s5_weights(s5_a_re[e], s5_a_im[e], s5_log_dt[e], s5_b_re[e], s5_b_im[e],
                                            s5_c_re[e], s5_c_im[e])
            dsk = row(s5_d[e])
            wgl = s5_w_glu[e].astype(BF16)
            bgl = row(s5_b_glu[e])
            w_out = w_out_e[e].astype(BF16)
            wg, wu, wd = ffn_w_gate[e].astype(BF16), ffn_w_up[e].astype(BF16), ffn_w_down[e].astype(BF16)
            tf = _ff_tile(wg.shape[1], 1536)

            uq, kp_all, vp_all = _in_proj(xp, row(norm_mix_e[e]), w_in, head_gain, kp_all, vp_all, e, tm_p)
            uq3 = uq.reshape(bp, seq, 2 * wdt)
            s5_out, hfin = _s5_scan(uq3, jnp.zeros((bp, 1, 2 * gn), F32), wb, tabs, wc, dsk, wgl, bgl,
                                    _tile(seq, 256))
            att = _moba_prompt(uq3, kp_all.reshape(bp, seq, -1), vp_all.reshape(bp, seq, -1), e, bias_tiles)
            xp = _mix_ffn(xp, s5_out.reshape(mp, wdt), att.reshape(mp, wdt), w_out, row(norm_ffn_e[e]),
                          wg, wu, wd, tm_p, tf)
            s5p_l.append(hfin.reshape(bp, 2, -1, S5_STATE))

            uq_s, ks_all, vs_all = _in_proj(xs, row(norm_mix_e[e]), w_in, head_gain, ks_all, vs_all, e, db)
            proj_s = jnp.concatenate([uq_s, ks_all[:, e * wdt:(e + 1) * wdt], vs_all[:, e * wdt:(e + 1) * wdt]],
                                     axis=1)
            s5_out_s, hr_s, hi_s = _s5_step(proj_s, state_s5_re[:, e].reshape(db, gn).astype(F32),
                                            state_s5_im[:, e].reshape(db, gn).astype(F32),
                                            wb, lam, wc, dsk, wgl, bgl)
            att_s = _moba_decode(proj_s, cache_kt, cache_vt, page_table, rel_bias, e)
            xs = _mix_ffn(xs, s5_out_s, att_s, w_out, row(norm_ffn_e[e]), wg, wu, wd, db, tf)
            s5s_re_l.append(hr_s.reshape(db, -1, S5_STATE))
            s5s_im_l.append(hi_s.reshape(db, -1, S5_STATE))
        else:
            o = layer // 2
            wp = pool_w[o].astype(BF16)
            sc = row(pool_scale[o])
            wr = jnp.zeros((d, LANES), BF16).at[:, :n_exp].set(router_w[o].astype(BF16))
            br = jnp.zeros((1, LANES), F32).at[:, :n_exp].set(router_b[o].astype(F32))
            wg, wu, wd = moe_w_gate[o].astype(BF16), moe_w_up[o].astype(BF16), moe_w_down[o].astype(BF16)
            tf = _ff_tile(wg.shape[2], 1792)

            xp3, hist_p = _pool_prompt(xp.reshape(bp, seq, d), row(norm_mix_o[o]), wp, sc, _tile(seq, 512))
            xp = _moe_sparse(xp3.reshape(mp, d), row(norm_ffn_o[o]), wr, br, wg, wu, wd, tm_p, tf // 2)
            poolp_l.append(hist_p[:, HALO - POOL_HIST:])

            hist_s = state_pool[:, o].astype(F32)
            xs, hn_s = _pool_step(xs, row(norm_mix_o[o]), jnp.transpose(hist_s, (1, 0, 2)), wp, sc,
                                  min(POOL_HIST, past_len))
            xs = _moe_dense(xs, row(norm_ffn_o[o]), wr, br, wg, wu, wd, db, tf)
            pools_l.append(jnp.concatenate([hist_s[:, 1:], hn_s[:, None]], axis=1))

    s5p = jnp.stack(s5p_l, axis=1)
    return (xp.reshape(bp, seq, d), xs.reshape(db, 1, d),
            kp_all.reshape(bp, seq, n_even, nh, HEAD_DIM), vp_all.reshape(bp, seq, n_even, nh, HEAD_DIM),
            ks_all.reshape(db, 1, n_even, nh, HEAD_DIM), vs_all.reshape(db, 1, n_even, nh, HEAD_DIM),
            s5p[:, :, 0], s5p[:, :, 1],
            jnp.stack(s5s_re_l, axis=1), jnp.stack(s5s_im_l, axis=1),
            jnp.stack(poolp_l, axis=1), jnp.stack(pools_l, axis=1))
```

```python
import functools
import math

import jax
import jax.numpy as jnp
import numpy as np
from jax import lax
from jax.experimental import pallas as pl
from jax.experimental.pallas import tpu as pltpu

F32 = jnp.float32
BF16 = jnp.bfloat16
EPS = 1e-6
NEG = -1e30

S5_GROUP = 16
S5_STATE = 64
HEAD_DIM = 64
MOBA_BLOCK = 256
MOBA_TOPK = 3
REL_BUCKETS = 32
REL_MAX_DIST = 1024
POOL_WINDOWS = (2, 4, 8, 16)
POOL_HIST = max(POOL_WINDOWS) - 1
TOP_K_EXPERTS = 2

LANES = 128
SUBLANES = 8
VMEM_LIMIT = 48 * 1024 * 1024


def _cparams(*sem):
    return pltpu.CompilerParams(dimension_semantics=sem, vmem_limit_bytes=VMEM_LIMIT)


def _rmsnorm(x, g):
    ms = jnp.mean(x * x, axis=-1, keepdims=True)
    return x * lax.rsqrt(ms + EPS) * g


def _split_bf16(x):
    hi = x.astype(BF16)
    lo = (x - hi.astype(F32)).astype(BF16)
    return hi, lo


def _dot(a, b):
    return jnp.dot(a, b, preferred_element_type=F32)


def _dot_nt(a, b):
    return lax.dot_general(a, b, (((1,), (1,)), ((), ())), preferred_element_type=F32)


def _dot3(a, b):
    ah, al = _split_bf16(a)
    bh, bl = _split_bf16(b)
    return _dot(ah, bh) + _dot(ah, bl) + _dot(al, bh)


def _dot3_nt(a, b):
    ah, al = _split_bf16(a)
    bh, bl = _split_bf16(b)
    return _dot_nt(ah, bh) + _dot_nt(ah, bl) + _dot_nt(al, bh)


def _silu(x):
    return x * jax.nn.sigmoid(x)


def _in_proj_kernel(x_ref, g_ref, w_ref, hn_ref, bd_ref, k_in, v_in, uq_ref, k_ref, v_ref, h_scr):
    del k_in, v_in
    j = pl.program_id(1)

    @pl.when(j == 0)
    def _():
        h_scr[...] = _rmsnorm(x_ref[...], g_ref[...]).astype(BF16)

    y = _dot(h_scr[...], w_ref[...])

    def head_normed():
        hi, lo = _split_bf16(y * y)
        ms = _dot(hi, bd_ref[...]) + _dot(lo, bd_ref[...])
        return y * lax.rsqrt(ms + EPS) * hn_ref[0]

    @pl.when(j == 0)
    def _():
        uq_ref[...] = y

    @pl.when(j == 1)
    def _():
        uq_ref[...] = head_normed()

    @pl.when(j == 2)
    def _():
        k_ref[...] = head_normed()

    @pl.when(j == 3)
    def _():
        v_ref[...] = y


def _in_proj(x, g, w_bf, head_gain, k_all, v_all, layer, tm):
    m, d = x.shape
    wdt = w_bf.shape[1] // 4
    bd = np.kron(np.eye(wdt // HEAD_DIM), np.full((HEAD_DIM, HEAD_DIM), 1.0 / HEAD_DIM))
    bd = jnp.asarray(bd, BF16)
    kv_spec = pl.BlockSpec((tm, wdt), lambda i, j: (i, layer))
    return pl.pallas_call(
        _in_proj_kernel, name="in_proj",
        grid=(m // tm, 4),
        in_specs=[
            pl.BlockSpec((tm, d), lambda i, j: (i, 0)),
            pl.BlockSpec((1, d), lambda i, j: (0, 0)),
            pl.BlockSpec((d, wdt), lambda i, j: (0, j)),
            pl.BlockSpec((1, 1, wdt), lambda i, j: (j, 0, 0)),
            pl.BlockSpec((wdt, wdt), lambda i, j: (0, 0)),
            pl.BlockSpec(memory_space=pl.ANY),
            pl.BlockSpec(memory_space=pl.ANY),
        ],
        out_specs=[pl.BlockSpec((tm, wdt), lambda i, j: (i, jnp.minimum(j, 1))), kv_spec, kv_spec],
        out_shape=[jax.ShapeDtypeStruct((m, 2 * wdt), F32), jax.ShapeDtypeStruct(k_all.shape, F32),
                   jax.ShapeDtypeStruct(v_all.shape, F32)],
        input_output_aliases={5: 1, 6: 2},
        scratch_shapes=[pltpu.VMEM((tm, d), BF16)],
        compiler_params=_cparams("parallel", "arbitrary"),
    )(x, g, w_bf, head_gain, bd, k_all, v_all)


def _mix_ffn_kernel(x_ref, a_ref, b_ref, wo_ref, g_ref, wg_ref, wu_ref, wd_ref, o_ref,
                    x1_scr, h_scr, acc_scr):
    f = pl.program_id(1)
    half = a_ref.shape[1]

    @pl.when(f == 0)
    def _():
        x1 = (x_ref[...] + _dot(a_ref[...], wo_ref[:half, :]) + _dot(b_ref[...], wo_ref[half:, :]))
        x1_scr[...] = x1
        h_scr[...] = _rmsnorm(x1, g_ref[...]).astype(BF16)
        acc_scr[...] = jnp.zeros_like(acc_scr)

    h = h_scr[...]
    act = (_silu(_dot(h, wg_ref[...])) * _dot(h, wu_ref[...])).astype(BF16)
    acc_scr[...] += _dot(act, wd_ref[...])

    @pl.when(f == pl.num_programs(1) - 1)
    def _():
        o_ref[...] = x1_scr[...] + acc_scr[...]


def _mix_ffn(x, a, b, wo_bf, g, wg_bf, wu_bf, wd_bf, tm, tf):
    m, d = x.shape
    half = a.shape[1]
    ff = wg_bf.shape[1]
    return pl.pallas_call(
        _mix_ffn_kernel, name="mix_ffn",
        grid=(m // tm, ff // tf),
        in_specs=[
            pl.BlockSpec((tm, d), lambda i, f: (i, 0)),
            pl.BlockSpec((tm, half), lambda i, f: (i, 0)),
            pl.BlockSpec((tm, half), lambda i, f: (i, 0)),
            pl.BlockSpec((2 * half, d), lambda i, f: (0, 0)),
            pl.BlockSpec((1, d), lambda i, f: (0, 0)),
            pl.BlockSpec((d, tf), lambda i, f: (0, f)),
            pl.BlockSpec((d, tf), lambda i, f: (0, f)),
            pl.BlockSpec((tf, d), lambda i, f: (f, 0)),
        ],
        out_specs=pl.BlockSpec((tm, d), lambda i, f: (i, 0)),
        out_shape=jax.ShapeDtypeStruct((m, d), F32),
        scratch_shapes=[pltpu.VMEM((tm, d), F32), pltpu.VMEM((tm, d), BF16), pltpu.VMEM((tm, d), F32)],
        compiler_params=_cparams("parallel", "arbitrary"),
    )(x, a, b, wo_bf, g, wg_bf, wu_bf, wd_bf)


def _route(h, wr, br, n_exp):
    logits = _dot(h.astype(BF16), wr) + br
    lane = lax.broadcasted_iota(jnp.int32, logits.shape, 1)
    logits = jnp.where(lane < n_exp, logits, NEG)
    m1 = jnp.max(logits, axis=-1, keepdims=True)
    i1 = jnp.min(jnp.where(logits == m1, lane, LANES), axis=-1, keepdims=True)
    rest = jnp.where(lane == i1, NEG, logits)
    m2 = jnp.max(rest, axis=-1, keepdims=True)
    i2 = jnp.min(jnp.where(rest == m2, lane, LANES), axis=-1, keepdims=True)
    e2 = jnp.exp(m2 - m1)
    g1 = 1.0 / (1.0 + e2)
    g2 = e2 / (1.0 + e2)
    gate = jnp.where(lane == i1, g1, 0.0) + jnp.where(lane == i2, g2, 0.0)
    return gate, jnp.logical_or(lane == i1, lane == i2)


def _moe_dense_kernel(x_ref, g_ref, wr_ref, br_ref, wg_ref, wu_ref, wd_ref, o_ref,
                      h_scr, gate_scr, acc_scr, *, n_exp):
    e = pl.program_id(1)
    f = pl.program_id(2)

    @pl.when(jnp.logical_and(e == 0, f == 0))
    def _():
        h = _rmsnorm(x_ref[...], g_ref[...])
        h_scr[...] = h.astype(BF16)
        gate_scr[...] = _route(h, wr_ref[...], br_ref[...], n_exp)[0]
        acc_scr[...] = jnp.zeros_like(acc_scr)

    h = h_scr[...]
    lane = lax.broadcasted_iota(jnp.int32, gate_scr.shape, 1)
    w_e = jnp.sum(jnp.where(lane == e, gate_scr[...], 0.0), axis=-1, keepdims=True)
    act = (_silu(_dot(h, wg_ref[0, 0])) * _dot(h, wu_ref[0, 0])).astype(BF16)
    acc_scr[...] += w_e * _dot(act, wd_ref[0, 0])

    @pl.when(jnp.logical_and(e == n_exp - 1, f == pl.num_programs(2) - 1))
    def _():
        o_ref[...] = x_ref[...] + acc_scr[...]


def _moe_dense(x, g, wr_pad, br_pad, wg_bf, wu_bf, wd_bf, layer, tm, tf):
    m, d = x.shape
    _, n_exp, _, ff = wg_bf.shape
    return pl.pallas_call(
        functools.partial(_moe_dense_kernel, n_exp=n_exp), name="moe_dense",
        grid=(m // tm, n_exp, ff // tf),
        in_specs=[
            pl.BlockSpec((tm, d), lambda i, e, f: (i, 0)),
            pl.BlockSpec((1, d), lambda i, e, f: (0, 0)),
            pl.BlockSpec((d, LANES), lambda i, e, f: (0, 0)),
            pl.BlockSpec((1, LANES), lambda i, e, f: (0, 0)),
            pl.BlockSpec((1, 1, d, tf), lambda i, e, f: (layer, e, 0, f)),
            pl.BlockSpec((1, 1, d, tf), lambda i, e, f: (layer, e, 0, f)),
            pl.BlockSpec((1, 1, tf, d), lambda i, e, f: (layer, e, f, 0)),
        ],
        out_specs=pl.BlockSpec((tm, d), lambda i, e, f: (i, 0)),
        out_shape=jax.ShapeDtypeStruct((m, d), F32),
        scratch_shapes=[pltpu.VMEM((tm, d), BF16), pltpu.VMEM((tm, LANES), F32), pltpu.VMEM((tm, d), F32)],
        compiler_params=_cparams("parallel", "arbitrary", "arbitrary"),
    )(x, g, wr_pad, br_pad, wg_bf, wu_bf, wd_bf)


def _moe_route_kernel(x_ref, g_ref, wr_ref, br_ref, h_ref, gate_ref, pos_ref, post_ref, cb_ref, tot_ref, carry,
                      *, n_exp):
    i = pl.program_id(0)
    tm = x_ref.shape[0]

    @pl.when(i == 0)
    def _():
        carry[...] = jnp.zeros_like(carry)

    h = _rmsnorm(x_ref[...], g_ref[...])
    h_ref[...] = h.astype(BF16)
    gate, chosen = _route(h, wr_ref[...], br_ref[...], n_exp)
    gate_ref[...] = gate
    onehot = jnp.where(chosen, 1.0, 0.0)
    r = lax.broadcasted_iota(jnp.int32, (tm, tm), 0)
    c = lax.broadcasted_iota(jnp.int32, (tm, tm), 1)
    before = _dot(jnp.where(c < r, 1.0, 0.0).astype(BF16), onehot.astype(BF16))
    start = carry[...]
    pos = jnp.where(chosen, before + start, -1.0)
    pos_ref[...] = pos
    post_ref[...] = pos.T[:post_ref.shape[0], :]
    cb_ref[0] = start
    total = start + jnp.sum(onehot, axis=0, keepdims=True)
    carry[...] = total
    tot_ref[...] = total


def _moe_route(x, g, wr_bf, br, n_exp, tm):
    m, d = x.shape
    nt = m // tm
    ne8 = -(-n_exp // SUBLANES) * SUBLANES
    f = jax.ShapeDtypeStruct
    return pl.pallas_call(
        functools.partial(_moe_route_kernel, n_exp=n_exp), name="moe_route",
        grid=(nt,),
        in_specs=[
            pl.BlockSpec((tm, d), lambda i: (i, 0)),
            pl.BlockSpec((1, d), lambda i: (0, 0)),
            pl.BlockSpec((d, LANES), lambda i: (0, 0)),
            pl.BlockSpec((1, LANES), lambda i: (0, 0)),
        ],
        out_specs=[
            pl.BlockSpec((tm, d), lambda i: (i, 0)),
            pl.BlockSpec((tm, LANES), lambda i: (i, 0)),
            pl.BlockSpec((tm, LANES), lambda i: (i, 0)),
            pl.BlockSpec((ne8, tm), lambda i: (0, i)),
            pl.BlockSpec((1, 1, LANES), lambda i: (i, 0, 0)),
            pl.BlockSpec((1, LANES), lambda i: (0, 0)),
        ],
        out_shape=[f((m, d), BF16), f((m, LANES), F32), f((m, LANES), F32), f((ne8, m), F32),
                   f((nt, 1, LANES), F32), f((1, LANES), F32)],
        scratch_shapes=[pltpu.VMEM((1, LANES), F32)],
        compiler_params=_cparams("arbitrary"),
    )(x, g, wr_bf, br)


def _moe_gather_kernel(pc_ref, ps_ref, pe_ref, pfirst_ref, pvalid_ref, off_ref, h_ref, post_ref, o_ref, acc):
    k = pl.program_id(0)
    rc, tm = o_ref.shape[0], h_ref.shape[0]

    @pl.when(pfirst_ref[k] == 1)
    def _():
        acc[...] = jnp.zeros_like(acc)

    @pl.when(pvalid_ref[k] == 1)
    def _():
        e = pe_ref[k]
        dest = post_ref[pl.ds(e, 1), :] + off_ref[e].astype(F32)
        rows = (pc_ref[k] * rc + lax.broadcasted_iota(jnp.int32, (rc, tm), 0)).astype(F32)
        acc[...] += _dot(jnp.where(dest == rows, 1.0, 0.0).astype(BF16), h_ref[...])

    o_ref[...] = acc[...].astype(BF16)


def _moe_ffn_kernel(ce_ref, cv_ref, x_ref, wg_ref, wu_ref, wd_ref, y_ref, *, tf):
    c = pl.program_id(0)

    @pl.when(cv_ref[c] == 1)
    def _():
        x = x_ref[...]
        acc = jnp.zeros(y_ref.shape, F32)
        for f0 in range(0, wg_ref.shape[3], tf):
            act = (_silu(_dot(x, wg_ref[0, 0, :, f0:f0 + tf])) * _dot(x, wu_ref[0, 0, :, f0:f0 + tf])).astype(BF16)
            acc = acc + _dot(act, wd_ref[0, 0, f0:f0 + tf, :])
        y_ref[...] = acc.astype(BF16)

    @pl.when(cv_ref[c] == 0)
    def _():
        y_ref[...] = jnp.zeros_like(y_ref)


def _moe_combine_kernel(qs_ref, qc_ref, qe_ref, qfirst_ref, qvalid_ref, off_ref, x_ref, y_ref, pos_ref, gate_ref,
                        o_ref):
    k = pl.program_id(0)
    tm, rc = x_ref.shape[0], y_ref.shape[0]

    @pl.when(qfirst_ref[k] == 1)
    def _():
        o_ref[...] = x_ref[...]

    @pl.when(qvalid_ref[k] == 1)
    def _():
        e = qe_ref[k]
        lane = lax.broadcasted_iota(jnp.int32, pos_ref.shape, 1)
        pos_e = jnp.sum(jnp.where(lane == e, pos_ref[...], 0.0), axis=-1, keepdims=True)
        gate_e = jnp.sum(jnp.where(lane == e, gate_ref[...], 0.0), axis=-1, keepdims=True)
        dest = pos_e + off_ref[e].astype(F32)
        cols = (qc_ref[k] * rc + lax.broadcasted_iota(jnp.int32, (tm, rc), 1)).astype(F32)
        pick = jnp.where(dest == cols, 1.0, 0.0).astype(BF16)
        o_ref[...] += gate_e * _dot(pick, y_ref[...])


def _pair_list(inter, n_pairs_max):
    nb = inter.shape[1]
    flat = inter.reshape(-1)
    n = jnp.sum(flat.astype(jnp.int32))
    idx = jnp.nonzero(flat, size=n_pairs_max, fill_value=0)[0].astype(jnp.int32)
    k = jnp.arange(n_pairs_max, dtype=jnp.int32)
    idx = jnp.where(k < n, idx, idx[jnp.maximum(n - 1, 0)])
    a, b = idx // nb, idx % nb
    valid = k < n
    first = jnp.logical_and(valid, jnp.logical_or(k == 0, a != jnp.roll(a, 1)))
    return a, b, valid.astype(jnp.int32), first.astype(jnp.int32)


ROW_CHUNK = 256


def _moe_sparse(x, g, wr_bf, br, wg_bf, wu_bf, wd_bf, layer, tm, tf):
    m, d = x.shape
    _, n_exp, _, ff = wg_bf.shape
    nt = m // tm
    rc = ROW_CHUNK
    h, gate, pos, pos_t, cb, tot = _moe_route(x, g, wr_bf, br, n_exp, tm)

    cnt = tot[0, :n_exp].astype(jnp.int32)
    cbi = jnp.concatenate([cb[:, 0, :n_exp], tot[:, :n_exp]], axis=0).astype(jnp.int32)
    gsz = (cnt + rc - 1) // rc * rc
    ends = jnp.cumsum(gsz)
    off = (ends - gsz).astype(jnp.int32)
    n_chunks = TOP_K_EXPERTS * m // rc + n_exp
    cstart = jnp.arange(n_chunks, dtype=jnp.int32) * rc
    ce = jnp.minimum(jnp.searchsorted(ends, cstart, side='right'), n_exp - 1).astype(jnp.int32)
    cv = cstart < ends[-1]
    per_tile = cbi.T[ce]
    start = off[ce][:, None] + per_tile[:, :-1]
    end = off[ce][:, None] + per_tile[:, 1:]
    inter = (cv[:, None] & (end > start) & (start < cstart[:, None] + rc) & (end > cstart[:, None]))
    n_pairs = n_chunks + n_exp * nt
    unused = jnp.logical_and(jnp.logical_not(cv)[:, None], jnp.arange(nt)[None, :] == 0)
    pc, ps, plisted, pfirst = _pair_list(jnp.logical_or(inter, unused), n_pairs + n_exp)
    pvalid = plisted * inter[pc, ps].astype(jnp.int32)
    qs, qc, qvalid, qfirst = _pair_list(inter.T, n_pairs)

    xs = pl.pallas_call(
        _moe_gather_kernel, name="moe_gather",
        grid_spec=pltpu.PrefetchScalarGridSpec(
            num_scalar_prefetch=6,
            grid=(n_pairs + n_exp,),
            in_specs=[
                pl.BlockSpec((tm, d), lambda k, pc, ps, *_: (ps[k], 0)),
                pl.BlockSpec((pos_t.shape[0], tm), lambda k, pc, ps, *_: (0, ps[k])),
            ],
            out_specs=pl.BlockSpec((rc, d), lambda k, pc, *_: (pc[k], 0)),
            scratch_shapes=[pltpu.VMEM((rc, d), F32)],
        ),
        out_shape=jax.ShapeDtypeStruct((n_chunks * rc, d), BF16),
        compiler_params=_cparams("arbitrary"),
    )(pc, ps, ce[pc], pfirst, pvalid, off, h, pos_t)

    cvi = cv.astype(jnp.int32)
    wspec = lambda shape: pl.BlockSpec((1,) + shape, lambda c, ce, cv: (layer, ce[c], 0, 0),
                                       pipeline_mode=pl.Buffered(1))
    ys = pl.pallas_call(
        functools.partial(_moe_ffn_kernel, tf=tf), name="moe_ffn",
        grid_spec=pltpu.PrefetchScalarGridSpec(
            num_scalar_prefetch=2,
            grid=(n_chunks,),
            in_specs=[
                pl.BlockSpec((rc, d), lambda c, ce, cv: (c * cv[c], 0)),
                wspec((1, d, ff)), wspec((1, d, ff)), wspec((1, ff, d)),
            ],
            out_specs=pl.BlockSpec((rc, d), lambda c, ce, cv: (c, 0)),
        ),
        out_shape=jax.ShapeDtypeStruct((n_chunks * rc, d), BF16),
        compiler_params=_cparams("arbitrary"),
    )(ce, cvi, xs, wg_bf, wu_bf, wd_bf)

    return pl.pallas_call(
        _moe_combine_kernel, name="moe_combine",
        grid_spec=pltpu.PrefetchScalarGridSpec(
            num_scalar_prefetch=6,
            grid=(n_pairs,),
            in_specs=[
                pl.BlockSpec((tm, d), lambda k, qs, qc, *_: (qs[k], 0)),
                pl.BlockSpec((rc, d), lambda k, qs, qc, *_: (qc[k], 0)),
                pl.BlockSpec((tm, LANES), lambda k, qs, *_: (qs[k], 0)),
                pl.BlockSpec((tm, LANES), lambda k, qs, *_: (qs[k], 0)),
            ],
            out_specs=pl.BlockSpec((tm, d), lambda k, qs, *_: (qs[k], 0)),
        ),
        out_shape=jax.ShapeDtypeStruct((m, d), F32),
        compiler_params=_cparams("arbitrary"),
    )(qs, qc, ce[qc], qfirst, qvalid, off, x, ys, pos, gate)


HALO = 16


def _pool_kernel(x_ref, g_ref, wp_ref, sc_ref, o_ref, hist_ref, hbuf):
    t = pl.program_id(1)
    tm = x_ref.shape[1]
    gw = wp_ref.shape[1]

    @pl.when(t == 0)
    def _():
        hbuf[0:HALO, :] = jnp.zeros((HALO, hbuf.shape[1]), F32)

    @pl.when(t > 0)
    def _():
        hbuf[0:HALO, :] = hbuf[tm:tm + HALO, :]

    x = x_ref[0]
    hn = _rmsnorm(x, g_ref[...])
    hbuf[HALO:HALO + tm, :] = hn
    pos = t * tm + lax.broadcasted_iota(jnp.int32, (tm, 1), 0)
    ys = []
    for gi, w in enumerate(POOL_WINDOWS):
        sl = slice(gi * gw, (gi + 1) * gw)
        tot = hn[:, sl]
        for i in range(1, w):
            tot = tot + hbuf[HALO - i:HALO - i + tm, sl]
        cnt = jnp.minimum(pos + 1, w).astype(F32)
        pooled = tot / cnt - hn[:, sl]
        ys.append(_dot(pooled.astype(BF16), wp_ref[gi]))
    o_ref[0] = x + jnp.concatenate(ys, axis=1) * sc_ref[...]

    @pl.when(t == pl.num_programs(1) - 1)
    def _():
        hist_ref[0] = hbuf[tm:tm + HALO, :]


def _pool_prompt(x, g, wp_bf, sc, tm):
    b, s, d = x.shape
    ng, gw, _ = wp_bf.shape
    return pl.pallas_call(
        _pool_kernel, name="pool_prompt",
        grid=(b, s // tm),
        in_specs=[
            pl.BlockSpec((1, tm, d), lambda i, t: (i, t, 0)),
            pl.BlockSpec((1, d), lambda i, t: (0, 0)),
            pl.BlockSpec((ng, gw, gw), lambda i, t: (0, 0, 0)),
            pl.BlockSpec((1, d), lambda i, t: (0, 0)),
        ],
        out_specs=[
            pl.BlockSpec((1, tm, d), lambda i, t: (i, t, 0)),
            pl.BlockSpec((1, HALO, d), lambda i, t: (i, 0, 0)),
        ],
        out_shape=[jax.ShapeDtypeStruct((b, s, d), F32), jax.ShapeDtypeStruct((b, HALO, d), F32)],
        scratch_shapes=[pltpu.VMEM((HALO + tm, d), F32)],
        compiler_params=_cparams("parallel", "arbitrary"),
    )(x, g, wp_bf, sc)


def _pool_step_kernel(x_ref, g_ref, hist_ref, wp_ref, sc_ref, o_ref, hn_ref, *, n_valid):
    gw = wp_ref.shape[1]
    x = x_ref[...]
    hn = _rmsnorm(x, g_ref[...])
    hn_ref[...] = hn
    ys = []
    for gi, w in enumerate(POOL_WINDOWS):
        sl = slice(gi * gw, (gi + 1) * gw)
        tot = hn[:, sl]
        for i in range(1, min(w, n_valid + 1)):
            tot = tot + hist_ref[POOL_HIST - i][:, sl]
        pooled = tot / float(min(w, n_valid + 1)) - hn[:, sl]
        ys.append(_dot(pooled.astype(BF16), wp_ref[gi]))
    o_ref[...] = x + jnp.concatenate(ys, axis=1) * sc_ref[...]


def _pool_step(x, g, hist_t, wp_bf, sc, n_valid):
    b, d = x.shape
    return pl.pallas_call(
        functools.partial(_pool_step_kernel, n_valid=n_valid), name="pool_step",
        out_shape=[jax.ShapeDtypeStruct((b, d), F32), jax.ShapeDtypeStruct((b, d), F32)],
        compiler_params=pltpu.CompilerParams(vmem_limit_bytes=VMEM_LIMIT),
    )(x, g, hist_t, wp_bf, sc)


N_POW = SUBLANES


def _s5_prep_kernel(ar_ref, ai_ref, ldt_ref, br_ref, bi_ref, pwr_ref, pwi_ref, bbr_ref, bbi_ref):
    ar = ar_ref[...]
    ai = ai_ref[...]
    dt = jnp.exp(ldt_ref[...])
    mag = jnp.exp(ar * dt)
    lr = mag * jnp.cos(ai * dt)
    li = mag * jnp.sin(ai * dt)
    den = ar * ar + ai * ai
    rr = ((lr - 1.0) * ar + li * ai) / den
    ri = (li * ar - (lr - 1.0) * ai) / den
    br = br_ref[...]
    bi = bi_ref[...]
    bbr_ref[...] = rr * br - ri * bi
    bbi_ref[...] = rr * bi + ri * br
    cr, ci = lr, li
    for k in range(N_POW):
        pwr_ref[k] = cr
        pwi_ref[k] = ci
        cr, ci = cr * lr - ci * li, cr * li + ci * lr


def _s5_prep(a_re, a_im, log_dt, b_re, b_im):
    g, n = a_re.shape
    p = b_re.shape[2]
    f = jax.ShapeDtypeStruct
    return pl.pallas_call(
        _s5_prep_kernel, name="s5_prep",
        out_shape=[f((N_POW, g, 1, n), F32), f((N_POW, g, 1, n), F32), f((g, p, n), F32), f((g, p, n), F32)],
    )(a_re[:, None, :], a_im[:, None, :], log_dt[:, None, None],
      jnp.transpose(b_re, (0, 2, 1)), jnp.transpose(b_im, (0, 2, 1)))


def _s5_weights(a_re, a_im, log_dt, b_re, b_im, c_re, c_im):
    g, n = a_re.shape
    p = b_re.shape[2]
    gn = g * n
    pwr, pwi, bbr, bbi = _s5_prep(a_re, a_im, log_dt, b_re, b_im)
    pwr = pwr.reshape(N_POW, gn)
    pwi = pwi.reshape(N_POW, gn)
    row = np.arange(SUBLANES)[:, None]
    tabs = []
    for k in (1, 2, 4):
        tabs += [jnp.where(row >= k, pwr[k - 1][None], 0.0), jnp.where(row >= k, pwi[k - 1][None], 0.0)]
    tabs += [pwr, pwi]
    tabs = jnp.stack(tabs)
    lam = jnp.stack([pwr[0], pwi[0]])
    n_tiles = gn // LANES
    g_tile = LANES // n
    g_chunk = LANES // p
    place = np.zeros((n_tiles, g_chunk, g_tile), np.float32)
    for j in range(n_tiles):
        for i in range(g_tile):
            place[j, (j * g_tile + i) % g_chunk, i] = 1.0
    bb = jnp.stack([bbr, bbi]).reshape(2, n_tiles, g_tile, p, n)
    wb = jnp.einsum('cjipn,jli->jlpcin', bb, place).reshape(n_tiles, LANES, 2 * LANES).astype(BF16)
    n_out = g * p // LANES
    cc = jnp.stack([c_re, -c_im]).reshape(2, n_out, g_chunk, p, n)
    wc = jnp.einsum('cmlpn,lk->mclnkp', cc, np.eye(g_chunk, dtype=np.float32))
    wc = wc.reshape(n_out, 2 * g_chunk * n, LANES).astype(BF16)
    return wb, tabs, lam, wc


def _s5_drive(u, wb_ref, bur, bui, row0):
    t = u.shape[0]
    ub = u.astype(BF16)
    n_tiles = wb_ref.shape[0]
    u_per_tile = u.shape[1] // n_tiles
    for j in range(n_tiles):
        c0 = (j * u_per_tile) // LANES * LANES
        r = _dot(ub[:, c0:c0 + LANES], wb_ref[j])
        bur[row0:row0 + t, j * LANES:(j + 1) * LANES] = r[:, :LANES]
        bui[row0:row0 + t, j * LANES:(j + 1) * LANES] = r[:, LANES:]


def _s5_readout(u, hr, hi, wc_ref, d_ref, wgl_ref, bgl_ref):
    hrb = hr.astype(BF16)
    hib = hi.astype(BF16)
    kk = wc_ref.shape[1] // 2
    ys = []
    for m in range(wc_ref.shape[0]):
        ys.append(_dot(hrb[:, m * kk:(m + 1) * kk], wc_ref[m, :kk, :])
                  + _dot(hib[:, m * kk:(m + 1) * kk], wc_ref[m, kk:, :]))
    y = jnp.concatenate(ys, axis=1) + d_ref[...] * u
    z = jax.nn.gelu(y, approximate=True)
    return z * jax.nn.sigmoid(_dot(z.astype(BF16), wgl_ref[...]) + bgl_ref[...])


def _s5_scan_kernel(u_ref, h0_ref, wb_ref, tab_ref, wc_ref, d_ref, wgl_ref, bgl_ref, o_ref, hf_ref, bur, bui):
    t = pl.program_id(1)
    tt = u_ref.shape[1]
    gn = bur.shape[1]
    c0 = SUBLANES

    @pl.when(t == 0)
    def _():
        bur[0:c0, :] = jnp.broadcast_to(h0_ref[0, :, :gn], (c0, gn))
        bui[0:c0, :] = jnp.broadcast_to(h0_ref[0, :, gn:], (c0, gn))

    @pl.when(t > 0)
    def _():
        bur[0:c0, :] = bur[tt:tt + c0, :]
        bui[0:c0, :] = bui[tt:tt + c0, :]

    u = u_ref[0]
    _s5_drive(u, wb_ref, bur, bui, c0)

    def block(b, carry):
        base = pl.multiple_of(b * SUBLANES, SUBLANES)
        cr = bur[pl.ds(base + c0 - 1, 1), :]
        ci = bui[pl.ds(base + c0 - 1, 1), :]
        xr = bur[pl.ds(base + c0, SUBLANES), :]
        xi = bui[pl.ds(base + c0, SUBLANES), :]
        for s, k in enumerate((1, 2, 4)):
            ar = tab_ref[2 * s]
            ai = tab_ref[2 * s + 1]
            sr = pltpu.roll(xr, k, axis=0)
            si = pltpu.roll(xi, k, axis=0)
            xr, xi = xr + ar * sr - ai * si, xi + ar * si + ai * sr
        pr = tab_ref[6]
        pi = tab_ref[7]
        bur[pl.ds(base + c0, SUBLANES), :] = xr + pr * cr - pi * ci
        bui[pl.ds(base + c0, SUBLANES), :] = xi + pr * ci + pi * cr
        return carry

    lax.fori_loop(0, tt // SUBLANES, block, 0)
    o_ref[0] = _s5_readout(u, bur[c0:c0 + tt, :], bui[c0:c0 + tt, :], wc_ref, d_ref, wgl_ref, bgl_ref).astype(BF16)

    @pl.when(t == pl.num_programs(1) - 1)
    def _():
        hf_ref[0] = jnp.concatenate([bur[tt + c0 - 1:tt + c0, :], bui[tt + c0 - 1:tt + c0, :]], axis=1)


def _s5_scan(proj, h0, wb, tabs, wc, dsk, wgl_bf, bgl, tt):
    b, s, _ = proj.shape
    w = dsk.shape[1]
    gn = tabs.shape[2]
    full = lambda a: pl.BlockSpec(a.shape, lambda i, t: (0,) * a.ndim)
    return pl.pallas_call(
        _s5_scan_kernel, name="s5_scan",
        grid=(b, s // tt),
        in_specs=[
            pl.BlockSpec((1, tt, w), lambda i, t: (i, t, 0)),
            pl.BlockSpec((1, 1, 2 * gn), lambda i, t: (i, 0, 0)),
            full(wb), full(tabs), full(wc), full(dsk), full(wgl_bf), full(bgl),
        ],
        out_specs=[
            pl.BlockSpec((1, tt, w), lambda i, t: (i, t, 0)),
            pl.BlockSpec((1, 1, 2 * gn), lambda i, t: (i, 0, 0)),
        ],
        out_shape=[jax.ShapeDtypeStruct((b, s, w), BF16), jax.ShapeDtypeStruct((b, 1, 2 * gn), F32)],
        scratch_shapes=[pltpu.VMEM((tt + SUBLANES, gn), F32), pltpu.VMEM((tt + SUBLANES, gn), F32)],
        compiler_params=_cparams("parallel", "arbitrary"),
    )(proj, h0, wb, tabs, wc, dsk, wgl_bf, bgl)


def _s5_step_kernel(u_ref, h0r_ref, h0i_ref, wb_ref, lam_ref, wc_ref, d_ref, wgl_ref, bgl_ref,
                    o_ref, hr_ref, hi_ref, bur, bui):
    u = u_ref[...]
    _s5_drive(u, wb_ref, bur, bui, 0)
    lr = lam_ref[0:1, :]
    li = lam_ref[1:2, :]
    h0r = h0r_ref[...]
    h0i = h0i_ref[...]
    hr = bur[...] + lr * h0r - li * h0i
    hi = bui[...] + lr * h0i + li * h0r
    hr_ref[...] = hr
    hi_ref[...] = hi
    o_ref[...] = _s5_readout(u, hr, hi, wc_ref, d_ref, wgl_ref, bgl_ref).astype(BF16)


def _s5_step(proj, h0r, h0i, wb, lam, wc, dsk, wgl_bf, bgl):
    b = proj.shape[0]
    w = dsk.shape[1]
    gn = lam.shape[1]
    full = lambda a: pl.BlockSpec(a.shape, lambda i: (0,) * a.ndim)
    return pl.pallas_call(
        _s5_step_kernel, name="s5_step",
        grid=(1,),
        in_specs=[pl.BlockSpec((b, w), lambda i: (0, 0)), full(h0r), full(h0i), full(wb), full(lam), full(wc),
                  full(dsk), full(wgl_bf), full(bgl)],
        out_specs=[pl.BlockSpec((b, w), lambda i: (0, 0)), pl.BlockSpec((b, gn), lambda i: (0, 0)),
                   pl.BlockSpec((b, gn), lambda i: (0, 0))],
        out_shape=[jax.ShapeDtypeStruct((b, w), BF16), jax.ShapeDtypeStruct((b, gn), F32),
                   jax.ShapeDtypeStruct((b, gn), F32)],
        scratch_shapes=[pltpu.VMEM((b, gn), F32), pltpu.VMEM((b, gn), F32)],
        compiler_params=_cparams("arbitrary"),
    )(proj, h0r, h0i, wb, lam, wc, dsk, wgl_bf, bgl)


def _t5_bucket(rel):
    n = jnp.maximum(rel, 0)
    max_exact = REL_BUCKETS // 2
    nf = jnp.maximum(n, 1).astype(F32)
    large = max_exact + (jnp.log(nf / max_exact) / math.log(REL_MAX_DIST / max_exact)
                         * (REL_BUCKETS - max_exact)).astype(jnp.int32)
    large = jnp.minimum(large, REL_BUCKETS - 1)
    return jnp.where(n < max_exact, n, large)


N_DIST = -(-(REL_MAX_DIST + MOBA_BLOCK - 1) // MOBA_BLOCK) + 1


def _prompt_bias_tiles(rel_bias):
    blk = MOBA_BLOCK
    span = 2 * blk - 1
    rel = jnp.asarray(np.arange(-(blk - 1), N_DIST * blk), jnp.int32)
    line = jnp.where(rel >= 0, rel_bias.astype(F32).T[:, _t5_bucket(rel)], NEG)
    diag = jnp.stack([line[:, d * blk:d * blk + span] for d in range(N_DIST)], axis=1)
    line_rc = jnp.roll(diag[..., ::-1], -(blk - 1), axis=-1)
    rep = jnp.tile(line_rc, (1, 1, blk))[..., :blk * (span - 1)]
    return rep.reshape(diag.shape[0], N_DIST, blk, span - 1)[..., :blk]


def _rank_select(g, n_valid, idx, n_cand, axis):
    cnt = jnp.zeros(g.shape, jnp.int32)
    for jp in range(n_cand):
        gp = lax.slice_in_dim(g, jp, jp + 1, axis=axis)
        beats = jnp.logical_or(gp > g, jnp.logical_and(gp == g, jp < idx))
        cnt = cnt + jnp.where(jnp.logical_and(beats, jp < n_valid), 1, 0)
    return jnp.logical_and(idx < n_valid, cnt < MOBA_TOPK)


SWEEP_UNROLL = 4


def _moba_kernel(q_ref, k_ref, v_ref, bias_ref, o_ref, kft, vb, km, qa, sbuf, red, stat, acc_scr):
    i = pl.program_id(2)
    blk = MOBA_BLOCK
    nb = k_ref.shape[1] // blk
    n_hd = LANES // HEAD_DIM
    scale = HEAD_DIM ** -0.5

    @pl.when(i == 0)
    def _():
        feat = lax.broadcasted_iota(jnp.int32, (LANES, blk), 0)
        for j in range(nb):
            kj = k_ref[0, j * blk:(j + 1) * blk, :]
            kft[j, :LANES, :] = kj.T.astype(BF16)
            kft[j, LANES:, :] = jnp.where(feat == j, 1.0, 0.0).astype(BF16)
            km[j:j + 1, :] = jnp.mean(kj, axis=0, keepdims=True)
        vb[...] = v_ref[0].astype(BF16)

    lane = lax.broadcasted_iota(jnp.int32, (blk, LANES), 1)
    wide = lambda a: jnp.concatenate([a, a], axis=1)
    rep = lambda a: jnp.broadcast_to(a, (blk, LANES))
    for hd in range(n_hd):
        qm = (jnp.where(lane // HEAD_DIM == hd, q_ref[0], 0.0) * scale).astype(BF16)
        g = _dot_nt(km[...].astype(BF16), qm)
        jrow = lax.broadcasted_iota(jnp.int32, g.shape, 0)
        keep = jnp.logical_or(_rank_select(g, i, jrow, nb, axis=0), jrow == i)
        pen = jnp.concatenate([jnp.where(keep, 0.0, NEG), jnp.zeros((LANES - nb, blk), F32)], axis=0)
        qa[hd, :, :LANES] = qm
        qa[hd, :, LANES:] = pen.T.astype(BF16)
        red[hd] = jnp.full((blk, LANES), NEG, F32)

    def sweep(step):
        def group(t, carry):
            for u in range(SWEEP_UNROLL):
                step(SWEEP_UNROLL * t + u)
            return carry

        def single(j, carry):
            step(j)
            return carry

        n_grouped = (i + 1) // SWEEP_UNROLL
        lax.fori_loop(0, n_grouped, group, 0)
        lax.fori_loop(n_grouped * SWEEP_UNROLL, i + 1, single, 0)

    def scores(j):
        dist = jnp.minimum(i - j, N_DIST - 1)
        kt = kft[j]
        for hd in range(n_hd):
            s = _dot(qa[hd], kt) + bias_ref[hd, dist]
            sbuf[hd, j] = s
            red[hd] = jnp.maximum(red[hd], jnp.maximum(s[:, :LANES], s[:, LANES:]))

    sweep(scores)
    for hd in range(n_hd):
        stat[hd] = rep(jnp.max(red[hd], axis=-1, keepdims=True))
        red[hd] = jnp.zeros((blk, LANES), F32)

    def exps(j):
        for hd in range(n_hd):
            e = jnp.exp(sbuf[hd, j] - wide(stat[hd]))
            sbuf[hd, j] = e
            red[hd] += e[:, :LANES] + e[:, LANES:]

    sweep(exps)
    for hd in range(n_hd):
        stat[hd] = rep(1.0 / jnp.sum(red[hd], axis=-1, keepdims=True))
        acc_scr[hd] = jnp.zeros((blk, LANES), F32)

    def weighted(j):
        vj = vb[pl.ds(pl.multiple_of(j * blk, blk), blk), :]
        for hd in range(n_hd):
            p = (sbuf[hd, j] * wide(stat[hd])).astype(BF16)
            acc_scr[hd] += _dot(p, vj)

    sweep(weighted)
    o = acc_scr[0]
    for hd in range(1, n_hd):
        o = jnp.where(lane // HEAD_DIM == hd, acc_scr[hd], o)
    o_ref[0] = o.astype(BF16)


def _moba_prompt(uq, k_all, v_all, layer, bias_tiles):
    b, s, n = uq.shape
    w = n // 2
    hp = w // LANES
    heads_per = LANES // HEAD_DIM
    blk = MOBA_BLOCK
    assert s % blk == 0 and s // blk <= LANES
    return pl.pallas_call(
        _moba_kernel, name="moba_prompt",
        grid=(b, hp, s // blk),
        in_specs=[
            pl.BlockSpec((1, blk, LANES), lambda bi, h, i: (bi, i, hp + h)),
            pl.BlockSpec((1, s, LANES), lambda bi, h, i: (bi, 0, layer * hp + h)),
            pl.BlockSpec((1, s, LANES), lambda bi, h, i: (bi, 0, layer * hp + h)),
            pl.BlockSpec((heads_per, N_DIST, blk, blk), lambda bi, h, i: (h, 0, 0, 0)),
        ],
        out_specs=pl.BlockSpec((1, blk, LANES), lambda bi, h, i: (bi, i, h)),
        out_shape=jax.ShapeDtypeStruct((b, s, w), BF16),
        scratch_shapes=[
            pltpu.VMEM((s // blk, 2 * LANES, blk), BF16), pltpu.VMEM((s, LANES), BF16),
            pltpu.VMEM((s // blk, LANES), F32), pltpu.VMEM((heads_per, blk, 2 * LANES), BF16),
            pltpu.VMEM((heads_per, s // blk, blk, blk), F32), pltpu.VMEM((heads_per, blk, LANES), F32),
            pltpu.VMEM((heads_per, blk, LANES), F32), pltpu.VMEM((heads_per, blk, LANES), F32),
        ],
        compiler_params=_cparams("parallel", "parallel", "arbitrary"),
    )(uq, k_all, v_all, bias_tiles)


def _rnd(a):
    return a.astype(BF16).astype(F32)


DEC_GROUP = 8
PAGES_PER_BLOCK = 2


def _dec_scores_kernel(pt_ref, q_ref, *refs):
    del pt_ref
    k_refs, (bias_ref, s_ref, ks_ref) = refs[:-3], refs[-3:]
    nh = s_ref.shape[2]
    hd = nh * HEAD_DIM
    page = k_refs[0].shape[-1]
    q = (q_ref[0] * HEAD_DIM ** -0.5).astype(BF16)
    ones = jnp.ones((SUBLANES, page), BF16)
    for g in range(s_ref.shape[1]):
        kk = [r[0, 0].reshape(hd, page) for r in k_refs[PAGES_PER_BLOCK * g:PAGES_PER_BLOCK * (g + 1)]]
        s_ref[0, g] = jnp.concatenate([_dot(q, k.astype(BF16)) for k in kk], axis=1) + bias_ref[g]
        hi, lo = _split_bf16(kk[0] + kk[1])
        ks_ref[0, g] = (_dot_nt(ones, hi) + _dot_nt(ones, lo))[0:1]


def _dec_scores(qx, cache_kt, page_table, bias_dec, layer):
    b, nh, hd = qx.shape
    page = cache_kt.shape[4]
    assert MOBA_BLOCK == PAGES_PER_BLOCK * page
    nblk = page_table.shape[1] // PAGES_PER_BLOCK
    grp = math.gcd(DEC_GROUP, nblk)
    n_pg = PAGES_PER_BLOCK * grp
    pg = lambda t: pl.BlockSpec((1, 1, nh, HEAD_DIM, page),
                                lambda bi, j, pt: (pt[bi, n_pg * j + t], layer, 0, 0, 0))
    out = lambda rows, last: pl.BlockSpec((1, grp, rows, last), lambda bi, j, pt: (bi, j, 0, 0))
    return pl.pallas_call(
        _dec_scores_kernel, name="moba_dec_scores",
        grid_spec=pltpu.PrefetchScalarGridSpec(
            num_scalar_prefetch=1,
            grid=(b, nblk // grp),
            in_specs=[pl.BlockSpec((1, nh, hd), lambda bi, j, pt: (bi, 0, 0))]
            + [pg(t) for t in range(n_pg)]
            + [pl.BlockSpec((grp, nh, MOBA_BLOCK), lambda bi, j, pt: (j, 0, 0))],
            out_specs=[out(nh, MOBA_BLOCK), out(1, hd)],
        ),
        out_shape=[jax.ShapeDtypeStruct((b, nblk, nh, MOBA_BLOCK), F32), jax.ShapeDtypeStruct((b, nblk, 1, hd), F32)],
        compiler_params=_cparams("parallel", "arbitrary"),
    )(page_table, qx, *([cache_kt] * n_pg), bias_dec)


def _dec_select_kernel(q_ref, kn_ref, b0_ref, ks_ref, s_ref, p_ref, pown_ref, sel_ref):
    nblk = sel_ref.shape[1]
    blk = MOBA_BLOCK
    q = _rnd(q_ref[...] * HEAD_DIM ** -0.5)
    lane = lax.broadcasted_iota(jnp.int32, sel_ref.shape, 1)
    gate = jnp.zeros(sel_ref.shape, F32)
    for j in range(nblk):
        kmean = _rnd(ks_ref[:, j * HEAD_DIM:(j + 1) * HEAD_DIM] * (1.0 / blk))
        gate = jnp.where(lane == j, jnp.sum(q * kmean, axis=-1, keepdims=True), gate)
    sel = _rank_select(gate, nblk, lane, nblk, axis=1)
    sel_ref[...] = jnp.where(sel, 1.0, 0.0)
    s_own = jnp.sum(q * _rnd(kn_ref[...]), axis=-1, keepdims=True) + b0_ref[:, 0:1]
    m = s_own
    for j in range(nblk):
        mj = jnp.max(s_ref[:, j * blk:(j + 1) * blk], axis=-1, keepdims=True)
        m = jnp.maximum(m, jnp.where(sel[:, j:j + 1], mj, NEG))
    e_own = jnp.exp(s_own - m)
    den = e_own
    for j in range(nblk):
        e = jnp.where(sel[:, j:j + 1], jnp.exp(s_ref[:, j * blk:(j + 1) * blk] - m), 0.0)
        p_ref[:, j * blk:(j + 1) * blk] = e
        den = den + jnp.sum(e, axis=-1, keepdims=True)
    inv = 1.0 / den
    for j in range(nblk):
        p_ref[:, j * blk:(j + 1) * blk] = _rnd(p_ref[:, j * blk:(j + 1) * blk] * inv)
    pown_ref[...] = jnp.broadcast_to(_rnd(e_own * inv), pown_ref.shape)


def _dec_pv_kernel(pt_ref, src_ref, need_ref, p_ref, *refs):
    del pt_ref, src_ref
    v_refs, (pown_ref, vn_ref, o_ref) = refs[:-3], refs[-3:]
    bi = pl.program_id(0)
    j = pl.program_id(1)
    grp = p_ref.shape[1]
    nh = p_ref.shape[2]
    hd = nh * HEAD_DIM
    page = v_refs[0].shape[-1]

    @pl.when(j == 0)
    def _():
        o_ref[0] = pown_ref[0] * _rnd(vn_ref[0])

    for g in range(grp):
        @pl.when(need_ref[bi, j * grp + g] == 1)
        def _(g=g):
            pb = p_ref[0, g].astype(BF16)
            vv = [r[0, 0].reshape(hd, page).astype(BF16) for r in v_refs[PAGES_PER_BLOCK * g:PAGES_PER_BLOCK * (g + 1)]]
            o_all = _dot_nt(pb[:, :page], vv[0]) + _dot_nt(pb[:, page:], vv[1])
            col = lax.broadcasted_iota(jnp.int32, o_all.shape, 1)
            row = lax.broadcasted_iota(jnp.int32, o_all.shape, 0)
            o_ref[0] += jnp.sum(jnp.where(col // HEAD_DIM == row, o_all, 0.0), axis=0, keepdims=True)


def _moba_decode(proj_s, cache_kt, cache_vt, page_table, rel_bias, layer):
    b, n = proj_s.shape
    w = n // 4
    nh = w // HEAD_DIM
    page = cache_kt.shape[4]
    past_len = page_table.shape[1] * page
    assert past_len % MOBA_BLOCK == 0
    nblk = past_len // MOBA_BLOCK
    q = proj_s[:, w:2 * w].reshape(b, nh, HEAD_DIM)
    qx = jnp.einsum('bhd,hg->bhgd', q, jnp.eye(nh, dtype=F32)).reshape(b, nh, w)
    kpos = np.arange(past_len).reshape(nblk, MOBA_BLOCK)
    bias = rel_bias.astype(F32).T[:, _t5_bucket(jnp.asarray(past_len - kpos, jnp.int32))]
    s_p, ks_p = _dec_scores(qx, cache_kt, page_table, jnp.transpose(bias, (1, 0, 2)), layer)
    rows = b * nh
    blk = MOBA_BLOCK
    to_rows = lambda a, last: jnp.transpose(a.reshape(b, nblk, nh, last), (0, 2, 1, 3)).reshape(rows, nblk * last)
    b0 = jnp.broadcast_to(jnp.tile(rel_bias.astype(F32)[_t5_bucket(jnp.zeros((), jnp.int32))], b)[:, None],
                          (rows, LANES))
    f = jax.ShapeDtypeStruct
    p_rows, p_own, sel = pl.pallas_call(
        _dec_select_kernel, name="moba_dec_select",
        out_shape=[f((rows, nblk * blk), F32), f((rows, LANES), F32), f((rows, nblk), F32)],
        compiler_params=pltpu.CompilerParams(vmem_limit_bytes=VMEM_LIMIT),
    )(q.reshape(rows, HEAD_DIM), proj_s[:, 2 * w:3 * w].reshape(rows, HEAD_DIM), b0,
      to_rows(ks_p, HEAD_DIM), to_rows(s_p, blk))

    grp = math.gcd(DEC_GROUP, nblk)
    need = (jnp.max(sel.reshape(b, nh, nblk), axis=1) > 0.5).reshape(b, nblk // grp, grp)
    blk_id = jnp.arange(nblk, dtype=jnp.int32).reshape(1, nblk // grp, grp)
    last_needed = lax.cummax(jnp.where(need, blk_id, -1), axis=1)
    first_needed = jnp.min(jnp.where(need, blk_id, nblk), axis=1, keepdims=True)
    fallback = jnp.where(first_needed < nblk, first_needed, blk_id[:, :1])
    src = jnp.where(last_needed < 0, fallback, last_needed).astype(jnp.int32).reshape(b, nblk)
    p_blk = jnp.transpose(p_rows.reshape(b, nh, nblk, blk), (0, 2, 1, 3))
    p_own_x = jnp.repeat(p_own[:, 0].reshape(b, nh), HEAD_DIM, axis=1).reshape(b, 1, w)
    page_spec = lambda g, off: pl.BlockSpec(
        (1, 1, nh, HEAD_DIM, page),
        lambda bi, j, pt, sr, nd: (pt[bi, PAGES_PER_BLOCK * sr[bi, j * grp + g] + off], layer, 0, 0, 0))
    seq_spec = pl.BlockSpec((1, 1, w), lambda bi, j, pt, sr, nd: (bi, 0, 0))
    att = pl.pallas_call(
        _dec_pv_kernel, name="moba_dec_pv",
        grid_spec=pltpu.PrefetchScalarGridSpec(
            num_scalar_prefetch=3,
            grid=(b, nblk // grp),
            in_specs=[pl.BlockSpec((1, grp, nh, blk), lambda bi, j, pt, sr, nd: (bi, j, 0, 0))]
            + [page_spec(g, off) for g in range(grp) for off in range(PAGES_PER_BLOCK)]
            + [seq_spec, seq_spec],
            out_specs=seq_spec,
        ),
        out_shape=f((b, 1, w), F32),
        compiler_params=_cparams("parallel", "arbitrary"),
    )(page_table, src, need.reshape(b, nblk).astype(jnp.int32), p_blk,
      *([cache_vt] * (grp * PAGES_PER_BLOCK)), p_own_x, proj_s[:, 3 * w:].reshape(b, 1, w))
    return att.reshape(b, w).astype(BF16)


def _tile(m, pref):
    t = min(m, pref)
    while m % t:
        t //= 2
    return t


def _ff_tile(ff, pref):
    best = LANES
    for t in range(LANES, pref + 1, LANES):
        if ff % t == 0:
            best = t
    return best


def kernel(x_prompt, x_sample, cache_k, cache_v, page_table, state_s5_re, state_s5_im, state_pool, rel_bias,
           norm_mix_e, w_in_e, q_norm_e, k_norm_e, s5_a_re, s5_a_im, s5_log_dt, s5_b_re, s5_b_im, s5_c_re,
           s5_c_im, s5_d, s5_w_glu, s5_b_glu, w_out_e, norm_ffn_e, ffn_w_gate, ffn_w_up, ffn_w_down,
           norm_mix_o, pool_w, pool_scale, norm_ffn_o, router_w, router_b, moe_w_gate, moe_w_up, moe_w_down):
    bp, seq, d = x_prompt.shape
    db = x_sample.shape[0]
    assert x_sample.shape[1] == 1
    depth = norm_mix_e.shape[0] + norm_mix_o.shape[0]
    wdt = w_in_e.shape[2] // 4
    nh = wdt // HEAD_DIM
    past_len = page_table.shape[1] * cache_k.shape[2]
    n_exp = router_w.shape[2]
    gn = s5_a_re.shape[1] * s5_a_re.shape[2]

    mp = bp * seq
    xp = x_prompt.reshape(mp, d)
    xs = x_sample.reshape(db, d)
    tm_p = _tile(mp, 512)
    bias_tiles = _prompt_bias_tiles(rel_bias)
    row = lambda v: v.reshape(1, -1).astype(F32)
    cache_kt = jnp.transpose(cache_k, (0, 1, 3, 4, 2))
    cache_vt = jnp.transpose(cache_v, (0, 1, 3, 4, 2))

    n_even = norm_mix_e.shape[0]
    kp_all = jnp.zeros((mp, n_even * wdt), F32)
    vp_all = jnp.zeros((mp, n_even * wdt), F32)
    ks_all = jnp.zeros((db, n_even * wdt), F32)
    vs_all = jnp.zeros((db, n_even * wdt), F32)
    moe_wg, moe_wu, moe_wd = moe_w_gate.astype(BF16), moe_w_up.astype(BF16), moe_w_down.astype(BF16)
    s5p_l, s5s_re_l, s5s_im_l = [], [], []
    poolp_l, pools_l = [], []
    for layer in range(depth):
        if layer % 2 == 0:
            e = layer // 2
            w_in = w_in_e[e].astype(BF16)
            ones = jnp.ones((wdt,), F32)
            head_gain = jnp.stack([ones, jnp.tile(q_norm_e[e].astype(F32), nh),
                                   jnp.tile(k_norm_e[e].astype(F32), nh), ones]).reshape(4, 1, wdt)
            wb, tabs, lam, wc = _s5_weights(s5_a_re[e], s5_a_im[e], s5_log_dt[e], s5_b_re[e], s5_b_im[e],
                                            s5_c_re[e], s5_c_im[e])
            dsk = row(s5_d[e])
            wgl = s5_w_glu[e].astype(BF16)
            bgl = row(s5_b_glu[e])
            w_out = w_out_e[e].astype(BF16)
            wg, wu, wd = ffn_w_gate[e].astype(BF16), ffn_w_up[e].astype(BF16), ffn_w_down[e].astype(BF16)
            tf = _ff_tile(wg.shape[1], 1536)

            uq, kp_all, vp_all = _in_proj(xp, row(norm_mix_e[e]), w_in, head_gain, kp_all, vp_all, e, tm_p)
            uq3 = uq.reshape(bp, seq, 2 * wdt)
            s5_out, hfin = _s5_scan(uq3, jnp.zeros((bp, 1, 2 * gn), F32), wb, tabs, wc, dsk, wgl, bgl,
                                    _tile(seq, 256))
            att = _moba_prompt(uq3, kp_all.reshape(bp, seq, -1), vp_all.reshape(bp, seq, -1), e, bias_tiles)
            xp = _mix_ffn(xp, s5_out.reshape(mp, wdt), att.reshape(mp, wdt), w_out, row(norm_ffn_e[e]),
                          wg, wu, wd, tm_p, tf)
            s5p_l.append(hfin.reshape(bp, 2, -1, S5_STATE))

            uq_s, ks_all, vs_all = _in_proj(xs, row(norm_mix_e[e]), w_in, head_gain, ks_all, vs_all, e, db)
            proj_s = jnp.concatenate([uq_s, ks_all[:, e * wdt:(e + 1) * wdt], vs_all[:, e * wdt:(e + 1) * wdt]],
                                     axis=1)
            s5_out_s, hr_s, hi_s = _s5_step(proj_s, state_s5_re[:, e].reshape(db, gn).astype(F32),
                                            state_s5_im[:, e].reshape(db, gn).astype(F32),
                                            wb, lam, wc, dsk, wgl, bgl)
            att_s = _moba_decode(proj_s, cache_kt, cache_vt, page_table, rel_bias, e)
            xs = _mix_ffn(xs, s5_out_s, att_s, w_out, row(norm_ffn_e[e]), wg, wu, wd, db, tf)
            s5s_re_l.append(hr_s.reshape(db, -1, S5_STATE))
            s5s_im_l.append(hi_s.reshape(db, -1, S5_STATE))
        else:
            o = layer // 2
            wp = pool_w[o].astype(BF16)
            sc = row(pool_scale[o])
            wr = jnp.zeros((d, LANES), BF16).at[:, :n_exp].set(router_w[o].astype(BF16))
            br = jnp.zeros((1, LANES), F32).at[:, :n_exp].set(router_b[o].astype(F32))
            tf = _ff_tile(moe_wg.shape[3], 1792)

            xp3, hist_p = _pool_prompt(xp.reshape(bp, seq, d), row(norm_mix_o[o]), wp, sc, _tile(seq, 512))
            xp = _moe_sparse(xp3.reshape(mp, d), row(norm_ffn_o[o]), wr, br, moe_wg, moe_wu, moe_wd, o, tm_p, tf // 2)
            poolp_l.append(hist_p[:, HALO - POOL_HIST:])

            hist_s = state_pool[:, o].astype(F32)
            xs, hn_s = _pool_step(xs, row(norm_mix_o[o]), jnp.transpose(hist_s, (1, 0, 2)), wp, sc,
                                  min(POOL_HIST, past_len))
            xs = _moe_dense(xs, row(norm_ffn_o[o]), wr, br, moe_wg, moe_wu, moe_wd, o, db, tf)
            pools_l.append(jnp.concatenate([hist_s[:, 1:], hn_s[:, None]], axis=1))

    s5p = jnp.stack(s5p_l, axis=1)

    def per_head(a, lead):
        parts = [a[:, l * wdt:(l + 1) * wdt].reshape(lead + (nh, HEAD_DIM)) for l in range(n_even)]
        return jnp.stack(parts, axis=len(lead))

    return (xp.reshape(bp, seq, d), xs.reshape(db, 1, d),
            per_head(kp_all, (bp, seq)), per_head(vp_all, (bp, seq)),
            per_head(ks_all, (db, 1)), per_head(vs_all, (db, 1)),
            s5p[:, :, 0], s5p[:, :, 1],
            jnp.stack(s5s_re_l, axis=1), jnp.stack(s5s_im_l, axis=1),
            jnp.stack(poolp_l, axis=1), jnp.stack(pools_l, axis=1))
```

```python
import functools
import math

import jax
import jax.numpy as jnp
import numpy as np
from jax import lax
from jax.experimental import pallas as pl
from jax.experimental.pallas import tpu as pltpu

F32 = jnp.float32
BF16 = jnp.bfloat16
EPS = 1e-6
NEG = -1e30

S5_STATE = 64
HEAD_DIM = 64
MOBA_BLOCK = 256
MOBA_TOPK = 3
REL_BUCKETS = 32
REL_MAX_DIST = 1024
POOL_WINDOWS = (2, 4, 8, 16)
POOL_HIST = max(POOL_WINDOWS) - 1
TOP_K_EXPERTS = 2

LANES = 128
SUBLANES = 8
VMEM_LIMIT = 48 * 1024 * 1024

TOKEN_TILE = 512
SCAN_TILE = 256
FFN_TILE = 1536
EXPERT_TILE = 1792


def _cparams(*sem):
    return pltpu.CompilerParams(dimension_semantics=sem, vmem_limit_bytes=VMEM_LIMIT)


def _rmsnorm(x, g):
    ms = jnp.mean(x * x, axis=-1, keepdims=True)
    return x * lax.rsqrt(ms + EPS) * g


def _split_bf16(x):
    hi = x.astype(BF16)
    lo = (x - hi.astype(F32)).astype(BF16)
    return hi, lo


def _dot(a, b):
    return jnp.dot(a, b, preferred_element_type=F32)


def _dot_nt(a, b):
    return lax.dot_general(a, b, (((1,), (1,)), ((), ())), preferred_element_type=F32)


def _silu(x):
    return x * jax.nn.sigmoid(x)


def _in_proj_kernel(x_ref, g_ref, w_ref, hn_ref, bd_ref, k_in, v_in, uq_ref, k_ref, v_ref, h_scr):
    del k_in, v_in
    j = pl.program_id(1)

    @pl.when(j == 0)
    def _():
        h_scr[...] = _rmsnorm(x_ref[...], g_ref[...]).astype(BF16)

    y = _dot(h_scr[...], w_ref[...])

    def head_normed():
        hi, lo = _split_bf16(y * y)
        ms = _dot(hi, bd_ref[...]) + _dot(lo, bd_ref[...])
        return y * lax.rsqrt(ms + EPS) * hn_ref[0]

    @pl.when(j == 0)
    def _():
        uq_ref[...] = y

    @pl.when(j == 1)
    def _():
        uq_ref[...] = head_normed()

    @pl.when(j == 2)
    def _():
        k_ref[...] = head_normed()

    @pl.when(j == 3)
    def _():
        v_ref[...] = y


def _in_proj(x, g, w_bf, head_gain, k_all, v_all, layer, tm):
    m, d = x.shape
    wdt = w_bf.shape[1] // 4
    bd = np.kron(np.eye(wdt // HEAD_DIM), np.full((HEAD_DIM, HEAD_DIM), 1.0 / HEAD_DIM))
    bd = jnp.asarray(bd, BF16)
    kv_spec = pl.BlockSpec((tm, wdt), lambda i, j: (i, layer))
    return pl.pallas_call(
        _in_proj_kernel, name="in_proj",
        grid=(m // tm, 4),
        in_specs=[
            pl.BlockSpec((tm, d), lambda i, j: (i, 0)),
            pl.BlockSpec((1, d), lambda i, j: (0, 0)),
            pl.BlockSpec((d, wdt), lambda i, j: (0, j)),
            pl.BlockSpec((1, 1, wdt), lambda i, j: (j, 0, 0)),
            pl.BlockSpec((wdt, wdt), lambda i, j: (0, 0)),
            pl.BlockSpec(memory_space=pl.ANY),
            pl.BlockSpec(memory_space=pl.ANY),
        ],
        out_specs=[pl.BlockSpec((tm, wdt), lambda i, j: (i, jnp.minimum(j, 1))), kv_spec, kv_spec],
        out_shape=[jax.ShapeDtypeStruct((m, 2 * wdt), F32), jax.ShapeDtypeStruct(k_all.shape, F32),
                   jax.ShapeDtypeStruct(v_all.shape, F32)],
        input_output_aliases={5: 1, 6: 2},
        scratch_shapes=[pltpu.VMEM((tm, d), BF16)],
        compiler_params=_cparams("parallel", "arbitrary"),
    )(x, g, w_bf, head_gain, bd, k_all, v_all)


def _mix_ffn_kernel(x_ref, a_ref, b_ref, wo_ref, g_ref, wg_ref, wu_ref, wd_ref, o_ref,
                    x1_scr, h_scr, acc_scr):
    f = pl.program_id(1)
    half = a_ref.shape[1]

    @pl.when(f == 0)
    def _():
        x1 = (x_ref[...] + _dot(a_ref[...], wo_ref[:half, :]) + _dot(b_ref[...], wo_ref[half:, :]))
        x1_scr[...] = x1
        h_scr[...] = _rmsnorm(x1, g_ref[...]).astype(BF16)
        acc_scr[...] = jnp.zeros_like(acc_scr)

    h = h_scr[...]
    act = (_silu(_dot(h, wg_ref[...])) * _dot(h, wu_ref[...])).astype(BF16)
    acc_scr[...] += _dot(act, wd_ref[...])

    @pl.when(f == pl.num_programs(1) - 1)
    def _():
        o_ref[...] = x1_scr[...] + acc_scr[...]


def _mix_ffn(x, a, b, wo_bf, g, wg_bf, wu_bf, wd_bf, tm, tf):
    m, d = x.shape
    half = a.shape[1]
    ff = wg_bf.shape[1]
    return pl.pallas_call(
        _mix_ffn_kernel, name="mix_ffn",
        grid=(m // tm, ff // tf),
        in_specs=[
            pl.BlockSpec((tm, d), lambda i, f: (i, 0)),
            pl.BlockSpec((tm, half), lambda i, f: (i, 0)),
            pl.BlockSpec((tm, half), lambda i, f: (i, 0)),
            pl.BlockSpec((2 * half, d), lambda i, f: (0, 0)),
            pl.BlockSpec((1, d), lambda i, f: (0, 0)),
            pl.BlockSpec((d, tf), lambda i, f: (0, f)),
            pl.BlockSpec((d, tf), lambda i, f: (0, f)),
            pl.BlockSpec((tf, d), lambda i, f: (f, 0)),
        ],
        out_specs=pl.BlockSpec((tm, d), lambda i, f: (i, 0)),
        out_shape=jax.ShapeDtypeStruct((m, d), F32),
        scratch_shapes=[pltpu.VMEM((tm, d), F32), pltpu.VMEM((tm, d), BF16), pltpu.VMEM((tm, d), F32)],
        compiler_params=_cparams("parallel", "arbitrary"),
    )(x, a, b, wo_bf, g, wg_bf, wu_bf, wd_bf)


def _route(h, wr, br, n_exp):
    logits = _dot(h.astype(BF16), wr) + br
    lane = lax.broadcasted_iota(jnp.int32, logits.shape, 1)
    logits = jnp.where(lane < n_exp, logits, NEG)
    m1 = jnp.max(logits, axis=-1, keepdims=True)
    i1 = jnp.min(jnp.where(logits == m1, lane, LANES), axis=-1, keepdims=True)
    rest = jnp.where(lane == i1, NEG, logits)
    m2 = jnp.max(rest, axis=-1, keepdims=True)
    i2 = jnp.min(jnp.where(rest == m2, lane, LANES), axis=-1, keepdims=True)
    e2 = jnp.exp(m2 - m1)
    g1 = 1.0 / (1.0 + e2)
    g2 = e2 / (1.0 + e2)
    gate = jnp.where(lane == i1, g1, 0.0) + jnp.where(lane == i2, g2, 0.0)
    return gate, jnp.logical_or(lane == i1, lane == i2)


def _moe_dense_kernel(x_ref, g_ref, wr_ref, br_ref, wg_ref, wu_ref, wd_ref, o_ref,
                      h_scr, gate_scr, acc_scr, *, n_exp):
    e = pl.program_id(1)
    f = pl.program_id(2)

    @pl.when(jnp.logical_and(e == 0, f == 0))
    def _():
        h = _rmsnorm(x_ref[...], g_ref[...])
        h_scr[...] = h.astype(BF16)
        gate_scr[...] = _route(h, wr_ref[...], br_ref[...], n_exp)[0]
        acc_scr[...] = jnp.zeros_like(acc_scr)

    h = h_scr[...]
    lane = lax.broadcasted_iota(jnp.int32, gate_scr.shape, 1)
    w_e = jnp.sum(jnp.where(lane == e, gate_scr[...], 0.0), axis=-1, keepdims=True)
    act = (_silu(_dot(h, wg_ref[0, 0])) * _dot(h, wu_ref[0, 0])).astype(BF16)
    acc_scr[...] += w_e * _dot(act, wd_ref[0, 0])

    @pl.when(jnp.logical_and(e == n_exp - 1, f == pl.num_programs(2) - 1))
    def _():
        o_ref[...] = x_ref[...] + acc_scr[...]


def _moe_dense(x, g, wr_pad, br_pad, wg_bf, wu_bf, wd_bf, layer, tm, tf):
    m, d = x.shape
    _, n_exp, _, ff = wg_bf.shape
    return pl.pallas_call(
        functools.partial(_moe_dense_kernel, n_exp=n_exp), name="moe_dense",
        grid=(m // tm, n_exp, ff // tf),
        in_specs=[
            pl.BlockSpec((tm, d), lambda i, e, f: (i, 0)),
            pl.BlockSpec((1, d), lambda i, e, f: (0, 0)),
            pl.BlockSpec((d, LANES), lambda i, e, f: (0, 0)),
            pl.BlockSpec((1, LANES), lambda i, e, f: (0, 0)),
            pl.BlockSpec((1, 1, d, tf), lambda i, e, f: (layer, e, 0, f)),
            pl.BlockSpec((1, 1, d, tf), lambda i, e, f: (layer, e, 0, f)),
            pl.BlockSpec((1, 1, tf, d), lambda i, e, f: (layer, e, f, 0)),
        ],
        out_specs=pl.BlockSpec((tm, d), lambda i, e, f: (i, 0)),
        out_shape=jax.ShapeDtypeStruct((m, d), F32),
        scratch_shapes=[pltpu.VMEM((tm, d), BF16), pltpu.VMEM((tm, LANES), F32), pltpu.VMEM((tm, d), F32)],
        compiler_params=_cparams("parallel", "arbitrary", "arbitrary"),
    )(x, g, wr_pad, br_pad, wg_bf, wu_bf, wd_bf)


def _moe_route_kernel(x_ref, g_ref, wr_ref, br_ref, h_ref, gate_ref, pos_ref, post_ref, cb_ref, tot_ref, carry,
                      *, n_exp):
    i = pl.program_id(0)
    tm = x_ref.shape[0]

    @pl.when(i == 0)
    def _():
        carry[...] = jnp.zeros_like(carry)

    h = _rmsnorm(x_ref[...], g_ref[...])
    h_ref[...] = h.astype(BF16)
    gate, chosen = _route(h, wr_ref[...], br_ref[...], n_exp)
    gate_ref[...] = gate
    onehot = jnp.where(chosen, 1.0, 0.0)
    r = lax.broadcasted_iota(jnp.int32, (tm, tm), 0)
    c = lax.broadcasted_iota(jnp.int32, (tm, tm), 1)
    before = _dot(jnp.where(c < r, 1.0, 0.0).astype(BF16), onehot.astype(BF16))
    start = carry[...]
    pos = jnp.where(chosen, before + start, -1.0)
    pos_ref[...] = pos
    post_ref[...] = pos.T[:post_ref.shape[0], :]
    cb_ref[0] = start
    total = start + jnp.sum(onehot, axis=0, keepdims=True)
    carry[...] = total
    tot_ref[...] = total


def _moe_route(x, g, wr_bf, br, n_exp, tm):
    m, d = x.shape
    nt = m // tm
    ne8 = -(-n_exp // SUBLANES) * SUBLANES
    f = jax.ShapeDtypeStruct
    return pl.pallas_call(
        functools.partial(_moe_route_kernel, n_exp=n_exp), name="moe_route",
        grid=(nt,),
        in_specs=[
            pl.BlockSpec((tm, d), lambda i: (i, 0)),
            pl.BlockSpec((1, d), lambda i: (0, 0)),
            pl.BlockSpec((d, LANES), lambda i: (0, 0)),
            pl.BlockSpec((1, LANES), lambda i: (0, 0)),
        ],
        out_specs=[
            pl.BlockSpec((tm, d), lambda i: (i, 0)),
            pl.BlockSpec((tm, LANES), lambda i: (i, 0)),
            pl.BlockSpec((tm, LANES), lambda i: (i, 0)),
            pl.BlockSpec((ne8, tm), lambda i: (0, i)),
            pl.BlockSpec((1, 1, LANES), lambda i: (i, 0, 0)),
            pl.BlockSpec((1, LANES), lambda i: (0, 0)),
        ],
        out_shape=[f((m, d), BF16), f((m, LANES), F32), f((m, LANES), F32), f((ne8, m), F32),
                   f((nt, 1, LANES), F32), f((1, LANES), F32)],
        scratch_shapes=[pltpu.VMEM((1, LANES), F32)],
        compiler_params=_cparams("arbitrary"),
    )(x, g, wr_bf, br)


def _moe_gather_kernel(pc_ref, ps_ref, pe_ref, pfirst_ref, pvalid_ref, off_ref, h_ref, post_ref, o_ref, acc):
    k = pl.program_id(0)
    rc, tm = o_ref.shape[0], h_ref.shape[0]

    @pl.when(pfirst_ref[k] == 1)
    def _():
        acc[...] = jnp.zeros_like(acc)

    @pl.when(pvalid_ref[k] == 1)
    def _():
        e = pe_ref[k]
        dest = post_ref[pl.ds(e, 1), :] + off_ref[e].astype(F32)
        rows = (pc_ref[k] * rc + lax.broadcasted_iota(jnp.int32, (rc, tm), 0)).astype(F32)
        acc[...] += _dot(jnp.where(dest == rows, 1.0, 0.0).astype(BF16), h_ref[...])

    o_ref[...] = acc[...].astype(BF16)


def _moe_ffn_kernel(ce_ref, cv_ref, x_ref, wg_ref, wu_ref, wd_ref, y_ref, *, tf):
    c = pl.program_id(0)

    @pl.when(cv_ref[c] == 1)
    def _():
        x = x_ref[...]
        acc = jnp.zeros(y_ref.shape, F32)
        for f0 in range(0, wg_ref.shape[3], tf):
            act = (_silu(_dot(x, wg_ref[0, 0, :, f0:f0 + tf])) * _dot(x, wu_ref[0, 0, :, f0:f0 + tf])).astype(BF16)
            acc = acc + _dot(act, wd_ref[0, 0, f0:f0 + tf, :])
        y_ref[...] = acc.astype(BF16)

    @pl.when(cv_ref[c] == 0)
    def _():
        y_ref[...] = jnp.zeros_like(y_ref)


def _moe_combine_kernel(qs_ref, qc_ref, qe_ref, qfirst_ref, qvalid_ref, off_ref, x_ref, y_ref, pos_ref, gate_ref,
                        o_ref):
    k = pl.program_id(0)
    tm, rc = x_ref.shape[0], y_ref.shape[0]

    @pl.when(qfirst_ref[k] == 1)
    def _():
        o_ref[...] = x_ref[...]

    @pl.when(qvalid_ref[k] == 1)
    def _():
        e = qe_ref[k]
        lane = lax.broadcasted_iota(jnp.int32, pos_ref.shape, 1)
        pos_e = jnp.sum(jnp.where(lane == e, pos_ref[...], 0.0), axis=-1, keepdims=True)
        gate_e = jnp.sum(jnp.where(lane == e, gate_ref[...], 0.0), axis=-1, keepdims=True)
        dest = pos_e + off_ref[e].astype(F32)
        cols = (qc_ref[k] * rc + lax.broadcasted_iota(jnp.int32, (tm, rc), 1)).astype(F32)
        pick = jnp.where(dest == cols, 1.0, 0.0).astype(BF16)
        o_ref[...] += gate_e * _dot(pick, y_ref[...])


def _pair_list(inter, n_pairs_max):
    nb = inter.shape[1]
    flat = inter.reshape(-1)
    n = jnp.sum(flat.astype(jnp.int32))
    idx = jnp.nonzero(flat, size=n_pairs_max, fill_value=0)[0].astype(jnp.int32)
    k = jnp.arange(n_pairs_max, dtype=jnp.int32)
    idx = jnp.where(k < n, idx, idx[jnp.maximum(n - 1, 0)])
    a, b = idx // nb, idx % nb
    valid = k < n
    first = jnp.logical_and(valid, jnp.logical_or(k == 0, a != jnp.roll(a, 1)))
    return a, b, valid.astype(jnp.int32), first.astype(jnp.int32)


ROW_CHUNK = 256


def _moe_sparse(x, g, wr_bf, br, wg_bf, wu_bf, wd_bf, layer, tm, tf):
    m, d = x.shape
    _, n_exp, _, ff = wg_bf.shape
    nt = m // tm
    rc = ROW_CHUNK
    h, gate, pos, pos_t, cb, tot = _moe_route(x, g, wr_bf, br, n_exp, tm)

    cnt = tot[0, :n_exp].astype(jnp.int32)
    cbi = jnp.concatenate([cb[:, 0, :n_exp], tot[:, :n_exp]], axis=0).astype(jnp.int32)
    gsz = (cnt + rc - 1) // rc * rc
    ends = jnp.cumsum(gsz)
    off = (ends - gsz).astype(jnp.int32)
    n_chunks = TOP_K_EXPERTS * m // rc + n_exp
    cstart = jnp.arange(n_chunks, dtype=jnp.int32) * rc
    ce = jnp.minimum(jnp.searchsorted(ends, cstart, side='right'), n_exp - 1).astype(jnp.int32)
    cv = cstart < ends[-1]
    per_tile = cbi.T[ce]
    start = off[ce][:, None] + per_tile[:, :-1]
    end = off[ce][:, None] + per_tile[:, 1:]
    inter = (cv[:, None] & (end > start) & (start < cstart[:, None] + rc) & (end > cstart[:, None]))
    n_pairs = n_chunks + n_exp * nt
    unused = jnp.logical_and(jnp.logical_not(cv)[:, None], jnp.arange(nt)[None, :] == 0)
    pc, ps, plisted, pfirst = _pair_list(jnp.logical_or(inter, unused), n_pairs + n_exp)
    pvalid = plisted * inter[pc, ps].astype(jnp.int32)
    qs, qc, qvalid, qfirst = _pair_list(inter.T, n_pairs)

    xs = pl.pallas_call(
        _moe_gather_kernel, name="moe_gather",
        grid_spec=pltpu.PrefetchScalarGridSpec(
            num_scalar_prefetch=6,
            grid=(n_pairs + n_exp,),
            in_specs=[
                pl.BlockSpec((tm, d), lambda k, pc, ps, *_: (ps[k], 0)),
                pl.BlockSpec((pos_t.shape[0], tm), lambda k, pc, ps, *_: (0, ps[k])),
            ],
            out_specs=pl.BlockSpec((rc, d), lambda k, pc, *_: (pc[k], 0)),
            scratch_shapes=[pltpu.VMEM((rc, d), F32)],
        ),
        out_shape=jax.ShapeDtypeStruct((n_chunks * rc, d), BF16),
        compiler_params=_cparams("arbitrary"),
    )(pc, ps, ce[pc], pfirst, pvalid, off, h, pos_t)

    cvi = cv.astype(jnp.int32)
    wspec = lambda shape: pl.BlockSpec((1,) + shape, lambda c, ce, cv: (layer, ce[c], 0, 0),
                                       pipeline_mode=pl.Buffered(1))
    ys = pl.pallas_call(
        functools.partial(_moe_ffn_kernel, tf=tf), name="moe_ffn",
        grid_spec=pltpu.PrefetchScalarGridSpec(
            num_scalar_prefetch=2,
            grid=(n_chunks,),
            in_specs=[
                pl.BlockSpec((rc, d), lambda c, ce, cv: (c * cv[c], 0)),
                wspec((1, d, ff)), wspec((1, d, ff)), wspec((1, ff, d)),
            ],
            out_specs=pl.BlockSpec((rc, d), lambda c, ce, cv: (c, 0)),
        ),
        out_shape=jax.ShapeDtypeStruct((n_chunks * rc, d), BF16),
        compiler_params=_cparams("arbitrary"),
    )(ce, cvi, xs, wg_bf, wu_bf, wd_bf)

    return pl.pallas_call(
        _moe_combine_kernel, name="moe_combine",
        grid_spec=pltpu.PrefetchScalarGridSpec(
            num_scalar_prefetch=6,
            grid=(n_pairs,),
            in_specs=[
                pl.BlockSpec((tm, d), lambda k, qs, qc, *_: (qs[k], 0)),
                pl.BlockSpec((rc, d), lambda k, qs, qc, *_: (qc[k], 0)),
                pl.BlockSpec((tm, LANES), lambda k, qs, *_: (qs[k], 0)),
                pl.BlockSpec((tm, LANES), lambda k, qs, *_: (qs[k], 0)),
            ],
            out_specs=pl.BlockSpec((tm, d), lambda k, qs, *_: (qs[k], 0)),
        ),
        out_shape=jax.ShapeDtypeStruct((m, d), F32),
        compiler_params=_cparams("arbitrary"),
    )(qs, qc, ce[qc], qfirst, qvalid, off, x, ys, pos, gate)


HALO = 16


def _pool_kernel(x_ref, g_ref, wp_ref, sc_ref, o_ref, hist_ref, hbuf):
    t = pl.program_id(1)
    tm = x_ref.shape[1]
    gw = wp_ref.shape[1]

    @pl.when(t == 0)
    def _():
        hbuf[0:HALO, :] = jnp.zeros((HALO, hbuf.shape[1]), F32)

    @pl.when(t > 0)
    def _():
        hbuf[0:HALO, :] = hbuf[tm:tm + HALO, :]

    x = x_ref[0]
    hn = _rmsnorm(x, g_ref[...])
    hbuf[HALO:HALO + tm, :] = hn
    pos = t * tm + lax.broadcasted_iota(jnp.int32, (tm, 1), 0)
    ys = []
    for gi, w in enumerate(POOL_WINDOWS):
        sl = slice(gi * gw, (gi + 1) * gw)
        tot = hn[:, sl]
        for i in range(1, w):
            tot = tot + hbuf[HALO - i:HALO - i + tm, sl]
        cnt = jnp.minimum(pos + 1, w).astype(F32)
        pooled = tot / cnt - hn[:, sl]
        ys.append(_dot(pooled.astype(BF16), wp_ref[gi]))
    o_ref[0] = x + jnp.concatenate(ys, axis=1) * sc_ref[...]

    @pl.when(t == pl.num_programs(1) - 1)
    def _():
        hist_ref[0] = hbuf[tm:tm + HALO, :]


def _pool_prompt(x, g, wp_bf, sc, tm):
    b, s, d = x.shape
    ng, gw, _ = wp_bf.shape
    return pl.pallas_call(
        _pool_kernel, name="pool_prompt",
        grid=(b, s // tm),
        in_specs=[
            pl.BlockSpec((1, tm, d), lambda i, t: (i, t, 0)),
            pl.BlockSpec((1, d), lambda i, t: (0, 0)),
            pl.BlockSpec((ng, gw, gw), lambda i, t: (0, 0, 0)),
            pl.BlockSpec((1, d), lambda i, t: (0, 0)),
        ],
        out_specs=[
            pl.BlockSpec((1, tm, d), lambda i, t: (i, t, 0)),
            pl.BlockSpec((1, HALO, d), lambda i, t: (i, 0, 0)),
        ],
        out_shape=[jax.ShapeDtypeStruct((b, s, d), F32), jax.ShapeDtypeStruct((b, HALO, d), F32)],
        scratch_shapes=[pltpu.VMEM((HALO + tm, d), F32)],
        compiler_params=_cparams("parallel", "arbitrary"),
    )(x, g, wp_bf, sc)


def _pool_step_kernel(x_ref, g_ref, hist_ref, wp_ref, sc_ref, o_ref, hn_ref, *, n_valid):
    gw = wp_ref.shape[1]
    x = x_ref[...]
    hn = _rmsnorm(x, g_ref[...])
    hn_ref[...] = hn
    ys = []
    for gi, w in enumerate(POOL_WINDOWS):
        sl = slice(gi * gw, (gi + 1) * gw)
        tot = hn[:, sl]
        for i in range(1, min(w, n_valid + 1)):
            tot = tot + hist_ref[POOL_HIST - i][:, sl]
        pooled = tot / float(min(w, n_valid + 1)) - hn[:, sl]
        ys.append(_dot(pooled.astype(BF16), wp_ref[gi]))
    o_ref[...] = x + jnp.concatenate(ys, axis=1) * sc_ref[...]


def _pool_step(x, g, hist_t, wp_bf, sc, n_valid):
    b, d = x.shape
    return pl.pallas_call(
        functools.partial(_pool_step_kernel, n_valid=n_valid), name="pool_step",
        out_shape=[jax.ShapeDtypeStruct((b, d), F32), jax.ShapeDtypeStruct((b, d), F32)],
        compiler_params=pltpu.CompilerParams(vmem_limit_bytes=VMEM_LIMIT),
    )(x, g, hist_t, wp_bf, sc)


N_POW = SUBLANES


def _s5_prep_kernel(ar_ref, ai_ref, ldt_ref, br_ref, bi_ref, pwr_ref, pwi_ref, bbr_ref, bbi_ref):
    ar = ar_ref[...]
    ai = ai_ref[...]
    dt = jnp.exp(ldt_ref[...])
    mag = jnp.exp(ar * dt)
    lr = mag * jnp.cos(ai * dt)
    li = mag * jnp.sin(ai * dt)
    den = ar * ar + ai * ai
    rr = ((lr - 1.0) * ar + li * ai) / den
    ri = (li * ar - (lr - 1.0) * ai) / den
    br = br_ref[...]
    bi = bi_ref[...]
    bbr_ref[...] = rr * br - ri * bi
    bbi_ref[...] = rr * bi + ri * br
    cr, ci = lr, li
    for k in range(N_POW):
        pwr_ref[k] = cr
        pwi_ref[k] = ci
        cr, ci = cr * lr - ci * li, cr * li + ci * lr


def _s5_prep(a_re, a_im, log_dt, b_re, b_im):
    g, n = a_re.shape
    p = b_re.shape[2]
    f = jax.ShapeDtypeStruct
    return pl.pallas_call(
        _s5_prep_kernel, name="s5_prep",
        out_shape=[f((N_POW, g, 1, n), F32), f((N_POW, g, 1, n), F32), f((g, p, n), F32), f((g, p, n), F32)],
    )(a_re[:, None, :], a_im[:, None, :], log_dt[:, None, None],
      jnp.transpose(b_re, (0, 2, 1)), jnp.transpose(b_im, (0, 2, 1)))


def _s5_weights(a_re, a_im, log_dt, b_re, b_im, c_re, c_im):
    g, n = a_re.shape
    p = b_re.shape[2]
    gn = g * n
    pwr, pwi, bbr, bbi = _s5_prep(a_re, a_im, log_dt, b_re, b_im)
    pwr = pwr.reshape(N_POW, gn)
    pwi = pwi.reshape(N_POW, gn)
    row = np.arange(SUBLANES)[:, None]
    tabs = []
    for k in (1, 2, 4):
        tabs += [jnp.where(row >= k, pwr[k - 1][None], 0.0), jnp.where(row >= k, pwi[k - 1][None], 0.0)]
    tabs += [pwr, pwi]
    tabs = jnp.stack(tabs)
    lam = jnp.stack([pwr[0], pwi[0]])
    n_tiles = gn // LANES
    g_tile = LANES // n
    g_chunk = LANES // p
    place = np.zeros((n_tiles, g_chunk, g_tile), np.float32)
    for j in range(n_tiles):
        for i in range(g_tile):
            place[j, (j * g_tile + i) % g_chunk, i] = 1.0
    bb = jnp.stack([bbr, bbi]).reshape(2, n_tiles, g_tile, p, n)
    wb = jnp.einsum('cjipn,jli->jlpcin', bb, place).reshape(n_tiles, LANES, 2 * LANES).astype(BF16)
    n_out = g * p // LANES
    cc = jnp.stack([c_re, -c_im]).reshape(2, n_out, g_chunk, p, n)
    wc = jnp.einsum('cmlpn,lk->mclnkp', cc, np.eye(g_chunk, dtype=np.float32))
    wc = wc.reshape(n_out, 2 * g_chunk * n, LANES).astype(BF16)
    return wb, tabs, lam, wc


def _s5_drive(u, wb_ref, bur, bui, row0):
    t = u.shape[0]
    ub = u.astype(BF16)
    n_tiles = wb_ref.shape[0]
    u_per_tile = u.shape[1] // n_tiles
    for j in range(n_tiles):
        c0 = (j * u_per_tile) // LANES * LANES
        r = _dot(ub[:, c0:c0 + LANES], wb_ref[j])
        bur[row0:row0 + t, j * LANES:(j + 1) * LANES] = r[:, :LANES]
        bui[row0:row0 + t, j * LANES:(j + 1) * LANES] = r[:, LANES:]


def _s5_readout(u, hr, hi, wc_ref, d_ref, wgl_ref, bgl_ref):
    hrb = hr.astype(BF16)
    hib = hi.astype(BF16)
    kk = wc_ref.shape[1] // 2
    ys = []
    for m in range(wc_ref.shape[0]):
        ys.append(_dot(hrb[:, m * kk:(m + 1) * kk], wc_ref[m, :kk, :])
                  + _dot(hib[:, m * kk:(m + 1) * kk], wc_ref[m, kk:, :]))
    y = jnp.concatenate(ys, axis=1) + d_ref[...] * u
    z = jax.nn.gelu(y, approximate=True)
    return z * jax.nn.sigmoid(_dot(z.astype(BF16), wgl_ref[...]) + bgl_ref[...])


def _s5_scan_kernel(u_ref, h0_ref, wb_ref, tab_ref, wc_ref, d_ref, wgl_ref, bgl_ref, o_ref, hf_ref, bur, bui):
    t = pl.program_id(1)
    tt = u_ref.shape[1]
    gn = bur.shape[1]
    c0 = SUBLANES

    @pl.when(t == 0)
    def _():
        bur[0:c0, :] = jnp.broadcast_to(h0_ref[0, :, :gn], (c0, gn))
        bui[0:c0, :] = jnp.broadcast_to(h0_ref[0, :, gn:], (c0, gn))

    @pl.when(t > 0)
    def _():
        bur[0:c0, :] = bur[tt:tt + c0, :]
        bui[0:c0, :] = bui[tt:tt + c0, :]

    u = u_ref[0]
    _s5_drive(u, wb_ref, bur, bui, c0)

    def block(b, carry):
        base = pl.multiple_of(b * SUBLANES, SUBLANES)
        cr = bur[pl.ds(base + c0 - 1, 1), :]
        ci = bui[pl.ds(base + c0 - 1, 1), :]
        xr = bur[pl.ds(base + c0, SUBLANES), :]
        xi = bui[pl.ds(base + c0, SUBLANES), :]
        for s, k in enumerate((1, 2, 4)):
            ar = tab_ref[2 * s]
            ai = tab_ref[2 * s + 1]
            sr = pltpu.roll(xr, k, axis=0)
            si = pltpu.roll(xi, k, axis=0)
            xr, xi = xr + ar * sr - ai * si, xi + ar * si + ai * sr
        pr = tab_ref[6]
        pi = tab_ref[7]
        bur[pl.ds(base + c0, SUBLANES), :] = xr + pr * cr - pi * ci
        bui[pl.ds(base + c0, SUBLANES), :] = xi + pr * ci + pi * cr
        return carry

    lax.fori_loop(0, tt // SUBLANES, block, 0)
    o_ref[0] = _s5_readout(u, bur[c0:c0 + tt, :], bui[c0:c0 + tt, :], wc_ref, d_ref, wgl_ref, bgl_ref).astype(BF16)

    @pl.when(t == pl.num_programs(1) - 1)
    def _():
        hf_ref[0] = jnp.concatenate([bur[tt + c0 - 1:tt + c0, :], bui[tt + c0 - 1:tt + c0, :]], axis=1)


def _s5_scan(proj, h0, wb, tabs, wc, dsk, wgl_bf, bgl, tt):
    b, s, _ = proj.shape
    w = dsk.shape[1]
    gn = tabs.shape[2]
    full = lambda a: pl.BlockSpec(a.shape, lambda i, t: (0,) * a.ndim)
    return pl.pallas_call(
        _s5_scan_kernel, name="s5_scan",
        grid=(b, s // tt),
        in_specs=[
            pl.BlockSpec((1, tt, w), lambda i, t: (i, t, 0)),
            pl.BlockSpec((1, 1, 2 * gn), lambda i, t: (i, 0, 0)),
            full(wb), full(tabs), full(wc), full(dsk), full(wgl_bf), full(bgl),
        ],
        out_specs=[
            pl.BlockSpec((1, tt, w), lambda i, t: (i, t, 0)),
            pl.BlockSpec((1, 1, 2 * gn), lambda i, t: (i, 0, 0)),
        ],
        out_shape=[jax.ShapeDtypeStruct((b, s, w), BF16), jax.ShapeDtypeStruct((b, 1, 2 * gn), F32)],
        scratch_shapes=[pltpu.VMEM((tt + SUBLANES, gn), F32), pltpu.VMEM((tt + SUBLANES, gn), F32)],
        compiler_params=_cparams("parallel", "arbitrary"),
    )(proj, h0, wb, tabs, wc, dsk, wgl_bf, bgl)


def _s5_step_kernel(u_ref, h0r_ref, h0i_ref, wb_ref, lam_ref, wc_ref, d_ref, wgl_ref, bgl_ref,
                    o_ref, hr_ref, hi_ref, bur, bui):
    u = u_ref[...]
    _s5_drive(u, wb_ref, bur, bui, 0)
    lr = lam_ref[0:1, :]
    li = lam_ref[1:2, :]
    h0r = h0r_ref[...]
    h0i = h0i_ref[...]
    hr = bur[...] + lr * h0r - li * h0i
    hi = bui[...] + lr * h0i + li * h0r
    hr_ref[...] = hr
    hi_ref[...] = hi
    o_ref[...] = _s5_readout(u, hr, hi, wc_ref, d_ref, wgl_ref, bgl_ref).astype(BF16)


def _s5_step(proj, h0r, h0i, wb, lam, wc, dsk, wgl_bf, bgl):
    b = proj.shape[0]
    w = dsk.shape[1]
    gn = lam.shape[1]
    full = lambda a: pl.BlockSpec(a.shape, lambda i: (0,) * a.ndim)
    return pl.pallas_call(
        _s5_step_kernel, name="s5_step",
        grid=(1,),
        in_specs=[pl.BlockSpec((b, w), lambda i: (0, 0)), full(h0r), full(h0i), full(wb), full(lam), full(wc),
                  full(dsk), full(wgl_bf), full(bgl)],
        out_specs=[pl.BlockSpec((b, w), lambda i: (0, 0)), pl.BlockSpec((b, gn), lambda i: (0, 0)),
                   pl.BlockSpec((b, gn), lambda i: (0, 0))],
        out_shape=[jax.ShapeDtypeStruct((b, w), BF16), jax.ShapeDtypeStruct((b, gn), F32),
                   jax.ShapeDtypeStruct((b, gn), F32)],
        scratch_shapes=[pltpu.VMEM((b, gn), F32), pltpu.VMEM((b, gn), F32)],
        compiler_params=_cparams("arbitrary"),
    )(proj, h0r, h0i, wb, lam, wc, dsk, wgl_bf, bgl)


def _t5_bucket(rel):
    n = jnp.maximum(rel, 0)
    max_exact = REL_BUCKETS // 2
    nf = jnp.maximum(n, 1).astype(F32)
    large = max_exact + (jnp.log(nf / max_exact) / math.log(REL_MAX_DIST / max_exact)
                         * (REL_BUCKETS - max_exact)).astype(jnp.int32)
    large = jnp.minimum(large, REL_BUCKETS - 1)
    return jnp.where(n < max_exact, n, large)


N_DIST = -(-(REL_MAX_DIST + MOBA_BLOCK - 1) // MOBA_BLOCK) + 1


def _prompt_bias_tiles(rel_bias):
    blk = MOBA_BLOCK
    span = 2 * blk - 1
    rel = jnp.asarray(np.arange(-(blk - 1), N_DIST * blk), jnp.int32)
    line = jnp.where(rel >= 0, rel_bias.astype(F32).T[:, _t5_bucket(rel)], NEG)
    diag = jnp.stack([line[:, d * blk:d * blk + span] for d in range(N_DIST)], axis=1)
    line_rc = jnp.roll(diag[..., ::-1], -(blk - 1), axis=-1)
    rep = jnp.tile(line_rc, (1, 1, blk))[..., :blk * (span - 1)]
    return rep.reshape(diag.shape[0], N_DIST, blk, span - 1)[..., :blk]


def _rank_select(g, n_valid, idx, n_cand, axis):
    cnt = jnp.zeros(g.shape, jnp.int32)
    for jp in range(n_cand):
        gp = lax.slice_in_dim(g, jp, jp + 1, axis=axis)
        beats = jnp.logical_or(gp > g, jnp.logical_and(gp == g, jp < idx))
        cnt = cnt + jnp.where(jnp.logical_and(beats, jp < n_valid), 1, 0)
    return jnp.logical_and(idx < n_valid, cnt < MOBA_TOPK)


SWEEP_UNROLL = 4


def _moba_kernel(q_ref, k_ref, v_ref, bias_ref, o_ref, kft, vb, km, qa, sbuf, red, stat, acc_scr):
    i = pl.program_id(2)
    blk = MOBA_BLOCK
    nb = k_ref.shape[1] // blk
    n_hd = LANES // HEAD_DIM
    scale = HEAD_DIM ** -0.5

    @pl.when(i == 0)
    def _():
        feat = lax.broadcasted_iota(jnp.int32, (LANES, blk), 0)
        for j in range(nb):
            kj = k_ref[0, j * blk:(j + 1) * blk, :]
            kft[j, :LANES, :] = kj.T.astype(BF16)
            kft[j, LANES:, :] = jnp.where(feat == j, 1.0, 0.0).astype(BF16)
            km[j:j + 1, :] = jnp.mean(kj, axis=0, keepdims=True)
        vb[...] = v_ref[0].astype(BF16)

    lane = lax.broadcasted_iota(jnp.int32, (blk, LANES), 1)
    wide = lambda a: jnp.concatenate([a, a], axis=1)
    rep = lambda a: jnp.broadcast_to(a, (blk, LANES))
    for hd in range(n_hd):
        qm = (jnp.where(lane // HEAD_DIM == hd, q_ref[0], 0.0) * scale).astype(BF16)
        g = _dot_nt(km[...].astype(BF16), qm)
        jrow = lax.broadcasted_iota(jnp.int32, g.shape, 0)
        keep = jnp.logical_or(_rank_select(g, i, jrow, nb, axis=0), jrow == i)
        pen = jnp.concatenate([jnp.where(keep, 0.0, NEG), jnp.zeros((LANES - nb, blk), F32)], axis=0)
        qa[hd, :, :LANES] = qm
        qa[hd, :, LANES:] = pen.T.astype(BF16)
        red[hd] = jnp.full((blk, LANES), NEG, F32)

    def sweep(step):
        def group(t, carry):
            for u in range(SWEEP_UNROLL):
                step(SWEEP_UNROLL * t + u)
            return carry

        def single(j, carry):
            step(j)
            return carry

        n_grouped = (i + 1) // SWEEP_UNROLL
        lax.fori_loop(0, n_grouped, group, 0)
        lax.fori_loop(n_grouped * SWEEP_UNROLL, i + 1, single, 0)

    def scores(j):
        dist = jnp.minimum(i - j, N_DIST - 1)
        kt = kft[j]
        for hd in range(n_hd):
            s = _dot(qa[hd], kt) + bias_ref[hd, dist]
            sbuf[hd, j] = s
            red[hd] = jnp.maximum(red[hd], jnp.maximum(s[:, :LANES], s[:, LANES:]))

    sweep(scores)
    for hd in range(n_hd):
        stat[hd] = rep(jnp.max(red[hd], axis=-1, keepdims=True))
        red[hd] = jnp.zeros((blk, LANES), F32)

    def exps(j):
        for hd in range(n_hd):
            e = jnp.exp(sbuf[hd, j] - wide(stat[hd]))
            sbuf[hd, j] = e
            red[hd] += e[:, :LANES] + e[:, LANES:]

    sweep(exps)
    for hd in range(n_hd):
        stat[hd] = rep(1.0 / jnp.sum(red[hd], axis=-1, keepdims=True))
        acc_scr[hd] = jnp.zeros((blk, LANES), F32)

    def weighted(j):
        vj = vb[pl.ds(pl.multiple_of(j * blk, blk), blk), :]
        for hd in range(n_hd):
            p = (sbuf[hd, j] * wide(stat[hd])).astype(BF16)
            acc_scr[hd] += _dot(p, vj)

    sweep(weighted)
    o = acc_scr[0]
    for hd in range(1, n_hd):
        o = jnp.where(lane // HEAD_DIM == hd, acc_scr[hd], o)
    o_ref[0] = o.astype(BF16)


def _moba_prompt(uq, k_all, v_all, layer, bias_tiles):
    b, s, n = uq.shape
    w = n // 2
    hp = w // LANES
    heads_per = LANES // HEAD_DIM
    blk = MOBA_BLOCK
    assert s % blk == 0 and s // blk <= LANES
    return pl.pallas_call(
        _moba_kernel, name="moba_prompt",
        grid=(b, hp, s // blk),
        in_specs=[
            pl.BlockSpec((1, blk, LANES), lambda bi, h, i: (bi, i, hp + h)),
            pl.BlockSpec((1, s, LANES), lambda bi, h, i: (bi, 0, layer * hp + h)),
            pl.BlockSpec((1, s, LANES), lambda bi, h, i: (bi, 0, layer * hp + h)),
            pl.BlockSpec((heads_per, N_DIST, blk, blk), lambda bi, h, i: (h, 0, 0, 0)),
        ],
        out_specs=pl.BlockSpec((1, blk, LANES), lambda bi, h, i: (bi, i, h)),
        out_shape=jax.ShapeDtypeStruct((b, s, w), BF16),
        scratch_shapes=[
            pltpu.VMEM((s // blk, 2 * LANES, blk), BF16), pltpu.VMEM((s, LANES), BF16),
            pltpu.VMEM((s // blk, LANES), F32), pltpu.VMEM((heads_per, blk, 2 * LANES), BF16),
            pltpu.VMEM((heads_per, s // blk, blk, blk), F32), pltpu.VMEM((heads_per, blk, LANES), F32),
            pltpu.VMEM((heads_per, blk, LANES), F32), pltpu.VMEM((heads_per, blk, LANES), F32),
        ],
        compiler_params=_cparams("parallel", "parallel", "arbitrary"),
    )(uq, k_all, v_all, bias_tiles)


def _rnd(a):
    return a.astype(BF16).astype(F32)


DEC_GROUP = 8
PAGES_PER_BLOCK = 2


def _dec_scores_kernel(pt_ref, q_ref, *refs):
    del pt_ref
    k_refs, (bias_ref, s_ref, ks_ref) = refs[:-3], refs[-3:]
    nh = s_ref.shape[2]
    hd = nh * HEAD_DIM
    page = k_refs[0].shape[-1]
    q = (q_ref[0] * HEAD_DIM ** -0.5).astype(BF16)
    ones = jnp.ones((SUBLANES, page), BF16)
    for g in range(s_ref.shape[1]):
        kk = [r[0, 0].reshape(hd, page) for r in k_refs[PAGES_PER_BLOCK * g:PAGES_PER_BLOCK * (g + 1)]]
        s_ref[0, g] = jnp.concatenate([_dot(q, k.astype(BF16)) for k in kk], axis=1) + bias_ref[g]
        hi, lo = _split_bf16(kk[0] + kk[1])
        ks_ref[0, g] = (_dot_nt(ones, hi) + _dot_nt(ones, lo))[0:1]


def _dec_scores(qx, cache_kt, page_table, bias_dec, layer):
    b, nh, hd = qx.shape
    page = cache_kt.shape[4]
    assert MOBA_BLOCK == PAGES_PER_BLOCK * page
    nblk = page_table.shape[1] // PAGES_PER_BLOCK
    grp = math.gcd(DEC_GROUP, nblk)
    n_pg = PAGES_PER_BLOCK * grp
    pg = lambda t: pl.BlockSpec((1, 1, nh, HEAD_DIM, page),
                                lambda bi, j, pt: (pt[bi, n_pg * j + t], layer, 0, 0, 0))
    out = lambda rows, last: pl.BlockSpec((1, grp, rows, last), lambda bi, j, pt: (bi, j, 0, 0))
    return pl.pallas_call(
        _dec_scores_kernel, name="moba_dec_scores",
        grid_spec=pltpu.PrefetchScalarGridSpec(
            num_scalar_prefetch=1,
            grid=(b, nblk // grp),
            in_specs=[pl.BlockSpec((1, nh, hd), lambda bi, j, pt: (bi, 0, 0))]
            + [pg(t) for t in range(n_pg)]
            + [pl.BlockSpec((grp, nh, MOBA_BLOCK), lambda bi, j, pt: (j, 0, 0))],
            out_specs=[out(nh, MOBA_BLOCK), out(1, hd)],
        ),
        out_shape=[jax.ShapeDtypeStruct((b, nblk, nh, MOBA_BLOCK), F32), jax.ShapeDtypeStruct((b, nblk, 1, hd), F32)],
        compiler_params=_cparams("parallel", "arbitrary"),
    )(page_table, qx, *([cache_kt] * n_pg), bias_dec)


def _dec_select_kernel(q_ref, kn_ref, b0_ref, ks_ref, s_ref, p_ref, pown_ref, sel_ref):
    nblk = sel_ref.shape[1]
    blk = MOBA_BLOCK
    q = _rnd(q_ref[...] * HEAD_DIM ** -0.5)
    lane = lax.broadcasted_iota(jnp.int32, sel_ref.shape, 1)
    gate = jnp.zeros(sel_ref.shape, F32)
    for j in range(nblk):
        kmean = _rnd(ks_ref[:, j * HEAD_DIM:(j + 1) * HEAD_DIM] * (1.0 / blk))
        gate = jnp.where(lane == j, jnp.sum(q * kmean, axis=-1, keepdims=True), gate)
    sel = _rank_select(gate, nblk, lane, nblk, axis=1)
    sel_ref[...] = jnp.where(sel, 1.0, 0.0)
    s_own = jnp.sum(q * _rnd(kn_ref[...]), axis=-1, keepdims=True) + b0_ref[:, 0:1]
    m = s_own
    for j in range(nblk):
        mj = jnp.max(s_ref[:, j * blk:(j + 1) * blk], axis=-1, keepdims=True)
        m = jnp.maximum(m, jnp.where(sel[:, j:j + 1], mj, NEG))
    e_own = jnp.exp(s_own - m)
    den = e_own
    for j in range(nblk):
        e = jnp.where(sel[:, j:j + 1], jnp.exp(s_ref[:, j * blk:(j + 1) * blk] - m), 0.0)
        p_ref[:, j * blk:(j + 1) * blk] = e
        den = den + jnp.sum(e, axis=-1, keepdims=True)
    inv = 1.0 / den
    for j in range(nblk):
        p_ref[:, j * blk:(j + 1) * blk] = _rnd(p_ref[:, j * blk:(j + 1) * blk] * inv)
    pown_ref[...] = jnp.broadcast_to(_rnd(e_own * inv), pown_ref.shape)


def _dec_pv_kernel(pt_ref, src_ref, need_ref, p_ref, *refs):
    del pt_ref, src_ref
    v_refs, (pown_ref, vn_ref, o_ref) = refs[:-3], refs[-3:]
    bi = pl.program_id(0)
    j = pl.program_id(1)
    grp = p_ref.shape[1]
    nh = p_ref.shape[2]
    hd = nh * HEAD_DIM
    page = v_refs[0].shape[-1]

    @pl.when(j == 0)
    def _():
        o_ref[0] = pown_ref[0] * _rnd(vn_ref[0])

    for g in range(grp):
        @pl.when(need_ref[bi, j * grp + g] == 1)
        def _(g=g):
            pb = p_ref[0, g].astype(BF16)
            vv = [r[0, 0].reshape(hd, page).astype(BF16) for r in v_refs[PAGES_PER_BLOCK * g:PAGES_PER_BLOCK * (g + 1)]]
            o_all = _dot_nt(pb[:, :page], vv[0]) + _dot_nt(pb[:, page:], vv[1])
            col = lax.broadcasted_iota(jnp.int32, o_all.shape, 1)
            row = lax.broadcasted_iota(jnp.int32, o_all.shape, 0)
            o_ref[0] += jnp.sum(jnp.where(col // HEAD_DIM == row, o_all, 0.0), axis=0, keepdims=True)


def _moba_decode(proj_s, cache_kt, cache_vt, page_table, rel_bias, layer):
    b, n = proj_s.shape
    w = n // 4
    nh = w // HEAD_DIM
    page = cache_kt.shape[4]
    past_len = page_table.shape[1] * page
    assert past_len % MOBA_BLOCK == 0
    nblk = past_len // MOBA_BLOCK
    q = proj_s[:, w:2 * w].reshape(b, nh, HEAD_DIM)
    qx = jnp.einsum('bhd,hg->bhgd', q, jnp.eye(nh, dtype=F32)).reshape(b, nh, w)
    kpos = np.arange(past_len).reshape(nblk, MOBA_BLOCK)
    bias = rel_bias.astype(F32).T[:, _t5_bucket(jnp.asarray(past_len - kpos, jnp.int32))]
    s_p, ks_p = _dec_scores(qx, cache_kt, page_table, jnp.transpose(bias, (1, 0, 2)), layer)
    rows = b * nh
    blk = MOBA_BLOCK
    to_rows = lambda a, last: jnp.transpose(a.reshape(b, nblk, nh, last), (0, 2, 1, 3)).reshape(rows, nblk * last)
    b0 = jnp.broadcast_to(jnp.tile(rel_bias.astype(F32)[_t5_bucket(jnp.zeros((), jnp.int32))], b)[:, None],
                          (rows, LANES))
    f = jax.ShapeDtypeStruct
    p_rows, p_own, sel = pl.pallas_call(
        _dec_select_kernel, name="moba_dec_select",
        out_shape=[f((rows, nblk * blk), F32), f((rows, LANES), F32), f((rows, nblk), F32)],
        compiler_params=pltpu.CompilerParams(vmem_limit_bytes=VMEM_LIMIT),
    )(q.reshape(rows, HEAD_DIM), proj_s[:, 2 * w:3 * w].reshape(rows, HEAD_DIM), b0,
      to_rows(ks_p, HEAD_DIM), to_rows(s_p, blk))

    grp = math.gcd(DEC_GROUP, nblk)
    need = (jnp.max(sel.reshape(b, nh, nblk), axis=1) > 0.5).reshape(b, nblk // grp, grp)
    blk_id = jnp.arange(nblk, dtype=jnp.int32).reshape(1, nblk // grp, grp)
    last_needed = lax.cummax(jnp.where(need, blk_id, -1), axis=1)
    first_needed = jnp.min(jnp.where(need, blk_id, nblk), axis=1, keepdims=True)
    fallback = jnp.where(first_needed < nblk, first_needed, blk_id[:, :1])
    src = jnp.where(last_needed < 0, fallback, last_needed).astype(jnp.int32).reshape(b, nblk)
    p_blk = jnp.transpose(p_rows.reshape(b, nh, nblk, blk), (0, 2, 1, 3))
    p_own_x = jnp.repeat(p_own[:, 0].reshape(b, nh), HEAD_DIM, axis=1).reshape(b, 1, w)
    page_spec = lambda g, off: pl.BlockSpec(
        (1, 1, nh, HEAD_DIM, page),
        lambda bi, j, pt, sr, nd: (pt[bi, PAGES_PER_BLOCK * sr[bi, j * grp + g] + off], layer, 0, 0, 0))
    seq_spec = pl.BlockSpec((1, 1, w), lambda bi, j, pt, sr, nd: (bi, 0, 0))
    att = pl.pallas_call(
        _dec_pv_kernel, name="moba_dec_pv",
        grid_spec=pltpu.PrefetchScalarGridSpec(
            num_scalar_prefetch=3,
            grid=(b, nblk // grp),
            in_specs=[pl.BlockSpec((1, grp, nh, blk), lambda bi, j, pt, sr, nd: (bi, j, 0, 0))]
            + [page_spec(g, off) for g in range(grp) for off in range(PAGES_PER_BLOCK)]
            + [seq_spec, seq_spec],
            out_specs=seq_spec,
        ),
        out_shape=f((b, 1, w), F32),
        compiler_params=_cparams("parallel", "arbitrary"),
    )(page_table, src, need.reshape(b, nblk).astype(jnp.int32), p_blk,
      *([cache_vt] * (grp * PAGES_PER_BLOCK)), p_own_x, proj_s[:, 3 * w:].reshape(b, 1, w))
    return att.reshape(b, w).astype(BF16)


def _tile(m, pref):
    t = min(m, pref)
    while m % t:
        t //= 2
    return t


def _ff_tile(ff, pref):
    best = LANES
    for t in range(LANES, pref + 1, LANES):
        if ff % t == 0:
            best = t
    return best


def kernel(x_prompt, x_sample, cache_k, cache_v, page_table, state_s5_re, state_s5_im, state_pool, rel_bias,
           norm_mix_e, w_in_e, q_norm_e, k_norm_e, s5_a_re, s5_a_im, s5_log_dt, s5_b_re, s5_b_im, s5_c_re,
           s5_c_im, s5_d, s5_w_glu, s5_b_glu, w_out_e, norm_ffn_e, ffn_w_gate, ffn_w_up, ffn_w_down,
           norm_mix_o, pool_w, pool_scale, norm_ffn_o, router_w, router_b, moe_w_gate, moe_w_up, moe_w_down):
    bp, seq, d = x_prompt.shape
    db = x_sample.shape[0]
    assert x_sample.shape[1] == 1
    depth = norm_mix_e.shape[0] + norm_mix_o.shape[0]
    wdt = w_in_e.shape[2] // 4
    nh = wdt // HEAD_DIM
    past_len = page_table.shape[1] * cache_k.shape[2]
    n_exp = router_w.shape[2]
    gn = s5_a_re.shape[1] * s5_a_re.shape[2]

    mp = bp * seq
    xp = x_prompt.reshape(mp, d)
    xs = x_sample.reshape(db, d)
    tm_p = _tile(mp, TOKEN_TILE)
    bias_tiles = _prompt_bias_tiles(rel_bias)
    row = lambda v: v.reshape(1, -1).astype(F32)
    cache_kt = jnp.transpose(cache_k, (0, 1, 3, 4, 2))
    cache_vt = jnp.transpose(cache_v, (0, 1, 3, 4, 2))

    n_even = norm_mix_e.shape[0]
    kp_all = jnp.zeros((mp, n_even * wdt), F32)
    vp_all = jnp.zeros((mp, n_even * wdt), F32)
    ks_all = jnp.zeros((db, n_even * wdt), F32)
    vs_all = jnp.zeros((db, n_even * wdt), F32)
    moe_wg, moe_wu, moe_wd = moe_w_gate.astype(BF16), moe_w_up.astype(BF16), moe_w_down.astype(BF16)
    s5p_l, s5s_re_l, s5s_im_l = [], [], []
    poolp_l, pools_l = [], []
    for layer in range(depth):
        if layer % 2 == 0:
            e = layer // 2
            w_in = w_in_e[e].astype(BF16)
            ones = jnp.ones((wdt,), F32)
            head_gain = jnp.stack([ones, jnp.tile(q_norm_e[e].astype(F32), nh),
                                   jnp.tile(k_norm_e[e].astype(F32), nh), ones]).reshape(4, 1, wdt)
            wb, tabs, lam, wc = _s5_weights(s5_a_re[e], s5_a_im[e], s5_log_dt[e], s5_b_re[e], s5_b_im[e],
                                            s5_c_re[e], s5_c_im[e])
            dsk = row(s5_d[e])
            wgl = s5_w_glu[e].astype(BF16)
            bgl = row(s5_b_glu[e])
            w_out = w_out_e[e].astype(BF16)
            wg, wu, wd = ffn_w_gate[e].astype(BF16), ffn_w_up[e].astype(BF16), ffn_w_down[e].astype(BF16)
            tf = _ff_tile(wg.shape[1], FFN_TILE)

            uq, kp_all, vp_all = _in_proj(xp, row(norm_mix_e[e]), w_in, head_gain, kp_all, vp_all, e, tm_p)
            uq3 = uq.reshape(bp, seq, 2 * wdt)
            s5_out, hfin = _s5_scan(uq3, jnp.zeros((bp, 1, 2 * gn), F32), wb, tabs, wc, dsk, wgl, bgl,
                                    _tile(seq, SCAN_TILE))
            att = _moba_prompt(uq3, kp_all.reshape(bp, seq, -1), vp_all.reshape(bp, seq, -1), e, bias_tiles)
            xp = _mix_ffn(xp, s5_out.reshape(mp, wdt), att.reshape(mp, wdt), w_out, row(norm_ffn_e[e]),
                          wg, wu, wd, tm_p, tf)
            s5p_l.append(hfin.reshape(bp, 2, -1, S5_STATE))

            uq_s, ks_all, vs_all = _in_proj(xs, row(norm_mix_e[e]), w_in, head_gain, ks_all, vs_all, e, db)
            proj_s = jnp.concatenate([uq_s, ks_all[:, e * wdt:(e + 1) * wdt], vs_all[:, e * wdt:(e + 1) * wdt]],
                                     axis=1)
            s5_out_s, hr_s, hi_s = _s5_step(proj_s, state_s5_re[:, e].reshape(db, gn).astype(F32),
                                            state_s5_im[:, e].reshape(db, gn).astype(F32),
                                            wb, lam, wc, dsk, wgl, bgl)
            att_s = _moba_decode(proj_s, cache_kt, cache_vt, page_table, rel_bias, e)
            xs = _mix_ffn(xs, s5_out_s, att_s, w_out, row(norm_ffn_e[e]), wg, wu, wd, db, tf)
            s5s_re_l.append(hr_s.reshape(db, -1, S5_STATE))
            s5s_im_l.append(hi_s.reshape(db, -1, S5_STATE))
        else:
            o = layer // 2
            wp = pool_w[o].astype(BF16)
            sc = row(pool_scale[o])
            wr = jnp.zeros((d, LANES), BF16).at[:, :n_exp].set(router_w[o].astype(BF16))
            br = jnp.zeros((1, LANES), F32).at[:, :n_exp].set(router_b[o].astype(F32))
            tf = _ff_tile(moe_wg.shape[3], EXPERT_TILE)

            xp3, hist_p = _pool_prompt(xp.reshape(bp, seq, d), row(norm_mix_o[o]), wp, sc, _tile(seq, TOKEN_TILE))
            xp = _moe_sparse(xp3.reshape(mp, d), row(norm_ffn_o[o]), wr, br, moe_wg, moe_wu, moe_wd, o, tm_p, tf // 2)
            poolp_l.append(hist_p[:, HALO - POOL_HIST:])

            hist_s = state_pool[:, o].astype(F32)
            xs, hn_s = _pool_step(xs, row(norm_mix_o[o]), jnp.transpose(hist_s, (1, 0, 2)), wp, sc,
                                  min(POOL_HIST, past_len))
            xs = _moe_dense(xs, row(norm_ffn_o[o]), wr, br, moe_wg, moe_wu, moe_wd, o, db, tf)
            pools_l.append(jnp.concatenate([hist_s[:, 1:], hn_s[:, None]], axis=1))

    s5p = jnp.stack(s5p_l, axis=1)

    def per_head(a, lead):
        parts = [a[:, l * wdt:(l + 1) * wdt].reshape(lead + (nh, HEAD_DIM)) for l in range(n_even)]
        return jnp.stack(parts, axis=len(lead))

    return (xp.reshape(bp, seq, d), xs.reshape(db, 1, d),
            per_head(kp_all, (bp, seq)), per_head(vp_all, (bp, seq)),
            per_head(ks_all, (db, 1)), per_head(vs_all, (db, 1)),
            s5p[:, :, 0], s5p[:, :, 1],
            jnp.stack(s5s_re_l, axis=1), jnp.stack(s5s_im_l, axis=1),
            jnp.stack(poolp_l, axis=1), jnp.stack(pools_l, axis=1))
```

```python
import functools
import math

import jax
import jax.numpy as jnp
import numpy as np
from jax import lax
from jax.experimental import pallas as pl
from jax.experimental.pallas import tpu as pltpu

F32 = jnp.float32
BF16 = jnp.bfloat16
EPS = 1e-6
NEG = -1e30

S5_STATE = 64
HEAD_DIM = 64
MOBA_BLOCK = 256
MOBA_TOPK = 3
REL_BUCKETS = 32
REL_MAX_DIST = 1024
POOL_WINDOWS = (2, 4, 8, 16)
POOL_HIST = max(POOL_WINDOWS) - 1
TOP_K_EXPERTS = 2

LANES = 128
SUBLANES = 8
VMEM_LIMIT = 48 * 1024 * 1024

TOKEN_TILE = 512
SCAN_TILE = 512
FFN_TILE = 1536
EXPERT_TILE = 1792


def _cparams(*sem):
    return pltpu.CompilerParams(dimension_semantics=sem, vmem_limit_bytes=VMEM_LIMIT)


def _rmsnorm(x, g):
    ms = jnp.mean(x * x, axis=-1, keepdims=True)
    return x * lax.rsqrt(ms + EPS) * g


def _split_bf16(x):
    hi = x.astype(BF16)
    lo = (x - hi.astype(F32)).astype(BF16)
    return hi, lo


def _dot(a, b):
    return jnp.dot(a, b, preferred_element_type=F32)


def _dot_nt(a, b):
    return lax.dot_general(a, b, (((1,), (1,)), ((), ())), preferred_element_type=F32)


def _silu(x):
    return x * jax.nn.sigmoid(x)


def _in_proj_kernel(x_ref, g_ref, w_ref, hn_ref, bd_ref, k_in, v_in, uq_ref, k_ref, v_ref, h_scr):
    del k_in, v_in
    j = pl.program_id(1)

    @pl.when(j == 0)
    def _():
        h_scr[...] = _rmsnorm(x_ref[...], g_ref[...]).astype(BF16)

    y = _dot(h_scr[...], w_ref[...])

    def head_normed():
        hi, lo = _split_bf16(y * y)
        ms = _dot(hi, bd_ref[...]) + _dot(lo, bd_ref[...])
        return y * lax.rsqrt(ms + EPS) * hn_ref[0]

    @pl.when(j == 0)
    def _():
        uq_ref[...] = y

    @pl.when(j == 1)
    def _():
        uq_ref[...] = head_normed()

    @pl.when(j == 2)
    def _():
        k_ref[...] = head_normed()

    @pl.when(j == 3)
    def _():
        v_ref[...] = y


def _in_proj(x, g, w_bf, head_gain, k_all, v_all, layer, tm):
    m, d = x.shape
    wdt = w_bf.shape[1] // 4
    bd = np.kron(np.eye(wdt // HEAD_DIM), np.full((HEAD_DIM, HEAD_DIM), 1.0 / HEAD_DIM))
    bd = jnp.asarray(bd, BF16)
    kv_spec = pl.BlockSpec((tm, wdt), lambda i, j: (i, layer))
    return pl.pallas_call(
        _in_proj_kernel, name="in_proj",
        grid=(m // tm, 4),
        in_specs=[
            pl.BlockSpec((tm, d), lambda i, j: (i, 0)),
            pl.BlockSpec((1, d), lambda i, j: (0, 0)),
            pl.BlockSpec((d, wdt), lambda i, j: (0, j)),
            pl.BlockSpec((1, 1, wdt), lambda i, j: (j, 0, 0)),
            pl.BlockSpec((wdt, wdt), lambda i, j: (0, 0)),
            pl.BlockSpec(memory_space=pl.ANY),
            pl.BlockSpec(memory_space=pl.ANY),
        ],
        out_specs=[pl.BlockSpec((tm, wdt), lambda i, j: (i, jnp.minimum(j, 1))), kv_spec, kv_spec],
        out_shape=[jax.ShapeDtypeStruct((m, 2 * wdt), F32), jax.ShapeDtypeStruct(k_all.shape, F32),
                   jax.ShapeDtypeStruct(v_all.shape, F32)],
        input_output_aliases={5: 1, 6: 2},
        scratch_shapes=[pltpu.VMEM((tm, d), BF16)],
        compiler_params=_cparams("parallel", "arbitrary"),
    )(x, g, w_bf, head_gain, bd, k_all, v_all)


def _mix_ffn_kernel(x_ref, a_ref, b_ref, wo_ref, g_ref, wg_ref, wu_ref, wd_ref, o_ref,
                    x1_scr, h_scr, acc_scr):
    f = pl.program_id(1)
    half = a_ref.shape[1]

    @pl.when(f == 0)
    def _():
        x1 = (x_ref[...] + _dot(a_ref[...], wo_ref[:half, :]) + _dot(b_ref[...], wo_ref[half:, :]))
        x1_scr[...] = x1
        h_scr[...] = _rmsnorm(x1, g_ref[...]).astype(BF16)
        acc_scr[...] = jnp.zeros_like(acc_scr)

    h = h_scr[...]
    act = (_silu(_dot(h, wg_ref[...])) * _dot(h, wu_ref[...])).astype(BF16)
    acc_scr[...] += _dot(act, wd_ref[...])

    @pl.when(f == pl.num_programs(1) - 1)
    def _():
        o_ref[...] = x1_scr[...] + acc_scr[...]


def _mix_ffn(x, a, b, wo_bf, g, wg_bf, wu_bf, wd_bf, tm, tf):
    m, d = x.shape
    half = a.shape[1]
    ff = wg_bf.shape[1]
    return pl.pallas_call(
        _mix_ffn_kernel, name="mix_ffn",
        grid=(m // tm, ff // tf),
        in_specs=[
            pl.BlockSpec((tm, d), lambda i, f: (i, 0)),
            pl.BlockSpec((tm, half), lambda i, f: (i, 0)),
            pl.BlockSpec((tm, half), lambda i, f: (i, 0)),
            pl.BlockSpec((2 * half, d), lambda i, f: (0, 0)),
            pl.BlockSpec((1, d), lambda i, f: (0, 0)),
            pl.BlockSpec((d, tf), lambda i, f: (0, f)),
            pl.BlockSpec((d, tf), lambda i, f: (0, f)),
            pl.BlockSpec((tf, d), lambda i, f: (f, 0)),
        ],
        out_specs=pl.BlockSpec((tm, d), lambda i, f: (i, 0)),
        out_shape=jax.ShapeDtypeStruct((m, d), F32),
        scratch_shapes=[pltpu.VMEM((tm, d), F32), pltpu.VMEM((tm, d), BF16), pltpu.VMEM((tm, d), F32)],
        compiler_params=_cparams("parallel", "arbitrary"),
    )(x, a, b, wo_bf, g, wg_bf, wu_bf, wd_bf)


def _route(h, wr, br, n_exp):
    logits = _dot(h.astype(BF16), wr) + br
    lane = lax.broadcasted_iota(jnp.int32, logits.shape, 1)
    logits = jnp.where(lane < n_exp, logits, NEG)
    m1 = jnp.max(logits, axis=-1, keepdims=True)
    i1 = jnp.min(jnp.where(logits == m1, lane, LANES), axis=-1, keepdims=True)
    rest = jnp.where(lane == i1, NEG, logits)
    m2 = jnp.max(rest, axis=-1, keepdims=True)
    i2 = jnp.min(jnp.where(rest == m2, lane, LANES), axis=-1, keepdims=True)
    e2 = jnp.exp(m2 - m1)
    g1 = 1.0 / (1.0 + e2)
    g2 = e2 / (1.0 + e2)
    gate = jnp.where(lane == i1, g1, 0.0) + jnp.where(lane == i2, g2, 0.0)
    return gate, jnp.logical_or(lane == i1, lane == i2)


def _moe_dense_kernel(x_ref, g_ref, wr_ref, br_ref, wg_ref, wu_ref, wd_ref, o_ref,
                      h_scr, gate_scr, acc_scr, *, n_exp):
    e = pl.program_id(1)
    f = pl.program_id(2)

    @pl.when(jnp.logical_and(e == 0, f == 0))
    def _():
        h = _rmsnorm(x_ref[...], g_ref[...])
        h_scr[...] = h.astype(BF16)
        gate_scr[...] = _route(h, wr_ref[...], br_ref[...], n_exp)[0]
        acc_scr[...] = jnp.zeros_like(acc_scr)

    h = h_scr[...]
    lane = lax.broadcasted_iota(jnp.int32, gate_scr.shape, 1)
    w_e = jnp.sum(jnp.where(lane == e, gate_scr[...], 0.0), axis=-1, keepdims=True)
    act = (_silu(_dot(h, wg_ref[0, 0])) * _dot(h, wu_ref[0, 0])).astype(BF16)
    acc_scr[...] += w_e * _dot(act, wd_ref[0, 0])

    @pl.when(jnp.logical_and(e == n_exp - 1, f == pl.num_programs(2) - 1))
    def _():
        o_ref[...] = x_ref[...] + acc_scr[...]


def _moe_dense(x, g, wr_pad, br_pad, wg_bf, wu_bf, wd_bf, layer, tm, tf):
    m, d = x.shape
    _, n_exp, _, ff = wg_bf.shape
    return pl.pallas_call(
        functools.partial(_moe_dense_kernel, n_exp=n_exp), name="moe_dense",
        grid=(m // tm, n_exp, ff // tf),
        in_specs=[
            pl.BlockSpec((tm, d), lambda i, e, f: (i, 0)),
            pl.BlockSpec((1, d), lambda i, e, f: (0, 0)),
            pl.BlockSpec((d, LANES), lambda i, e, f: (0, 0)),
            pl.BlockSpec((1, LANES), lambda i, e, f: (0, 0)),
            pl.BlockSpec((1, 1, d, tf), lambda i, e, f: (layer, e, 0, f)),
            pl.BlockSpec((1, 1, d, tf), lambda i, e, f: (layer, e, 0, f)),
            pl.BlockSpec((1, 1, tf, d), lambda i, e, f: (layer, e, f, 0)),
        ],
        out_specs=pl.BlockSpec((tm, d), lambda i, e, f: (i, 0)),
        out_shape=jax.ShapeDtypeStruct((m, d), F32),
        scratch_shapes=[pltpu.VMEM((tm, d), BF16), pltpu.VMEM((tm, LANES), F32), pltpu.VMEM((tm, d), F32)],
        compiler_params=_cparams("parallel", "arbitrary", "arbitrary"),
    )(x, g, wr_pad, br_pad, wg_bf, wu_bf, wd_bf)


def _moe_route_kernel(x_ref, g_ref, wr_ref, br_ref, h_ref, gate_ref, pos_ref, post_ref, cb_ref, tot_ref, carry,
                      *, n_exp):
    i = pl.program_id(0)
    tm = x_ref.shape[0]

    @pl.when(i == 0)
    def _():
        carry[...] = jnp.zeros_like(carry)

    h = _rmsnorm(x_ref[...], g_ref[...])
    h_ref[...] = h.astype(BF16)
    gate, chosen = _route(h, wr_ref[...], br_ref[...], n_exp)
    gate_ref[...] = gate
    onehot = jnp.where(chosen, 1.0, 0.0)
    r = lax.broadcasted_iota(jnp.int32, (tm, tm), 0)
    c = lax.broadcasted_iota(jnp.int32, (tm, tm), 1)
    before = _dot(jnp.where(c < r, 1.0, 0.0).astype(BF16), onehot.astype(BF16))
    start = carry[...]
    pos = jnp.where(chosen, before + start, -1.0)
    pos_ref[...] = pos
    post_ref[...] = pos.T[:post_ref.shape[0], :]
    cb_ref[0] = start
    total = start + jnp.sum(onehot, axis=0, keepdims=True)
    carry[...] = total
    tot_ref[...] = total


def _moe_route(x, g, wr_bf, br, n_exp, tm):
    m, d = x.shape
    nt = m // tm
    ne8 = -(-n_exp // SUBLANES) * SUBLANES
    f = jax.ShapeDtypeStruct
    return pl.pallas_call(
        functools.partial(_moe_route_kernel, n_exp=n_exp), name="moe_route",
        grid=(nt,),
        in_specs=[
            pl.BlockSpec((tm, d), lambda i: (i, 0)),
            pl.BlockSpec((1, d), lambda i: (0, 0)),
            pl.BlockSpec((d, LANES), lambda i: (0, 0)),
            pl.BlockSpec((1, LANES), lambda i: (0, 0)),
        ],
        out_specs=[
            pl.BlockSpec((tm, d), lambda i: (i, 0)),
            pl.BlockSpec((tm, LANES), lambda i: (i, 0)),
            pl.BlockSpec((tm, LANES), lambda i: (i, 0)),
            pl.BlockSpec((ne8, tm), lambda i: (0, i)),
            pl.BlockSpec((1, 1, LANES), lambda i: (i, 0, 0)),
            pl.BlockSpec((1, LANES), lambda i: (0, 0)),
        ],
        out_shape=[f((m, d), BF16), f((m, LANES), F32), f((m, LANES), F32), f((ne8, m), F32),
                   f((nt, 1, LANES), F32), f((1, LANES), F32)],
        scratch_shapes=[pltpu.VMEM((1, LANES), F32)],
        compiler_params=_cparams("arbitrary"),
    )(x, g, wr_bf, br)


def _moe_gather_kernel(pc_ref, ps_ref, pe_ref, pfirst_ref, pvalid_ref, off_ref, h_ref, post_ref, o_ref, acc):
    k = pl.program_id(0)
    rc, tm = o_ref.shape[0], h_ref.shape[0]

    @pl.when(pfirst_ref[k] == 1)
    def _():
        acc[...] = jnp.zeros_like(acc)

    @pl.when(pvalid_ref[k] == 1)
    def _():
        e = pe_ref[k]
        dest = post_ref[pl.ds(e, 1), :] + off_ref[e].astype(F32)
        rows = (pc_ref[k] * rc + lax.broadcasted_iota(jnp.int32, (rc, tm), 0)).astype(F32)
        acc[...] += _dot(jnp.where(dest == rows, 1.0, 0.0).astype(BF16), h_ref[...])

    o_ref[...] = acc[...].astype(BF16)


def _moe_ffn_kernel(ce_ref, cv_ref, x_ref, wg_ref, wu_ref, wd_ref, y_ref, *, tf):
    c = pl.program_id(0)

    @pl.when(cv_ref[c] == 1)
    def _():
        x = x_ref[...]
        acc = jnp.zeros(y_ref.shape, F32)
        for f0 in range(0, wg_ref.shape[3], tf):
            act = (_silu(_dot(x, wg_ref[0, 0, :, f0:f0 + tf])) * _dot(x, wu_ref[0, 0, :, f0:f0 + tf])).astype(BF16)
            acc = acc + _dot(act, wd_ref[0, 0, f0:f0 + tf, :])
        y_ref[...] = acc.astype(BF16)

    @pl.when(cv_ref[c] == 0)
    def _():
        y_ref[...] = jnp.zeros_like(y_ref)


def _moe_combine_kernel(qs_ref, qc_ref, qe_ref, qfirst_ref, qvalid_ref, off_ref, x_ref, y_ref, pos_ref, gate_ref,
                        o_ref):
    k = pl.program_id(0)
    tm, rc = x_ref.shape[0], y_ref.shape[0]

    @pl.when(qfirst_ref[k] == 1)
    def _():
        o_ref[...] = x_ref[...]

    @pl.when(qvalid_ref[k] == 1)
    def _():
        e = qe_ref[k]
        lane = lax.broadcasted_iota(jnp.int32, pos_ref.shape, 1)
        pos_e = jnp.sum(jnp.where(lane == e, pos_ref[...], 0.0), axis=-1, keepdims=True)
        gate_e = jnp.sum(jnp.where(lane == e, gate_ref[...], 0.0), axis=-1, keepdims=True)
        dest = pos_e + off_ref[e].astype(F32)
        cols = (qc_ref[k] * rc + lax.broadcasted_iota(jnp.int32, (tm, rc), 1)).astype(F32)
        pick = jnp.where(dest == cols, 1.0, 0.0).astype(BF16)
        o_ref[...] += gate_e * _dot(pick, y_ref[...])


def _pair_list(inter, n_pairs_max):
    nb = inter.shape[1]
    flat = inter.reshape(-1)
    n = jnp.sum(flat.astype(jnp.int32))
    idx = jnp.nonzero(flat, size=n_pairs_max, fill_value=0)[0].astype(jnp.int32)
    k = jnp.arange(n_pairs_max, dtype=jnp.int32)
    idx = jnp.where(k < n, idx, idx[jnp.maximum(n - 1, 0)])
    a, b = idx // nb, idx % nb
    valid = k < n
    first = jnp.logical_and(valid, jnp.logical_or(k == 0, a != jnp.roll(a, 1)))
    return a, b, valid.astype(jnp.int32), first.astype(jnp.int32)


ROW_CHUNK = 256


def _moe_sparse(x, g, wr_bf, br, wg_bf, wu_bf, wd_bf, layer, tm, tf):
    m, d = x.shape
    _, n_exp, _, ff = wg_bf.shape
    nt = m // tm
    rc = ROW_CHUNK
    h, gate, pos, pos_t, cb, tot = _moe_route(x, g, wr_bf, br, n_exp, tm)

    cnt = tot[0, :n_exp].astype(jnp.int32)
    cbi = jnp.concatenate([cb[:, 0, :n_exp], tot[:, :n_exp]], axis=0).astype(jnp.int32)
    gsz = (cnt + rc - 1) // rc * rc
    ends = jnp.cumsum(gsz)
    off = (ends - gsz).astype(jnp.int32)
    n_chunks = TOP_K_EXPERTS * m // rc + n_exp
    cstart = jnp.arange(n_chunks, dtype=jnp.int32) * rc
    ce = jnp.minimum(jnp.searchsorted(ends, cstart, side='right'), n_exp - 1).astype(jnp.int32)
    cv = cstart < ends[-1]
    per_tile = cbi.T[ce]
    start = off[ce][:, None] + per_tile[:, :-1]
    end = off[ce][:, None] + per_tile[:, 1:]
    inter = (cv[:, None] & (end > start) & (start < cstart[:, None] + rc) & (end > cstart[:, None]))
    n_pairs = n_chunks + n_exp * nt
    unused = jnp.logical_and(jnp.logical_not(cv)[:, None], jnp.arange(nt)[None, :] == 0)
    pc, ps, plisted, pfirst = _pair_list(jnp.logical_or(inter, unused), n_pairs + n_exp)
    pvalid = plisted * inter[pc, ps].astype(jnp.int32)
    qs, qc, qvalid, qfirst = _pair_list(inter.T, n_pairs)

    xs = pl.pallas_call(
        _moe_gather_kernel, name="moe_gather",
        grid_spec=pltpu.PrefetchScalarGridSpec(
            num_scalar_prefetch=6,
            grid=(n_pairs + n_exp,),
            in_specs=[
                pl.BlockSpec((tm, d), lambda k, pc, ps, *_: (ps[k], 0)),
                pl.BlockSpec((pos_t.shape[0], tm), lambda k, pc, ps, *_: (0, ps[k])),
            ],
            out_specs=pl.BlockSpec((rc, d), lambda k, pc, *_: (pc[k], 0)),
            scratch_shapes=[pltpu.VMEM((rc, d), F32)],
        ),
        out_shape=jax.ShapeDtypeStruct((n_chunks * rc, d), BF16),
        compiler_params=_cparams("arbitrary"),
    )(pc, ps, ce[pc], pfirst, pvalid, off, h, pos_t)

    cvi = cv.astype(jnp.int32)
    wspec = lambda shape: pl.BlockSpec((1,) + shape, lambda c, ce, cv: (layer, ce[c], 0, 0),
                                       pipeline_mode=pl.Buffered(1))
    ys = pl.pallas_call(
        functools.partial(_moe_ffn_kernel, tf=tf), name="moe_ffn",
        grid_spec=pltpu.PrefetchScalarGridSpec(
            num_scalar_prefetch=2,
            grid=(n_chunks,),
            in_specs=[
                pl.BlockSpec((rc, d), lambda c, ce, cv: (c * cv[c], 0)),
                wspec((1, d, ff)), wspec((1, d, ff)), wspec((1, ff, d)),
            ],
            out_specs=pl.BlockSpec((rc, d), lambda c, ce, cv: (c, 0)),
        ),
        out_shape=jax.ShapeDtypeStruct((n_chunks * rc, d), BF16),
        compiler_params=_cparams("arbitrary"),
    )(ce, cvi, xs, wg_bf, wu_bf, wd_bf)

    return pl.pallas_call(
        _moe_combine_kernel, name="moe_combine",
        grid_spec=pltpu.PrefetchScalarGridSpec(
            num_scalar_prefetch=6,
            grid=(n_pairs,),
            in_specs=[
                pl.BlockSpec((tm, d), lambda k, qs, qc, *_: (qs[k], 0)),
                pl.BlockSpec((rc, d), lambda k, qs, qc, *_: (qc[k], 0)),
                pl.BlockSpec((tm, LANES), lambda k, qs, *_: (qs[k], 0)),
                pl.BlockSpec((tm, LANES), lambda k, qs, *_: (qs[k], 0)),
            ],
            out_specs=pl.BlockSpec((tm, d), lambda k, qs, *_: (qs[k], 0)),
        ),
        out_shape=jax.ShapeDtypeStruct((m, d), F32),
        compiler_params=_cparams("arbitrary"),
    )(qs, qc, ce[qc], qfirst, qvalid, off, x, ys, pos, gate)


HALO = 16


def _pool_kernel(x_ref, g_ref, wp_ref, sc_ref, o_ref, hist_ref, hbuf):
    t = pl.program_id(1)
    tm = x_ref.shape[1]
    gw = wp_ref.shape[1]

    @pl.when(t == 0)
    def _():
        hbuf[0:HALO, :] = jnp.zeros((HALO, hbuf.shape[1]), F32)

    @pl.when(t > 0)
    def _():
        hbuf[0:HALO, :] = hbuf[tm:tm + HALO, :]

    x = x_ref[0]
    hn = _rmsnorm(x, g_ref[...])
    hbuf[HALO:HALO + tm, :] = hn
    pos = t * tm + lax.broadcasted_iota(jnp.int32, (tm, 1), 0)
    ys = []
    for gi, w in enumerate(POOL_WINDOWS):
        sl = slice(gi * gw, (gi + 1) * gw)
        tot = hn[:, sl]
        for i in range(1, w):
            tot = tot + hbuf[HALO - i:HALO - i + tm, sl]
        cnt = jnp.minimum(pos + 1, w).astype(F32)
        pooled = tot / cnt - hn[:, sl]
        ys.append(_dot(pooled.astype(BF16), wp_ref[gi]))
    o_ref[0] = x + jnp.concatenate(ys, axis=1) * sc_ref[...]

    @pl.when(t == pl.num_programs(1) - 1)
    def _():
        hist_ref[0] = hbuf[tm:tm + HALO, :]


def _pool_prompt(x, g, wp_bf, sc, tm):
    b, s, d = x.shape
    ng, gw, _ = wp_bf.shape
    return pl.pallas_call(
        _pool_kernel, name="pool_prompt",
        grid=(b, s // tm),
        in_specs=[
            pl.BlockSpec((1, tm, d), lambda i, t: (i, t, 0)),
            pl.BlockSpec((1, d), lambda i, t: (0, 0)),
            pl.BlockSpec((ng, gw, gw), lambda i, t: (0, 0, 0)),
            pl.BlockSpec((1, d), lambda i, t: (0, 0)),
        ],
        out_specs=[
            pl.BlockSpec((1, tm, d), lambda i, t: (i, t, 0)),
            pl.BlockSpec((1, HALO, d), lambda i, t: (i, 0, 0)),
        ],
        out_shape=[jax.ShapeDtypeStruct((b, s, d), F32), jax.ShapeDtypeStruct((b, HALO, d), F32)],
        scratch_shapes=[pltpu.VMEM((HALO + tm, d), F32)],
        compiler_params=_cparams("parallel", "arbitrary"),
    )(x, g, wp_bf, sc)


def _pool_step_kernel(x_ref, g_ref, hist_ref, wp_ref, sc_ref, o_ref, hn_ref, *, n_valid):
    gw = wp_ref.shape[1]
    x = x_ref[...]
    hn = _rmsnorm(x, g_ref[...])
    hn_ref[...] = hn
    ys = []
    for gi, w in enumerate(POOL_WINDOWS):
        sl = slice(gi * gw, (gi + 1) * gw)
        tot = hn[:, sl]
        for i in range(1, min(w, n_valid + 1)):
            tot = tot + hist_ref[POOL_HIST - i][:, sl]
        pooled = tot / float(min(w, n_valid + 1)) - hn[:, sl]
        ys.append(_dot(pooled.astype(BF16), wp_ref[gi]))
    o_ref[...] = x + jnp.concatenate(ys, axis=1) * sc_ref[...]


def _pool_step(x, g, hist_t, wp_bf, sc, n_valid):
    b, d = x.shape
    return pl.pallas_call(
        functools.partial(_pool_step_kernel, n_valid=n_valid), name="pool_step",
        out_shape=[jax.ShapeDtypeStruct((b, d), F32), jax.ShapeDtypeStruct((b, d), F32)],
        compiler_params=pltpu.CompilerParams(vmem_limit_bytes=VMEM_LIMIT),
    )(x, g, hist_t, wp_bf, sc)


N_POW = SUBLANES


def _s5_prep_kernel(ar_ref, ai_ref, ldt_ref, br_ref, bi_ref, pwr_ref, pwi_ref, bbr_ref, bbi_ref):
    ar = ar_ref[...]
    ai = ai_ref[...]
    dt = jnp.exp(ldt_ref[...])
    mag = jnp.exp(ar * dt)
    lr = mag * jnp.cos(ai * dt)
    li = mag * jnp.sin(ai * dt)
    den = ar * ar + ai * ai
    rr = ((lr - 1.0) * ar + li * ai) / den
    ri = (li * ar - (lr - 1.0) * ai) / den
    br = br_ref[...]
    bi = bi_ref[...]
    bbr_ref[...] = rr * br - ri * bi
    bbi_ref[...] = rr * bi + ri * br
    cr, ci = lr, li
    for k in range(N_POW):
        pwr_ref[k] = cr
        pwi_ref[k] = ci
        cr, ci = cr * lr - ci * li, cr * li + ci * lr


def _s5_prep(a_re, a_im, log_dt, b_re, b_im):
    g, n = a_re.shape
    p = b_re.shape[2]
    f = jax.ShapeDtypeStruct
    return pl.pallas_call(
        _s5_prep_kernel, name="s5_prep",
        out_shape=[f((N_POW, g, 1, n), F32), f((N_POW, g, 1, n), F32), f((g, p, n), F32), f((g, p, n), F32)],
    )(a_re[:, None, :], a_im[:, None, :], log_dt[:, None, None],
      jnp.transpose(b_re, (0, 2, 1)), jnp.transpose(b_im, (0, 2, 1)))


def _s5_weights(a_re, a_im, log_dt, b_re, b_im, c_re, c_im):
    g, n = a_re.shape
    p = b_re.shape[2]
    gn = g * n
    pwr, pwi, bbr, bbi = _s5_prep(a_re, a_im, log_dt, b_re, b_im)
    pwr = pwr.reshape(N_POW, gn)
    pwi = pwi.reshape(N_POW, gn)
    row = np.arange(SUBLANES)[:, None]
    tabs = []
    for k in (1, 2, 4):
        tabs += [jnp.where(row >= k, pwr[k - 1][None], 0.0), jnp.where(row >= k, pwi[k - 1][None], 0.0)]
    tabs += [pwr, pwi]
    tabs = jnp.stack(tabs)
    lam = jnp.stack([pwr[0], pwi[0]])
    n_tiles = gn // LANES
    g_tile = LANES // n
    g_chunk = LANES // p
    place = np.zeros((n_tiles, g_chunk, g_tile), np.float32)
    for j in range(n_tiles):
        for i in range(g_tile):
            place[j, (j * g_tile + i) % g_chunk, i] = 1.0
    bb = jnp.stack([bbr, bbi]).reshape(2, n_tiles, g_tile, p, n)
    wb = jnp.einsum('cjipn,jli->jlpcin', bb, place).reshape(n_tiles, LANES, 2 * LANES).astype(BF16)
    n_out = g * p // LANES
    cc = jnp.stack([c_re, -c_im]).reshape(2, n_out, g_chunk, p, n)
    wc = jnp.einsum('cmlpn,lk->mclnkp', cc, np.eye(g_chunk, dtype=np.float32))
    wc = wc.reshape(n_out, 2 * g_chunk * n, LANES).astype(BF16)
    return wb, tabs, lam, wc


def _s5_drive(u, wb_ref, bur, bui, row0):
    t = u.shape[0]
    ub = u.astype(BF16)
    n_tiles = wb_ref.shape[0]
    u_per_tile = u.shape[1] // n_tiles
    for j in range(n_tiles):
        c0 = (j * u_per_tile) // LANES * LANES
        r = _dot(ub[:, c0:c0 + LANES], wb_ref[j])
        bur[row0:row0 + t, j * LANES:(j + 1) * LANES] = r[:, :LANES]
        bui[row0:row0 + t, j * LANES:(j + 1) * LANES] = r[:, LANES:]


def _s5_readout(u, hr, hi, wc_ref, d_ref, wgl_ref, bgl_ref):
    hrb = hr.astype(BF16)
    hib = hi.astype(BF16)
    kk = wc_ref.shape[1] // 2
    ys = []
    for m in range(wc_ref.shape[0]):
        ys.append(_dot(hrb[:, m * kk:(m + 1) * kk], wc_ref[m, :kk, :])
                  + _dot(hib[:, m * kk:(m + 1) * kk], wc_ref[m, kk:, :]))
    y = jnp.concatenate(ys, axis=1) + d_ref[...] * u
    z = jax.nn.gelu(y, approximate=True)
    return z * jax.nn.sigmoid(_dot(z.astype(BF16), wgl_ref[...]) + bgl_ref[...])


def _s5_scan_kernel(u_ref, h0_ref, wb_ref, tab_ref, wc_ref, d_ref, wgl_ref, bgl_ref, o_ref, hf_ref, bur, bui):
    t = pl.program_id(1)
    tt = u_ref.shape[1]
    gn = bur.shape[1]
    c0 = SUBLANES

    @pl.when(t == 0)
    def _():
        bur[0:c0, :] = jnp.broadcast_to(h0_ref[0, :, :gn], (c0, gn))
        bui[0:c0, :] = jnp.broadcast_to(h0_ref[0, :, gn:], (c0, gn))

    @pl.when(t > 0)
    def _():
        bur[0:c0, :] = bur[tt:tt + c0, :]
        bui[0:c0, :] = bui[tt:tt + c0, :]

    u = u_ref[0]
    _s5_drive(u, wb_ref, bur, bui, c0)

    def block(b, carry):
        base = pl.multiple_of(b * SUBLANES, SUBLANES)
        cr = bur[pl.ds(base + c0 - 1, 1), :]
        ci = bui[pl.ds(base + c0 - 1, 1), :]
        xr = bur[pl.ds(base + c0, SUBLANES), :]
        xi = bui[pl.ds(base + c0, SUBLANES), :]
        for s, k in enumerate((1, 2, 4)):
            ar = tab_ref[2 * s]
            ai = tab_ref[2 * s + 1]
            sr = pltpu.roll(xr, k, axis=0)
            si = pltpu.roll(xi, k, axis=0)
            xr, xi = xr + ar * sr - ai * si, xi + ar * si + ai * sr
        pr = tab_ref[6]
        pi = tab_ref[7]
        bur[pl.ds(base + c0, SUBLANES), :] = xr + pr * cr - pi * ci
        bui[pl.ds(base + c0, SUBLANES), :] = xi + pr * ci + pi * cr
        return carry

    lax.fori_loop(0, tt // SUBLANES, block, 0)
    o_ref[0] = _s5_readout(u, bur[c0:c0 + tt, :], bui[c0:c0 + tt, :], wc_ref, d_ref, wgl_ref, bgl_ref).astype(BF16)

    @pl.when(t == pl.num_programs(1) - 1)
    def _():
        hf_ref[0] = jnp.concatenate([bur[tt + c0 - 1:tt + c0, :], bui[tt + c0 - 1:tt + c0, :]], axis=1)


def _s5_scan(proj, h0, wb, tabs, wc, dsk, wgl_bf, bgl, tt):
    b, s, _ = proj.shape
    w = dsk.shape[1]
    gn = tabs.shape[2]
    full = lambda a: pl.BlockSpec(a.shape, lambda i, t: (0,) * a.ndim)
    return pl.pallas_call(
        _s5_scan_kernel, name="s5_scan",
        grid=(b, s // tt),
        in_specs=[
            pl.BlockSpec((1, tt, w), lambda i, t: (i, t, 0)),
            pl.BlockSpec((1, 1, 2 * gn), lambda i, t: (i, 0, 0)),
            full(wb), full(tabs), full(wc), full(dsk), full(wgl_bf), full(bgl),
        ],
        out_specs=[
            pl.BlockSpec((1, tt, w), lambda i, t: (i, t, 0)),
            pl.BlockSpec((1, 1, 2 * gn), lambda i, t: (i, 0, 0)),
        ],
        out_shape=[jax.ShapeDtypeStruct((b, s, w), BF16), jax.ShapeDtypeStruct((b, 1, 2 * gn), F32)],
        scratch_shapes=[pltpu.VMEM((tt + SUBLANES, gn), F32), pltpu.VMEM((tt + SUBLANES, gn), F32)],
        compiler_params=_cparams("parallel", "arbitrary"),
    )(proj, h0, wb, tabs, wc, dsk, wgl_bf, bgl)


def _s5_step_kernel(u_ref, h0r_ref, h0i_ref, wb_ref, lam_ref, wc_ref, d_ref, wgl_ref, bgl_ref,
                    o_ref, hr_ref, hi_ref, bur, bui):
    u = u_ref[...]
    _s5_drive(u, wb_ref, bur, bui, 0)
    lr = lam_ref[0:1, :]
    li = lam_ref[1:2, :]
    h0r = h0r_ref[...]
    h0i = h0i_ref[...]
    hr = bur[...] + lr * h0r - li * h0i
    hi = bui[...] + lr * h0i + li * h0r
    hr_ref[...] = hr
    hi_ref[...] = hi
    o_ref[...] = _s5_readout(u, hr, hi, wc_ref, d_ref, wgl_ref, bgl_ref).astype(BF16)


def _s5_step(proj, h0r, h0i, wb, lam, wc, dsk, wgl_bf, bgl):
    b = proj.shape[0]
    w = dsk.shape[1]
    gn = lam.shape[1]
    full = lambda a: pl.BlockSpec(a.shape, lambda i: (0,) * a.ndim)
    return pl.pallas_call(
        _s5_step_kernel, name="s5_step",
        grid=(1,),
        in_specs=[pl.BlockSpec((b, w), lambda i: (0, 0)), full(h0r), full(h0i), full(wb), full(lam), full(wc),
                  full(dsk), full(wgl_bf), full(bgl)],
        out_specs=[pl.BlockSpec((b, w), lambda i: (0, 0)), pl.BlockSpec((b, gn), lambda i: (0, 0)),
                   pl.BlockSpec((b, gn), lambda i: (0, 0))],
        out_shape=[jax.ShapeDtypeStruct((b, w), BF16), jax.ShapeDtypeStruct((b, gn), F32),
                   jax.ShapeDtypeStruct((b, gn), F32)],
        scratch_shapes=[pltpu.VMEM((b, gn), F32), pltpu.VMEM((b, gn), F32)],
        compiler_params=_cparams("arbitrary"),
    )(proj, h0r, h0i, wb, lam, wc, dsk, wgl_bf, bgl)


def _t5_bucket(rel):
    n = jnp.maximum(rel, 0)
    max_exact = REL_BUCKETS // 2
    nf = jnp.maximum(n, 1).astype(F32)
    large = max_exact + (jnp.log(nf / max_exact) / math.log(REL_MAX_DIST / max_exact)
                         * (REL_BUCKETS - max_exact)).astype(jnp.int32)
    large = jnp.minimum(large, REL_BUCKETS - 1)
    return jnp.where(n < max_exact, n, large)


N_DIST = -(-(REL_MAX_DIST + MOBA_BLOCK - 1) // MOBA_BLOCK) + 1


def _prompt_bias_tiles(rel_bias):
    blk = MOBA_BLOCK
    span = 2 * blk - 1
    rel = jnp.asarray(np.arange(-(blk - 1), N_DIST * blk), jnp.int32)
    line = jnp.where(rel >= 0, rel_bias.astype(F32).T[:, _t5_bucket(rel)], NEG)
    diag = jnp.stack([line[:, d * blk:d * blk + span] for d in range(N_DIST)], axis=1)
    line_rc = jnp.roll(diag[..., ::-1], -(blk - 1), axis=-1)
    rep = jnp.tile(line_rc, (1, 1, blk))[..., :blk * (span - 1)]
    return rep.reshape(diag.shape[0], N_DIST, blk, span - 1)[..., :blk]


def _rank_select(g, n_valid, idx, n_cand, axis):
    cnt = jnp.zeros(g.shape, jnp.int32)
    for jp in range(n_cand):
        gp = lax.slice_in_dim(g, jp, jp + 1, axis=axis)
        beats = jnp.logical_or(gp > g, jnp.logical_and(gp == g, jp < idx))
        cnt = cnt + jnp.where(jnp.logical_and(beats, jp < n_valid), 1, 0)
    return jnp.logical_and(idx < n_valid, cnt < MOBA_TOPK)


SWEEP_UNROLL = 4


def _moba_kernel(q_ref, k_ref, v_ref, bias_ref, o_ref, kft, vb, km, qa, sbuf, red, stat, acc_scr):
    i = pl.program_id(2)
    blk = MOBA_BLOCK
    nb = k_ref.shape[1] // blk
    n_hd = LANES // HEAD_DIM
    scale = HEAD_DIM ** -0.5

    @pl.when(i == 0)
    def _():
        feat = lax.broadcasted_iota(jnp.int32, (LANES, blk), 0)
        for j in range(nb):
            kj = k_ref[0, j * blk:(j + 1) * blk, :]
            kft[j, :LANES, :] = kj.T.astype(BF16)
            kft[j, LANES:, :] = jnp.where(feat == j, 1.0, 0.0).astype(BF16)
            km[j:j + 1, :] = jnp.mean(kj, axis=0, keepdims=True)
        vb[...] = v_ref[0].astype(BF16)

    lane = lax.broadcasted_iota(jnp.int32, (blk, LANES), 1)
    wide = lambda a: jnp.concatenate([a, a], axis=1)
    rep = lambda a: jnp.broadcast_to(a, (blk, LANES))
    for hd in range(n_hd):
        qm = (jnp.where(lane // HEAD_DIM == hd, q_ref[0], 0.0) * scale).astype(BF16)
        g = _dot_nt(km[...].astype(BF16), qm)
        jrow = lax.broadcasted_iota(jnp.int32, g.shape, 0)
        keep = jnp.logical_or(_rank_select(g, i, jrow, nb, axis=0), jrow == i)
        pen = jnp.concatenate([jnp.where(keep, 0.0, NEG), jnp.zeros((LANES - nb, blk), F32)], axis=0)
        qa[hd, :, :LANES] = qm
        qa[hd, :, LANES:] = pen.T.astype(BF16)
        red[hd] = jnp.full((blk, LANES), NEG, F32)

    def sweep(step):
        def group(t, carry):
            for u in range(SWEEP_UNROLL):
                step(SWEEP_UNROLL * t + u)
            return carry

        def single(j, carry):
            step(j)
            return carry

        n_grouped = (i + 1) // SWEEP_UNROLL
        lax.fori_loop(0, n_grouped, group, 0)
        lax.fori_loop(n_grouped * SWEEP_UNROLL, i + 1, single, 0)

    def scores(j):
        dist = jnp.minimum(i - j, N_DIST - 1)
        kt = kft[j]
        for hd in range(n_hd):
            s = _dot(qa[hd], kt) + bias_ref[hd, dist]
            sbuf[hd, j] = s
            red[hd] = jnp.maximum(red[hd], jnp.maximum(s[:, :LANES], s[:, LANES:]))

    sweep(scores)
    for hd in range(n_hd):
        stat[hd] = rep(jnp.max(red[hd], axis=-1, keepdims=True))
        red[hd] = jnp.zeros((blk, LANES), F32)

    def exps(j):
        for hd in range(n_hd):
            e = jnp.exp(sbuf[hd, j] - wide(stat[hd]))
            sbuf[hd, j] = e
            red[hd] += e[:, :LANES] + e[:, LANES:]

    sweep(exps)
    for hd in range(n_hd):
        stat[hd] = rep(1.0 / jnp.sum(red[hd], axis=-1, keepdims=True))
        acc_scr[hd] = jnp.zeros((blk, LANES), F32)

    def weighted(j):
        vj = vb[pl.ds(pl.multiple_of(j * blk, blk), blk), :]
        for hd in range(n_hd):
            p = (sbuf[hd, j] * wide(stat[hd])).astype(BF16)
            acc_scr[hd] += _dot(p, vj)

    sweep(weighted)
    o = acc_scr[0]
    for hd in range(1, n_hd):
        o = jnp.where(lane // HEAD_DIM == hd, acc_scr[hd], o)
    o_ref[0] = o.astype(BF16)


def _moba_prompt(uq, k_all, v_all, layer, bias_tiles):
    b, s, n = uq.shape
    w = n // 2
    hp = w // LANES
    heads_per = LANES // HEAD_DIM
    blk = MOBA_BLOCK
    assert s % blk == 0 and s // blk <= LANES
    return pl.pallas_call(
        _moba_kernel, name="moba_prompt",
        grid=(b, hp, s // blk),
        in_specs=[
            pl.BlockSpec((1, blk, LANES), lambda bi, h, i: (bi, i, hp + h)),
            pl.BlockSpec((1, s, LANES), lambda bi, h, i: (bi, 0, layer * hp + h)),
            pl.BlockSpec((1, s, LANES), lambda bi, h, i: (bi, 0, layer * hp + h)),
            pl.BlockSpec((heads_per, N_DIST, blk, blk), lambda bi, h, i: (h, 0, 0, 0)),
        ],
        out_specs=pl.BlockSpec((1, blk, LANES), lambda bi, h, i: (bi, i, h)),
        out_shape=jax.ShapeDtypeStruct((b, s, w), BF16),
        scratch_shapes=[
            pltpu.VMEM((s // blk, 2 * LANES, blk), BF16), pltpu.VMEM((s, LANES), BF16),
            pltpu.VMEM((s // blk, LANES), F32), pltpu.VMEM((heads_per, blk, 2 * LANES), BF16),
            pltpu.VMEM((heads_per, s // blk, blk, blk), F32), pltpu.VMEM((heads_per, blk, LANES), F32),
            pltpu.VMEM((heads_per, blk, LANES), F32), pltpu.VMEM((heads_per, blk, LANES), F32),
        ],
        compiler_params=_cparams("parallel", "parallel", "arbitrary"),
    )(uq, k_all, v_all, bias_tiles)


def _rnd(a):
    return a.astype(BF16).astype(F32)


DEC_GROUP = 16
PAGES_PER_BLOCK = 2


def _dec_scores_kernel(pt_ref, q_ref, *refs):
    del pt_ref
    k_refs, (bias_ref, s_ref, ks_ref) = refs[:-3], refs[-3:]
    nh = s_ref.shape[2]
    hd = nh * HEAD_DIM
    page = k_refs[0].shape[-1]
    q = (q_ref[0] * HEAD_DIM ** -0.5).astype(BF16)
    ones = jnp.ones((SUBLANES, page), BF16)
    for g in range(s_ref.shape[1]):
        kk = [r[0, 0].reshape(hd, page) for r in k_refs[PAGES_PER_BLOCK * g:PAGES_PER_BLOCK * (g + 1)]]
        s_ref[0, g] = jnp.concatenate([_dot(q, k.astype(BF16)) for k in kk], axis=1) + bias_ref[g]
        hi, lo = _split_bf16(kk[0] + kk[1])
        ks_ref[0, g] = (_dot_nt(ones, hi) + _dot_nt(ones, lo))[0:1]


def _dec_scores(qx, cache_kt, page_table, bias_dec, layer):
    b, nh, hd = qx.shape
    page = cache_kt.shape[4]
    assert MOBA_BLOCK == PAGES_PER_BLOCK * page
    nblk = page_table.shape[1] // PAGES_PER_BLOCK
    grp = math.gcd(DEC_GROUP, nblk)
    n_pg = PAGES_PER_BLOCK * grp
    pg = lambda t: pl.BlockSpec((1, 1, nh, HEAD_DIM, page),
                                lambda bi, j, pt: (pt[bi, n_pg * j + t], layer, 0, 0, 0))
    out = lambda rows, last: pl.BlockSpec((1, grp, rows, last), lambda bi, j, pt: (bi, j, 0, 0))
    return pl.pallas_call(
        _dec_scores_kernel, name="moba_dec_scores",
        grid_spec=pltpu.PrefetchScalarGridSpec(
            num_scalar_prefetch=1,
            grid=(b, nblk // grp),
            in_specs=[pl.BlockSpec((1, nh, hd), lambda bi, j, pt: (bi, 0, 0))]
            + [pg(t) for t in range(n_pg)]
            + [pl.BlockSpec((grp, nh, MOBA_BLOCK), lambda bi, j, pt: (j, 0, 0))],
            out_specs=[out(nh, MOBA_BLOCK), out(1, hd)],
        ),
        out_shape=[jax.ShapeDtypeStruct((b, nblk, nh, MOBA_BLOCK), F32), jax.ShapeDtypeStruct((b, nblk, 1, hd), F32)],
        compiler_params=_cparams("parallel", "arbitrary"),
    )(page_table, qx, *([cache_kt] * n_pg), bias_dec)


def _dec_select_kernel(q_ref, kn_ref, b0_ref, ks_ref, s_ref, p_ref, pown_ref, sel_ref):
    nblk = sel_ref.shape[1]
    blk = MOBA_BLOCK
    q = _rnd(q_ref[...] * HEAD_DIM ** -0.5)
    lane = lax.broadcasted_iota(jnp.int32, sel_ref.shape, 1)
    gate = jnp.zeros(sel_ref.shape, F32)
    for j in range(nblk):
        kmean = _rnd(ks_ref[:, j * HEAD_DIM:(j + 1) * HEAD_DIM] * (1.0 / blk))
        gate = jnp.where(lane == j, jnp.sum(q * kmean, axis=-1, keepdims=True), gate)
    sel = _rank_select(gate, nblk, lane, nblk, axis=1)
    sel_ref[...] = jnp.where(sel, 1.0, 0.0)
    s_own = jnp.sum(q * _rnd(kn_ref[...]), axis=-1, keepdims=True) + b0_ref[:, 0:1]
    m = s_own
    for j in range(nblk):
        mj = jnp.max(s_ref[:, j * blk:(j + 1) * blk], axis=-1, keepdims=True)
        m = jnp.maximum(m, jnp.where(sel[:, j:j + 1], mj, NEG))
    e_own = jnp.exp(s_own - m)
    den = e_own
    for j in range(nblk):
        e = jnp.where(sel[:, j:j + 1], jnp.exp(s_ref[:, j * blk:(j + 1) * blk] - m), 0.0)
        p_ref[:, j * blk:(j + 1) * blk] = e
        den = den + jnp.sum(e, axis=-1, keepdims=True)
    inv = 1.0 / den
    for j in range(nblk):
        p_ref[:, j * blk:(j + 1) * blk] = _rnd(p_ref[:, j * blk:(j + 1) * blk] * inv)
    pown_ref[...] = jnp.broadcast_to(_rnd(e_own * inv), pown_ref.shape)


def _dec_pv_kernel(pt_ref, src_ref, need_ref, p_ref, *refs):
    del pt_ref, src_ref
    v_refs, (pown_ref, vn_ref, o_ref) = refs[:-3], refs[-3:]
    bi = pl.program_id(0)
    j = pl.program_id(1)
    grp = p_ref.shape[1]
    nh = p_ref.shape[2]
    hd = nh * HEAD_DIM
    page = v_refs[0].shape[-1]

    @pl.when(j == 0)
    def _():
        o_ref[0] = pown_ref[0] * _rnd(vn_ref[0])

    for g in range(grp):
        @pl.when(need_ref[bi, j * grp + g] == 1)
        def _(g=g):
            pb = p_ref[0, g].astype(BF16)
            vv = [r[0, 0].reshape(hd, page).astype(BF16) for r in v_refs[PAGES_PER_BLOCK * g:PAGES_PER_BLOCK * (g + 1)]]
            o_all = _dot_nt(pb[:, :page], vv[0]) + _dot_nt(pb[:, page:], vv[1])
            col = lax.broadcasted_iota(jnp.int32, o_all.shape, 1)
            row = lax.broadcasted_iota(jnp.int32, o_all.shape, 0)
            o_ref[0] += jnp.sum(jnp.where(col // HEAD_DIM == row, o_all, 0.0), axis=0, keepdims=True)


def _moba_decode(proj_s, cache_kt, cache_vt, page_table, rel_bias, layer):
    b, n = proj_s.shape
    w = n // 4
    nh = w // HEAD_DIM
    page = cache_kt.shape[4]
    past_len = page_table.shape[1] * page
    assert past_len % MOBA_BLOCK == 0
    nblk = past_len // MOBA_BLOCK
    q = proj_s[:, w:2 * w].reshape(b, nh, HEAD_DIM)
    qx = jnp.einsum('bhd,hg->bhgd', q, jnp.eye(nh, dtype=F32)).reshape(b, nh, w)
    kpos = np.arange(past_len).reshape(nblk, MOBA_BLOCK)
    bias = rel_bias.astype(F32).T[:, _t5_bucket(jnp.asarray(past_len - kpos, jnp.int32))]
    s_p, ks_p = _dec_scores(qx, cache_kt, page_table, jnp.transpose(bias, (1, 0, 2)), layer)
    rows = b * nh
    blk = MOBA_BLOCK
    to_rows = lambda a, last: jnp.transpose(a.reshape(b, nblk, nh, last), (0, 2, 1, 3)).reshape(rows, nblk * last)
    b0 = jnp.broadcast_to(jnp.tile(rel_bias.astype(F32)[_t5_bucket(jnp.zeros((), jnp.int32))], b)[:, None],
                          (rows, LANES))
    f = jax.ShapeDtypeStruct
    p_rows, p_own, sel = pl.pallas_call(
        _dec_select_kernel, name="moba_dec_select",
        out_shape=[f((rows, nblk * blk), F32), f((rows, LANES), F32), f((rows, nblk), F32)],
        compiler_params=pltpu.CompilerParams(vmem_limit_bytes=VMEM_LIMIT),
    )(q.reshape(rows, HEAD_DIM), proj_s[:, 2 * w:3 * w].reshape(rows, HEAD_DIM), b0,
      to_rows(ks_p, HEAD_DIM), to_rows(s_p, blk))

    grp = math.gcd(DEC_GROUP, nblk)
    need = (jnp.max(sel.reshape(b, nh, nblk), axis=1) > 0.5).reshape(b, nblk // grp, grp)
    blk_id = jnp.arange(nblk, dtype=jnp.int32).reshape(1, nblk // grp, grp)
    last_needed = lax.cummax(jnp.where(need, blk_id, -1), axis=1)
    first_needed = jnp.min(jnp.where(need, blk_id, nblk), axis=1, keepdims=True)
    fallback = jnp.where(first_needed < nblk, first_needed, blk_id[:, :1])
    src = jnp.where(last_needed < 0, fallback, last_needed).astype(jnp.int32).reshape(b, nblk)
    p_blk = jnp.transpose(p_rows.reshape(b, nh, nblk, blk), (0, 2, 1, 3))
    p_own_x = jnp.repeat(p_own[:, 0].reshape(b, nh), HEAD_DIM, axis=1).reshape(b, 1, w)
    page_spec = lambda g, off: pl.BlockSpec(
        (1, 1, nh, HEAD_DIM, page),
        lambda bi, j, pt, sr, nd: (pt[bi, PAGES_PER_BLOCK * sr[bi, j * grp + g] + off], layer, 0, 0, 0))
    seq_spec = pl.BlockSpec((1, 1, w), lambda bi, j, pt, sr, nd: (bi, 0, 0))
    att = pl.pallas_call(
        _dec_pv_kernel, name="moba_dec_pv",
        grid_spec=pltpu.PrefetchScalarGridSpec(
            num_scalar_prefetch=3,
            grid=(b, nblk // grp),
            in_specs=[pl.BlockSpec((1, grp, nh, blk), lambda bi, j, pt, sr, nd: (bi, j, 0, 0))]
            + [page_spec(g, off) for g in range(grp) for off in range(PAGES_PER_BLOCK)]
            + [seq_spec, seq_spec],
            out_specs=seq_spec,
        ),
        out_shape=f((b, 1, w), F32),
        compiler_params=_cparams("parallel", "arbitrary"),
    )(page_table, src, need.reshape(b, nblk).astype(jnp.int32), p_blk,
      *([cache_vt] * (grp * PAGES_PER_BLOCK)), p_own_x, proj_s[:, 3 * w:].reshape(b, 1, w))
    return att.reshape(b, w).astype(BF16)


def _tile(m, pref):
    t = min(m, pref)
    while m % t:
        t //= 2
    return t


def _ff_tile(ff, pref):
    best = LANES
    for t in range(LANES, pref + 1, LANES):
        if ff % t == 0:
            best = t
    return best


def kernel(x_prompt, x_sample, cache_k, cache_v, page_table, state_s5_re, state_s5_im, state_pool, rel_bias,
           norm_mix_e, w_in_e, q_norm_e, k_norm_e, s5_a_re, s5_a_im, s5_log_dt, s5_b_re, s5_b_im, s5_c_re,
           s5_c_im, s5_d, s5_w_glu, s5_b_glu, w_out_e, norm_ffn_e, ffn_w_gate, ffn_w_up, ffn_w_down,
           norm_mix_o, pool_w, pool_scale, norm_ffn_o, router_w, router_b, moe_w_gate, moe_w_up, moe_w_down):
    bp, seq, d = x_prompt.shape
    db = x_sample.shape[0]
    assert x_sample.shape[1] == 1
    depth = norm_mix_e.shape[0] + norm_mix_o.shape[0]
    wdt = w_in_e.shape[2] // 4
    nh = wdt // HEAD_DIM
    past_len = page_table.shape[1] * cache_k.shape[2]
    n_exp = router_w.shape[2]
    gn = s5_a_re.shape[1] * s5_a_re.shape[2]

    mp = bp * seq
    xp = x_prompt.reshape(mp, d)
    xs = x_sample.reshape(db, d)
    tm_p = _tile(mp, TOKEN_TILE)
    bias_tiles = _prompt_bias_tiles(rel_bias)
    row = lambda v: v.reshape(1, -1).astype(F32)
    cache_kt = jnp.transpose(cache_k, (0, 1, 3, 4, 2))
    cache_vt = jnp.transpose(cache_v, (0, 1, 3, 4, 2))

    n_even = norm_mix_e.shape[0]
    kp_all = jnp.zeros((mp, n_even * wdt), F32)
    vp_all = jnp.zeros((mp, n_even * wdt), F32)
    ks_all = jnp.zeros((db, n_even * wdt), F32)
    vs_all = jnp.zeros((db, n_even * wdt), F32)
    moe_wg, moe_wu, moe_wd = moe_w_gate.astype(BF16), moe_w_up.astype(BF16), moe_w_down.astype(BF16)
    s5p_l, s5s_re_l, s5s_im_l = [], [], []
    poolp_l, pools_l = [], []
    for layer in range(depth):
        if layer % 2 == 0:
            e = layer // 2
            w_in = w_in_e[e].astype(BF16)
            ones = jnp.ones((wdt,), F32)
            head_gain = jnp.stack([ones, jnp.tile(q_norm_e[e].astype(F32), nh),
                                   jnp.tile(k_norm_e[e].astype(F32), nh), ones]).reshape(4, 1, wdt)
            wb, tabs, lam, wc = _s5_weights(s5_a_re[e], s5_a_im[e], s5_log_dt[e], s5_b_re[e], s5_b_im[e],
                                            s5_c_re[e], s5_c_im[e])
            dsk = row(s5_d[e])
            wgl = s5_w_glu[e].astype(BF16)
            bgl = row(s5_b_glu[e])
            w_out = w_out_e[e].astype(BF16)
            wg, wu, wd = ffn_w_gate[e].astype(BF16), ffn_w_up[e].astype(BF16), ffn_w_down[e].astype(BF16)
            tf = _ff_tile(wg.shape[1], FFN_TILE)

            uq, kp_all, vp_all = _in_proj(xp, row(norm_mix_e[e]), w_in, head_gain, kp_all, vp_all, e, tm_p)
            uq3 = uq.reshape(bp, seq, 2 * wdt)
            s5_out, hfin = _s5_scan(uq3, jnp.zeros((bp, 1, 2 * gn), F32), wb, tabs, wc, dsk, wgl, bgl,
                                    _tile(seq, SCAN_TILE))
            att = _moba_prompt(uq3, kp_all.reshape(bp, seq, -1), vp_all.reshape(bp, seq, -1), e, bias_tiles)
            xp = _mix_ffn(xp, s5_out.reshape(mp, wdt), att.reshape(mp, wdt), w_out, row(norm_ffn_e[e]),
                          wg, wu, wd, tm_p, tf)
            s5p_l.append(hfin.reshape(bp, 2, -1, S5_STATE))

            uq_s, ks_all, vs_all = _in_proj(xs, row(norm_mix_e[e]), w_in, head_gain, ks_all, vs_all, e, db)
            proj_s = jnp.concatenate([uq_s, ks_all[:, e * wdt:(e + 1) * wdt], vs_all[:, e * wdt:(e + 1) * wdt]],
                                     axis=1)
            s5_out_s, hr_s, hi_s = _s5_step(proj_s, state_s5_re[:, e].reshape(db, gn).astype(F32),
                                            state_s5_im[:, e].reshape(db, gn).astype(F32),
                                            wb, lam, wc, dsk, wgl, bgl)
            att_s = _moba_decode(proj_s, cache_kt, cache_vt, page_table, rel_bias, e)
            xs = _mix_ffn(xs, s5_out_s, att_s, w_out, row(norm_ffn_e[e]), wg, wu, wd, db, tf)
            s5s_re_l.append(hr_s.reshape(db, -1, S5_STATE))
            s5s_im_l.append(hi_s.reshape(db, -1, S5_STATE))
        else:
            o = layer // 2
            wp = pool_w[o].astype(BF16)
            sc = row(pool_scale[o])
            wr = jnp.zeros((d, LANES), BF16).at[:, :n_exp].set(router_w[o].astype(BF16))
            br = jnp.zeros((1, LANES), F32).at[:, :n_exp].set(router_b[o].astype(F32))
            tf = _ff_tile(moe_wg.shape[3], EXPERT_TILE)

            xp3, hist_p = _pool_prompt(xp.reshape(bp, seq, d), row(norm_mix_o[o]), wp, sc, _tile(seq, TOKEN_TILE))
            xp = _moe_sparse(xp3.reshape(mp, d), row(norm_ffn_o[o]), wr, br, moe_wg, moe_wu, moe_wd, o, tm_p, tf // 2)
            poolp_l.append(hist_p[:, HALO - POOL_HIST:])

            hist_s = state_pool[:, o].astype(F32)
            xs, hn_s = _pool_step(xs, row(norm_mix_o[o]), jnp.transpose(hist_s, (1, 0, 2)), wp, sc,
                                  min(POOL_HIST, past_len))
            xs = _moe_dense(xs, row(norm_ffn_o[o]), wr, br, moe_wg, moe_wu, moe_wd, o, db, tf)
            pools_l.append(jnp.concatenate([hist_s[:, 1:], hn_s[:, None]], axis=1))

    s5p = jnp.stack(s5p_l, axis=1)

    def per_head(a, lead):
        parts = [a[:, l * wdt:(l + 1) * wdt].reshape(lead + (nh, HEAD_DIM)) for l in range(n_even)]
        return jnp.stack(parts, axis=len(lead))

    return (xp.reshape(bp, seq, d), xs.reshape(db, 1, d),
            per_head(kp_all, (bp, seq)), per_head(vp_all, (bp, seq)),
            per_head(ks_all, (db, 1)), per_head(vs_all, (db, 1)),
            s5p[:, :, 0], s5p[:, :, 1],
            jnp.stack(s5s_re_l, axis=1), jnp.stack(s5s_im_l, axis=1),
            jnp.stack(poolp_l, axis=1), jnp.stack(pools_l, axis=1))
```

```python
import functools
import math

import jax
import jax.numpy as jnp
import numpy as np
from jax import lax
from jax.experimental import pallas as pl
from jax.experimental.pallas import tpu as pltpu

F32 = jnp.float32
BF16 = jnp.bfloat16
EPS = 1e-6
NEG = -1e30

S5_STATE = 64
HEAD_DIM = 64
MOBA_BLOCK = 256
MOBA_TOPK = 3
REL_BUCKETS = 32
REL_MAX_DIST = 1024
POOL_WINDOWS = (2, 4, 8, 16)
POOL_HIST = max(POOL_WINDOWS) - 1
TOP_K_EXPERTS = 2

LANES = 128
SUBLANES = 8
VMEM_LIMIT = 48 * 1024 * 1024

TOKEN_TILE = 512
SCAN_TILE = 1024
FFN_TILE = 1536
EXPERT_TILE = 1792


def _cparams(*sem):
    return pltpu.CompilerParams(dimension_semantics=sem, vmem_limit_bytes=VMEM_LIMIT)


def _rmsnorm(x, g):
    ms = jnp.mean(x * x, axis=-1, keepdims=True)
    return x * lax.rsqrt(ms + EPS) * g


def _split_bf16(x):
    hi = x.astype(BF16)
    lo = (x - hi.astype(F32)).astype(BF16)
    return hi, lo


def _dot(a, b):
    return jnp.dot(a, b, preferred_element_type=F32)


def _dot_nt(a, b):
    return lax.dot_general(a, b, (((1,), (1,)), ((), ())), preferred_element_type=F32)


def _silu(x):
    return x * jax.nn.sigmoid(x)


def _in_proj_kernel(x_ref, g_ref, w_ref, hn_ref, bd_ref, k_in, v_in, uq_ref, k_ref, v_ref, h_scr):
    del k_in, v_in
    j = pl.program_id(1)

    @pl.when(j == 0)
    def _():
        h_scr[...] = _rmsnorm(x_ref[...], g_ref[...]).astype(BF16)

    y = _dot(h_scr[...], w_ref[...])

    def head_normed():
        hi, lo = _split_bf16(y * y)
        ms = _dot(hi, bd_ref[...]) + _dot(lo, bd_ref[...])
        return y * lax.rsqrt(ms + EPS) * hn_ref[0]

    @pl.when(j == 0)
    def _():
        uq_ref[...] = y

    @pl.when(j == 1)
    def _():
        uq_ref[...] = head_normed()

    @pl.when(j == 2)
    def _():
        k_ref[...] = head_normed()

    @pl.when(j == 3)
    def _():
        v_ref[...] = y


def _in_proj(x, g, w_bf, head_gain, k_all, v_all, layer, tm):
    m, d = x.shape
    wdt = w_bf.shape[1] // 4
    bd = np.kron(np.eye(wdt // HEAD_DIM), np.full((HEAD_DIM, HEAD_DIM), 1.0 / HEAD_DIM))
    bd = jnp.asarray(bd, BF16)
    kv_spec = pl.BlockSpec((tm, wdt), lambda i, j: (i, layer))
    return pl.pallas_call(
        _in_proj_kernel, name="in_proj",
        grid=(m // tm, 4),
        in_specs=[
            pl.BlockSpec((tm, d), lambda i, j: (i, 0)),
            pl.BlockSpec((1, d), lambda i, j: (0, 0)),
            pl.BlockSpec((d, wdt), lambda i, j: (0, j)),
            pl.BlockSpec((1, 1, wdt), lambda i, j: (j, 0, 0)),
            pl.BlockSpec((wdt, wdt), lambda i, j: (0, 0)),
            pl.BlockSpec(memory_space=pl.ANY),
            pl.BlockSpec(memory_space=pl.ANY),
        ],
        out_specs=[pl.BlockSpec((tm, wdt), lambda i, j: (i, jnp.minimum(j, 1))), kv_spec, kv_spec],
        out_shape=[jax.ShapeDtypeStruct((m, 2 * wdt), F32), jax.ShapeDtypeStruct(k_all.shape, F32),
                   jax.ShapeDtypeStruct(v_all.shape, F32)],
        input_output_aliases={5: 1, 6: 2},
        scratch_shapes=[pltpu.VMEM((tm, d), BF16)],
        compiler_params=_cparams("parallel", "arbitrary"),
    )(x, g, w_bf, head_gain, bd, k_all, v_all)


def _mix_ffn_kernel(x_ref, a_ref, b_ref, wo_ref, g_ref, wg_ref, wu_ref, wd_ref, o_ref,
                    x1_scr, h_scr, acc_scr):
    f = pl.program_id(1)
    half = a_ref.shape[1]

    @pl.when(f == 0)
    def _():
        x1 = (x_ref[...] + _dot(a_ref[...], wo_ref[:half, :]) + _dot(b_ref[...], wo_ref[half:, :]))
        x1_scr[...] = x1
        h_scr[...] = _rmsnorm(x1, g_ref[...]).astype(BF16)
        acc_scr[...] = jnp.zeros_like(acc_scr)

    h = h_scr[...]
    act = (_silu(_dot(h, wg_ref[...])) * _dot(h, wu_ref[...])).astype(BF16)
    acc_scr[...] += _dot(act, wd_ref[...])

    @pl.when(f == pl.num_programs(1) - 1)
    def _():
        o_ref[...] = x1_scr[...] + acc_scr[...]


def _mix_ffn(x, a, b, wo_bf, g, wg_bf, wu_bf, wd_bf, tm, tf):
    m, d = x.shape
    half = a.shape[1]
    ff = wg_bf.shape[1]
    return pl.pallas_call(
        _mix_ffn_kernel, name="mix_ffn",
        grid=(m // tm, ff // tf),
        in_specs=[
            pl.BlockSpec((tm, d), lambda i, f: (i, 0)),
            pl.BlockSpec((tm, half), lambda i, f: (i, 0)),
            pl.BlockSpec((tm, half), lambda i, f: (i, 0)),
            pl.BlockSpec((2 * half, d), lambda i, f: (0, 0)),
            pl.BlockSpec((1, d), lambda i, f: (0, 0)),
            pl.BlockSpec((d, tf), lambda i, f: (0, f)),
            pl.BlockSpec((d, tf), lambda i, f: (0, f)),
            pl.BlockSpec((tf, d), lambda i, f: (f, 0)),
        ],
        out_specs=pl.BlockSpec((tm, d), lambda i, f: (i, 0)),
        out_shape=jax.ShapeDtypeStruct((m, d), F32),
        scratch_shapes=[pltpu.VMEM((tm, d), F32), pltpu.VMEM((tm, d), BF16), pltpu.VMEM((tm, d), F32)],
        compiler_params=_cparams("parallel", "arbitrary"),
    )(x, a, b, wo_bf, g, wg_bf, wu_bf, wd_bf)


def _route(h, wr, br, n_exp):
    logits = _dot(h.astype(BF16), wr) + br
    lane = lax.broadcasted_iota(jnp.int32, logits.shape, 1)
    logits = jnp.where(lane < n_exp, logits, NEG)
    m1 = jnp.max(logits, axis=-1, keepdims=True)
    i1 = jnp.min(jnp.where(logits == m1, lane, LANES), axis=-1, keepdims=True)
    rest = jnp.where(lane == i1, NEG, logits)
    m2 = jnp.max(rest, axis=-1, keepdims=True)
    i2 = jnp.min(jnp.where(rest == m2, lane, LANES), axis=-1, keepdims=True)
    e2 = jnp.exp(m2 - m1)
    g1 = 1.0 / (1.0 + e2)
    g2 = e2 / (1.0 + e2)
    gate = jnp.where(lane == i1, g1, 0.0) + jnp.where(lane == i2, g2, 0.0)
    return gate, jnp.logical_or(lane == i1, lane == i2)


def _moe_dense_kernel(x_ref, g_ref, wr_ref, br_ref, wg_ref, wu_ref, wd_ref, o_ref,
                      h_scr, gate_scr, acc_scr, *, n_exp):
    e = pl.program_id(1)
    f = pl.program_id(2)

    @pl.when(jnp.logical_and(e == 0, f == 0))
    def _():
        h = _rmsnorm(x_ref[...], g_ref[...])
        h_scr[...] = h.astype(BF16)
        gate_scr[...] = _route(h, wr_ref[...], br_ref[...], n_exp)[0]
        acc_scr[...] = jnp.zeros_like(acc_scr)

    h = h_scr[...]
    lane = lax.broadcasted_iota(jnp.int32, gate_scr.shape, 1)
    w_e = jnp.sum(jnp.where(lane == e, gate_scr[...], 0.0), axis=-1, keepdims=True)
    act = (_silu(_dot(h, wg_ref[0, 0])) * _dot(h, wu_ref[0, 0])).astype(BF16)
    acc_scr[...] += w_e * _dot(act, wd_ref[0, 0])

    @pl.when(jnp.logical_and(e == n_exp - 1, f == pl.num_programs(2) - 1))
    def _():
        o_ref[...] = x_ref[...] + acc_scr[...]


def _moe_dense(x, g, wr_pad, br_pad, wg_bf, wu_bf, wd_bf, layer, tm, tf):
    m, d = x.shape
    _, n_exp, _, ff = wg_bf.shape
    return pl.pallas_call(
        functools.partial(_moe_dense_kernel, n_exp=n_exp), name="moe_dense",
        grid=(m // tm, n_exp, ff // tf),
        in_specs=[
            pl.BlockSpec((tm, d), lambda i, e, f: (i, 0)),
            pl.BlockSpec((1, d), lambda i, e, f: (0, 0)),
            pl.BlockSpec((d, LANES), lambda i, e, f: (0, 0)),
            pl.BlockSpec((1, LANES), lambda i, e, f: (0, 0)),
            pl.BlockSpec((1, 1, d, tf), lambda i, e, f: (layer, e, 0, f)),
            pl.BlockSpec((1, 1, d, tf), lambda i, e, f: (layer, e, 0, f)),
            pl.BlockSpec((1, 1, tf, d), lambda i, e, f: (layer, e, f, 0)),
        ],
        out_specs=pl.BlockSpec((tm, d), lambda i, e, f: (i, 0)),
        out_shape=jax.ShapeDtypeStruct((m, d), F32),
        scratch_shapes=[pltpu.VMEM((tm, d), BF16), pltpu.VMEM((tm, LANES), F32), pltpu.VMEM((tm, d), F32)],
        compiler_params=_cparams("parallel", "arbitrary", "arbitrary"),
    )(x, g, wr_pad, br_pad, wg_bf, wu_bf, wd_bf)


def _moe_route_kernel(x_ref, g_ref, wr_ref, br_ref, h_ref, gate_ref, pos_ref, post_ref, cb_ref, tot_ref, carry,
                      *, n_exp):
    i = pl.program_id(0)
    tm = x_ref.shape[0]

    @pl.when(i == 0)
    def _():
        carry[...] = jnp.zeros_like(carry)

    h = _rmsnorm(x_ref[...], g_ref[...])
    h_ref[...] = h.astype(BF16)
    gate, chosen = _route(h, wr_ref[...], br_ref[...], n_exp)
    gate_ref[...] = gate
    onehot = jnp.where(chosen, 1.0, 0.0)
    r = lax.broadcasted_iota(jnp.int32, (tm, tm), 0)
    c = lax.broadcasted_iota(jnp.int32, (tm, tm), 1)
    before = _dot(jnp.where(c < r, 1.0, 0.0).astype(BF16), onehot.astype(BF16))
    start = carry[...]
    pos = jnp.where(chosen, before + start, -1.0)
    pos_ref[...] = pos
    post_ref[...] = pos.T[:post_ref.shape[0], :]
    cb_ref[0] = start
    total = start + jnp.sum(onehot, axis=0, keepdims=True)
    carry[...] = total
    tot_ref[...] = total


def _moe_route(x, g, wr_bf, br, n_exp, tm):
    m, d = x.shape
    nt = m // tm
    ne8 = -(-n_exp // SUBLANES) * SUBLANES
    f = jax.ShapeDtypeStruct
    return pl.pallas_call(
        functools.partial(_moe_route_kernel, n_exp=n_exp), name="moe_route",
        grid=(nt,),
        in_specs=[
            pl.BlockSpec((tm, d), lambda i: (i, 0)),
            pl.BlockSpec((1, d), lambda i: (0, 0)),
            pl.BlockSpec((d, LANES), lambda i: (0, 0)),
            pl.BlockSpec((1, LANES), lambda i: (0, 0)),
        ],
        out_specs=[
            pl.BlockSpec((tm, d), lambda i: (i, 0)),
            pl.BlockSpec((tm, LANES), lambda i: (i, 0)),
            pl.BlockSpec((tm, LANES), lambda i: (i, 0)),
            pl.BlockSpec((ne8, tm), lambda i: (0, i)),
            pl.BlockSpec((1, 1, LANES), lambda i: (i, 0, 0)),
            pl.BlockSpec((1, LANES), lambda i: (0, 0)),
        ],
        out_shape=[f((m, d), BF16), f((m, LANES), F32), f((m, LANES), F32), f((ne8, m), F32),
                   f((nt, 1, LANES), F32), f((1, LANES), F32)],
        scratch_shapes=[pltpu.VMEM((1, LANES), F32)],
        compiler_params=_cparams("arbitrary"),
    )(x, g, wr_bf, br)


def _moe_gather_kernel(pc_ref, ps_ref, pe_ref, pfirst_ref, pvalid_ref, off_ref, h_ref, post_ref, o_ref, acc):
    k = pl.program_id(0)
    rc, tm = o_ref.shape[0], h_ref.shape[0]

    @pl.when(pfirst_ref[k] == 1)
    def _():
        acc[...] = jnp.zeros_like(acc)

    @pl.when(pvalid_ref[k] == 1)
    def _():
        e = pe_ref[k]
        dest = post_ref[pl.ds(e, 1), :] + off_ref[e].astype(F32)
        rows = (pc_ref[k] * rc + lax.broadcasted_iota(jnp.int32, (rc, tm), 0)).astype(F32)
        acc[...] += _dot(jnp.where(dest == rows, 1.0, 0.0).astype(BF16), h_ref[...])

    o_ref[...] = acc[...].astype(BF16)


def _moe_ffn_kernel(ce_ref, cv_ref, x_ref, wg_ref, wu_ref, wd_ref, y_ref, *, tf):
    c = pl.program_id(0)

    @pl.when(cv_ref[c] == 1)
    def _():
        x = x_ref[...]
        acc = jnp.zeros(y_ref.shape, F32)
        for f0 in range(0, wg_ref.shape[3], tf):
            act = (_silu(_dot(x, wg_ref[0, 0, :, f0:f0 + tf])) * _dot(x, wu_ref[0, 0, :, f0:f0 + tf])).astype(BF16)
            acc = acc + _dot(act, wd_ref[0, 0, f0:f0 + tf, :])
        y_ref[...] = acc.astype(BF16)

    @pl.when(cv_ref[c] == 0)
    def _():
        y_ref[...] = jnp.zeros_like(y_ref)


def _moe_combine_kernel(qs_ref, qc_ref, qe_ref, qfirst_ref, qvalid_ref, off_ref, x_ref, y_ref, pos_ref, gate_ref,
                        o_ref):
    k = pl.program_id(0)
    tm, rc = x_ref.shape[0], y_ref.shape[0]

    @pl.when(qfirst_ref[k] == 1)
    def _():
        o_ref[...] = x_ref[...]

    @pl.when(qvalid_ref[k] == 1)
    def _():
        e = qe_ref[k]
        lane = lax.broadcasted_iota(jnp.int32, pos_ref.shape, 1)
        pos_e = jnp.sum(jnp.where(lane == e, pos_ref[...], 0.0), axis=-1, keepdims=True)
        gate_e = jnp.sum(jnp.where(lane == e, gate_ref[...], 0.0), axis=-1, keepdims=True)
        dest = pos_e + off_ref[e].astype(F32)
        cols = (qc_ref[k] * rc + lax.broadcasted_iota(jnp.int32, (tm, rc), 1)).astype(F32)
        pick = jnp.where(dest == cols, 1.0, 0.0).astype(BF16)
        o_ref[...] += gate_e * _dot(pick, y_ref[...])


def _pair_list(inter, n_pairs_max):
    nb = inter.shape[1]
    flat = inter.reshape(-1)
    n = jnp.sum(flat.astype(jnp.int32))
    idx = jnp.nonzero(flat, size=n_pairs_max, fill_value=0)[0].astype(jnp.int32)
    k = jnp.arange(n_pairs_max, dtype=jnp.int32)
    idx = jnp.where(k < n, idx, idx[jnp.maximum(n - 1, 0)])
    a, b = idx // nb, idx % nb
    valid = k < n
    first = jnp.logical_and(valid, jnp.logical_or(k == 0, a != jnp.roll(a, 1)))
    return a, b, valid.astype(jnp.int32), first.astype(jnp.int32)


ROW_CHUNK = 256


def _moe_sparse(x, g, wr_bf, br, wg_bf, wu_bf, wd_bf, layer, tm, tf):
    m, d = x.shape
    _, n_exp, _, ff = wg_bf.shape
    nt = m // tm
    rc = ROW_CHUNK
    h, gate, pos, pos_t, cb, tot = _moe_route(x, g, wr_bf, br, n_exp, tm)

    cnt = tot[0, :n_exp].astype(jnp.int32)
    cbi = jnp.concatenate([cb[:, 0, :n_exp], tot[:, :n_exp]], axis=0).astype(jnp.int32)
    gsz = (cnt + rc - 1) // rc * rc
    ends = jnp.cumsum(gsz)
    off = (ends - gsz).astype(jnp.int32)
    n_chunks = TOP_K_EXPERTS * m // rc + n_exp
    cstart = jnp.arange(n_chunks, dtype=jnp.int32) * rc
    ce = jnp.minimum(jnp.searchsorted(ends, cstart, side='right'), n_exp - 1).astype(jnp.int32)
    cv = cstart < ends[-1]
    per_tile = cbi.T[ce]
    start = off[ce][:, None] + per_tile[:, :-1]
    end = off[ce][:, None] + per_tile[:, 1:]
    inter = (cv[:, None] & (end > start) & (start < cstart[:, None] + rc) & (end > cstart[:, None]))
    n_pairs = n_chunks + n_exp * nt
    unused = jnp.logical_and(jnp.logical_not(cv)[:, None], jnp.arange(nt)[None, :] == 0)
    pc, ps, plisted, pfirst = _pair_list(jnp.logical_or(inter, unused), n_pairs + n_exp)
    pvalid = plisted * inter[pc, ps].astype(jnp.int32)
    qs, qc, qvalid, qfirst = _pair_list(inter.T, n_pairs)

    xs = pl.pallas_call(
        _moe_gather_kernel, name="moe_gather",
        grid_spec=pltpu.PrefetchScalarGridSpec(
            num_scalar_prefetch=6,
            grid=(n_pairs + n_exp,),
            in_specs=[
                pl.BlockSpec((tm, d), lambda k, pc, ps, *_: (ps[k], 0)),
                pl.BlockSpec((pos_t.shape[0], tm), lambda k, pc, ps, *_: (0, ps[k])),
            ],
            out_specs=pl.BlockSpec((rc, d), lambda k, pc, *_: (pc[k], 0)),
            scratch_shapes=[pltpu.VMEM((rc, d), F32)],
        ),
        out_shape=jax.ShapeDtypeStruct((n_chunks * rc, d), BF16),
        compiler_params=_cparams("arbitrary"),
    )(pc, ps, ce[pc], pfirst, pvalid, off, h, pos_t)

    cvi = cv.astype(jnp.int32)
    wspec = lambda shape: pl.BlockSpec((1,) + shape, lambda c, ce, cv: (layer, ce[c], 0, 0),
                                       pipeline_mode=pl.Buffered(1))
    ys = pl.pallas_call(
        functools.partial(_moe_ffn_kernel, tf=tf), name="moe_ffn",
        grid_spec=pltpu.PrefetchScalarGridSpec(
            num_scalar_prefetch=2,
            grid=(n_chunks,),
            in_specs=[
                pl.BlockSpec((rc, d), lambda c, ce, cv: (c * cv[c], 0)),
                wspec((1, d, ff)), wspec((1, d, ff)), wspec((1, ff, d)),
            ],
            out_specs=pl.BlockSpec((rc, d), lambda c, ce, cv: (c, 0)),
        ),
        out_shape=jax.ShapeDtypeStruct((n_chunks * rc, d), BF16),
        compiler_params=_cparams("arbitrary"),
    )(ce, cvi, xs, wg_bf, wu_bf, wd_bf)

    return pl.pallas_call(
        _moe_combine_kernel, name="moe_combine",
        grid_spec=pltpu.PrefetchScalarGridSpec(
            num_scalar_prefetch=6,
            grid=(n_pairs,),
            in_specs=[
                pl.BlockSpec((tm, d), lambda k, qs, qc, *_: (qs[k], 0)),
                pl.BlockSpec((rc, d), lambda k, qs, qc, *_: (qc[k], 0)),
                pl.BlockSpec((tm, LANES), lambda k, qs, *_: (qs[k], 0)),
                pl.BlockSpec((tm, LANES), lambda k, qs, *_: (qs[k], 0)),
            ],
            out_specs=pl.BlockSpec((tm, d), lambda k, qs, *_: (qs[k], 0)),
        ),
        out_shape=jax.ShapeDtypeStruct((m, d), F32),
        compiler_params=_cparams("arbitrary"),
    )(qs, qc, ce[qc], qfirst, qvalid, off, x, ys, pos, gate)


HALO = 16


def _pool_kernel(x_ref, g_ref, wp_ref, sc_ref, o_ref, hist_ref, hbuf):
    t = pl.program_id(1)
    tm = x_ref.shape[1]
    gw = wp_ref.shape[1]

    @pl.when(t == 0)
    def _():
        hbuf[0:HALO, :] = jnp.zeros((HALO, hbuf.shape[1]), F32)

    @pl.when(t > 0)
    def _():
        hbuf[0:HALO, :] = hbuf[tm:tm + HALO, :]

    x = x_ref[0]
    hn = _rmsnorm(x, g_ref[...])
    hbuf[HALO:HALO + tm, :] = hn
    pos = t * tm + lax.broadcasted_iota(jnp.int32, (tm, 1), 0)
    ys = []
    for gi, w in enumerate(POOL_WINDOWS):
        sl = slice(gi * gw, (gi + 1) * gw)
        tot = hn[:, sl]
        for i in range(1, w):
            tot = tot + hbuf[HALO - i:HALO - i + tm, sl]
        cnt = jnp.minimum(pos + 1, w).astype(F32)
        pooled = tot / cnt - hn[:, sl]
        ys.append(_dot(pooled.astype(BF16), wp_ref[gi]))
    o_ref[0] = x + jnp.concatenate(ys, axis=1) * sc_ref[...]

    @pl.when(t == pl.num_programs(1) - 1)
    def _():
        hist_ref[0] = hbuf[tm:tm + HALO, :]


def _pool_prompt(x, g, wp_bf, sc, tm):
    b, s, d = x.shape
    ng, gw, _ = wp_bf.shape
    return pl.pallas_call(
        _pool_kernel, name="pool_prompt",
        grid=(b, s // tm),
        in_specs=[
            pl.BlockSpec((1, tm, d), lambda i, t: (i, t, 0)),
            pl.BlockSpec((1, d), lambda i, t: (0, 0)),
            pl.BlockSpec((ng, gw, gw), lambda i, t: (0, 0, 0)),
            pl.BlockSpec((1, d), lambda i, t: (0, 0)),
        ],
        out_specs=[
            pl.BlockSpec((1, tm, d), lambda i, t: (i, t, 0)),
            pl.BlockSpec((1, HALO, d), lambda i, t: (i, 0, 0)),
        ],
        out_shape=[jax.ShapeDtypeStruct((b, s, d), F32), jax.ShapeDtypeStruct((b, HALO, d), F32)],
        scratch_shapes=[pltpu.VMEM((HALO + tm, d), F32)],
        compiler_params=_cparams("parallel", "arbitrary"),
    )(x, g, wp_bf, sc)


def _pool_step_kernel(x_ref, g_ref, hist_ref, wp_ref, sc_ref, o_ref, hn_ref, *, n_valid):
    gw = wp_ref.shape[1]
    x = x_ref[...]
    hn = _rmsnorm(x, g_ref[...])
    hn_ref[...] = hn
    ys = []
    for gi, w in enumerate(POOL_WINDOWS):
        sl = slice(gi * gw, (gi + 1) * gw)
        tot = hn[:, sl]
        for i in range(1, min(w, n_valid + 1)):
            tot = tot + hist_ref[POOL_HIST - i][:, sl]
        pooled = tot / float(min(w, n_valid + 1)) - hn[:, sl]
        ys.append(_dot(pooled.astype(BF16), wp_ref[gi]))
    o_ref[...] = x + jnp.concatenate(ys, axis=1) * sc_ref[...]


def _pool_step(x, g, hist_t, wp_bf, sc, n_valid):
    b, d = x.shape
    return pl.pallas_call(
        functools.partial(_pool_step_kernel, n_valid=n_valid), name="pool_step",
        out_shape=[jax.ShapeDtypeStruct((b, d), F32), jax.ShapeDtypeStruct((b, d), F32)],
        compiler_params=pltpu.CompilerParams(vmem_limit_bytes=VMEM_LIMIT),
    )(x, g, hist_t, wp_bf, sc)


N_POW = SUBLANES


def _s5_prep_kernel(ar_ref, ai_ref, ldt_ref, br_ref, bi_ref, pwr_ref, pwi_ref, bbr_ref, bbi_ref):
    ar = ar_ref[...]
    ai = ai_ref[...]
    dt = jnp.exp(ldt_ref[...])
    mag = jnp.exp(ar * dt)
    lr = mag * jnp.cos(ai * dt)
    li = mag * jnp.sin(ai * dt)
    den = ar * ar + ai * ai
    rr = ((lr - 1.0) * ar + li * ai) / den
    ri = (li * ar - (lr - 1.0) * ai) / den
    br = br_ref[...]
    bi = bi_ref[...]
    bbr_ref[...] = rr * br - ri * bi
    bbi_ref[...] = rr * bi + ri * br
    cr, ci = lr, li
    for k in range(N_POW):
        pwr_ref[k] = cr
        pwi_ref[k] = ci
        cr, ci = cr * lr - ci * li, cr * li + ci * lr


def _s5_prep(a_re, a_im, log_dt, b_re, b_im):
    g, n = a_re.shape
    p = b_re.shape[2]
    f = jax.ShapeDtypeStruct
    return pl.pallas_call(
        _s5_prep_kernel, name="s5_prep",
        out_shape=[f((N_POW, g, 1, n), F32), f((N_POW, g, 1, n), F32), f((g, p, n), F32), f((g, p, n), F32)],
    )(a_re[:, None, :], a_im[:, None, :], log_dt[:, None, None],
      jnp.transpose(b_re, (0, 2, 1)), jnp.transpose(b_im, (0, 2, 1)))


def _s5_weights(a_re, a_im, log_dt, b_re, b_im, c_re, c_im):
    g, n = a_re.shape
    p = b_re.shape[2]
    gn = g * n
    pwr, pwi, bbr, bbi = _s5_prep(a_re, a_im, log_dt, b_re, b_im)
    pwr = pwr.reshape(N_POW, gn)
    pwi = pwi.reshape(N_POW, gn)
    row = np.arange(SUBLANES)[:, None]
    tabs = []
    for k in (1, 2, 4):
        tabs += [jnp.where(row >= k, pwr[k - 1][None], 0.0), jnp.where(row >= k, pwi[k - 1][None], 0.0)]
    tabs += [pwr, pwi]
    tabs = jnp.stack(tabs)
    lam = jnp.stack([pwr[0], pwi[0]])
    n_tiles = gn // LANES
    g_tile = LANES // n
    g_chunk = LANES // p
    place = np.zeros((n_tiles, g_chunk, g_tile), np.float32)
    for j in range(n_tiles):
        for i in range(g_tile):
            place[j, (j * g_tile + i) % g_chunk, i] = 1.0
    bb = jnp.stack([bbr, bbi]).reshape(2, n_tiles, g_tile, p, n)
    wb = jnp.einsum('cjipn,jli->jlpcin', bb, place).reshape(n_tiles, LANES, 2 * LANES).astype(BF16)
    n_out = g * p // LANES
    cc = jnp.stack([c_re, -c_im]).reshape(2, n_out, g_chunk, p, n)
    wc = jnp.einsum('cmlpn,lk->mclnkp', cc, np.eye(g_chunk, dtype=np.float32))
    wc = wc.reshape(n_out, 2 * g_chunk * n, LANES).astype(BF16)
    return wb, tabs, lam, wc


def _s5_drive(u, wb_ref, bur, bui, row0):
    t = u.shape[0]
    ub = u.astype(BF16)
    n_tiles = wb_ref.shape[0]
    u_per_tile = u.shape[1] // n_tiles
    for j in range(n_tiles):
        c0 = (j * u_per_tile) // LANES * LANES
        r = _dot(ub[:, c0:c0 + LANES], wb_ref[j])
        bur[row0:row0 + t, j * LANES:(j + 1) * LANES] = r[:, :LANES]
        bui[row0:row0 + t, j * LANES:(j + 1) * LANES] = r[:, LANES:]


def _s5_readout(u, hr, hi, wc_ref, d_ref, wgl_ref, bgl_ref):
    hrb = hr.astype(BF16)
    hib = hi.astype(BF16)
    kk = wc_ref.shape[1] // 2
    ys = []
    for m in range(wc_ref.shape[0]):
        ys.append(_dot(hrb[:, m * kk:(m + 1) * kk], wc_ref[m, :kk, :])
                  + _dot(hib[:, m * kk:(m + 1) * kk], wc_ref[m, kk:, :]))
    y = jnp.concatenate(ys, axis=1) + d_ref[...] * u
    z = jax.nn.gelu(y, approximate=True)
    return z * jax.nn.sigmoid(_dot(z.astype(BF16), wgl_ref[...]) + bgl_ref[...])


def _s5_scan_kernel(u_ref, h0_ref, wb_ref, tab_ref, wc_ref, d_ref, wgl_ref, bgl_ref, o_ref, hf_ref, bur, bui):
    t = pl.program_id(1)
    tt = u_ref.shape[1]
    gn = bur.shape[1]
    c0 = SUBLANES

    @pl.when(t == 0)
    def _():
        bur[0:c0, :] = jnp.broadcast_to(h0_ref[0, :, :gn], (c0, gn))
        bui[0:c0, :] = jnp.broadcast_to(h0_ref[0, :, gn:], (c0, gn))

    @pl.when(t > 0)
    def _():
        bur[0:c0, :] = bur[tt:tt + c0, :]
        bui[0:c0, :] = bui[tt:tt + c0, :]

    u = u_ref[0]
    _s5_drive(u, wb_ref, bur, bui, c0)

    def block(b, carry):
        base = pl.multiple_of(b * SUBLANES, SUBLANES)
        cr = bur[pl.ds(base + c0 - 1, 1), :]
        ci = bui[pl.ds(base + c0 - 1, 1), :]
        xr = bur[pl.ds(base + c0, SUBLANES), :]
        xi = bui[pl.ds(base + c0, SUBLANES), :]
        for s, k in enumerate((1, 2, 4)):
            ar = tab_ref[2 * s]
            ai = tab_ref[2 * s + 1]
            sr = pltpu.roll(xr, k, axis=0)
            si = pltpu.roll(xi, k, axis=0)
            xr, xi = xr + ar * sr - ai * si, xi + ar * si + ai * sr
        pr = tab_ref[6]
        pi = tab_ref[7]
        bur[pl.ds(base + c0, SUBLANES), :] = xr + pr * cr - pi * ci
        bui[pl.ds(base + c0, SUBLANES), :] = xi + pr * ci + pi * cr
        return carry

    lax.fori_loop(0, tt // SUBLANES, block, 0)
    o_ref[0] = _s5_readout(u, bur[c0:c0 + tt, :], bui[c0:c0 + tt, :], wc_ref, d_ref, wgl_ref, bgl_ref).astype(BF16)

    @pl.when(t == pl.num_programs(1) - 1)
    def _():
        hf_ref[0] = jnp.concatenate([bur[tt + c0 - 1:tt + c0, :], bui[tt + c0 - 1:tt + c0, :]], axis=1)


def _s5_scan(proj, h0, wb, tabs, wc, dsk, wgl_bf, bgl, tt):
    b, s, _ = proj.shape
    w = dsk.shape[1]
    gn = tabs.shape[2]
    full = lambda a: pl.BlockSpec(a.shape, lambda i, t: (0,) * a.ndim)
    return pl.pallas_call(
        _s5_scan_kernel, name="s5_scan",
        grid=(b, s // tt),
        in_specs=[
            pl.BlockSpec((1, tt, w), lambda i, t: (i, t, 0)),
            pl.BlockSpec((1, 1, 2 * gn), lambda i, t: (i, 0, 0)),
            full(wb), full(tabs), full(wc), full(dsk), full(wgl_bf), full(bgl),
        ],
        out_specs=[
            pl.BlockSpec((1, tt, w), lambda i, t: (i, t, 0)),
            pl.BlockSpec((1, 1, 2 * gn), lambda i, t: (i, 0, 0)),
        ],
        out_shape=[jax.ShapeDtypeStruct((b, s, w), BF16), jax.ShapeDtypeStruct((b, 1, 2 * gn), F32)],
        scratch_shapes=[pltpu.VMEM((tt + SUBLANES, gn), F32), pltpu.VMEM((tt + SUBLANES, gn), F32)],
        compiler_params=_cparams("parallel", "arbitrary"),
    )(proj, h0, wb, tabs, wc, dsk, wgl_bf, bgl)


def _s5_step_kernel(u_ref, h0r_ref, h0i_ref, wb_ref, lam_ref, wc_ref, d_ref, wgl_ref, bgl_ref,
                    o_ref, hr_ref, hi_ref, bur, bui):
    u = u_ref[...]
    _s5_drive(u, wb_ref, bur, bui, 0)
    lr = lam_ref[0:1, :]
    li = lam_ref[1:2, :]
    h0r = h0r_ref[...]
    h0i = h0i_ref[...]
    hr = bur[...] + lr * h0r - li * h0i
    hi = bui[...] + lr * h0i + li * h0r
    hr_ref[...] = hr
    hi_ref[...] = hi
    o_ref[...] = _s5_readout(u, hr, hi, wc_ref, d_ref, wgl_ref, bgl_ref).astype(BF16)


def _s5_step(proj, h0r, h0i, wb, lam, wc, dsk, wgl_bf, bgl):
    b = proj.shape[0]
    w = dsk.shape[1]
    gn = lam.shape[1]
    full = lambda a: pl.BlockSpec(a.shape, lambda i: (0,) * a.ndim)
    return pl.pallas_call(
        _s5_step_kernel, name="s5_step",
        grid=(1,),
        in_specs=[pl.BlockSpec((b, w), lambda i: (0, 0)), full(h0r), full(h0i), full(wb), full(lam), full(wc),
                  full(dsk), full(wgl_bf), full(bgl)],
        out_specs=[pl.BlockSpec((b, w), lambda i: (0, 0)), pl.BlockSpec((b, gn), lambda i: (0, 0)),
                   pl.BlockSpec((b, gn), lambda i: (0, 0))],
        out_shape=[jax.ShapeDtypeStruct((b, w), BF16), jax.ShapeDtypeStruct((b, gn), F32),
                   jax.ShapeDtypeStruct((b, gn), F32)],
        scratch_shapes=[pltpu.VMEM((b, gn), F32), pltpu.VMEM((b, gn), F32)],
        compiler_params=_cparams("arbitrary"),
    )(proj, h0r, h0i, wb, lam, wc, dsk, wgl_bf, bgl)


def _t5_bucket(rel):
    n = jnp.maximum(rel, 0)
    max_exact = REL_BUCKETS // 2
    nf = jnp.maximum(n, 1).astype(F32)
    large = max_exact + (jnp.log(nf / max_exact) / math.log(REL_MAX_DIST / max_exact)
                         * (REL_BUCKETS - max_exact)).astype(jnp.int32)
    large = jnp.minimum(large, REL_BUCKETS - 1)
    return jnp.where(n < max_exact, n, large)


N_DIST = -(-(REL_MAX_DIST + MOBA_BLOCK - 1) // MOBA_BLOCK) + 1


def _prompt_bias_tiles(rel_bias):
    blk = MOBA_BLOCK
    span = 2 * blk - 1
    rel = jnp.asarray(np.arange(-(blk - 1), N_DIST * blk), jnp.int32)
    line = jnp.where(rel >= 0, rel_bias.astype(F32).T[:, _t5_bucket(rel)], NEG)
    diag = jnp.stack([line[:, d * blk:d * blk + span] for d in range(N_DIST)], axis=1)
    line_rc = jnp.roll(diag[..., ::-1], -(blk - 1), axis=-1)
    rep = jnp.tile(line_rc, (1, 1, blk))[..., :blk * (span - 1)]
    return rep.reshape(diag.shape[0], N_DIST, blk, span - 1)[..., :blk]


def _rank_select(g, n_valid, idx, n_cand, axis):
    cnt = jnp.zeros(g.shape, jnp.int32)
    for jp in range(n_cand):
        gp = lax.slice_in_dim(g, jp, jp + 1, axis=axis)
        beats = jnp.logical_or(gp > g, jnp.logical_and(gp == g, jp < idx))
        cnt = cnt + jnp.where(jnp.logical_and(beats, jp < n_valid), 1, 0)
    return jnp.logical_and(idx < n_valid, cnt < MOBA_TOPK)


SWEEP_UNROLL = 4


def _moba_kernel(q_ref, k_ref, v_ref, bias_ref, o_ref, kft, vb, km, qa, sbuf, red, stat, acc_scr):
    i = pl.program_id(2)
    blk = MOBA_BLOCK
    nb = k_ref.shape[1] // blk
    n_hd = LANES // HEAD_DIM
    scale = HEAD_DIM ** -0.5

    @pl.when(i == 0)
    def _():
        feat = lax.broadcasted_iota(jnp.int32, (LANES, blk), 0)
        for j in range(nb):
            kj = k_ref[0, j * blk:(j + 1) * blk, :]
            kft[j, :LANES, :] = kj.T.astype(BF16)
            kft[j, LANES:, :] = jnp.where(feat == j, 1.0, 0.0).astype(BF16)
            km[j:j + 1, :] = jnp.mean(kj, axis=0, keepdims=True)
        vb[...] = v_ref[0].astype(BF16)

    lane = lax.broadcasted_iota(jnp.int32, (blk, LANES), 1)
    wide = lambda a: jnp.concatenate([a, a], axis=1)
    rep = lambda a: jnp.broadcast_to(a, (blk, LANES))
    for hd in range(n_hd):
        qm = (jnp.where(lane // HEAD_DIM == hd, q_ref[0], 0.0) * scale).astype(BF16)
        g = _dot_nt(km[...].astype(BF16), qm)
        jrow = lax.broadcasted_iota(jnp.int32, g.shape, 0)
        keep = jnp.logical_or(_rank_select(g, i, jrow, nb, axis=0), jrow == i)
        pen = jnp.concatenate([jnp.where(keep, 0.0, NEG), jnp.zeros((LANES - nb, blk), F32)], axis=0)
        qa[hd, :, :LANES] = qm
        qa[hd, :, LANES:] = pen.T.astype(BF16)
        red[hd] = jnp.full((blk, LANES), NEG, F32)

    def sweep(step):
        def group(t, carry):
            for u in range(SWEEP_UNROLL):
                step(SWEEP_UNROLL * t + u)
            return carry

        def single(j, carry):
            step(j)
            return carry

        n_grouped = (i + 1) // SWEEP_UNROLL
        lax.fori_loop(0, n_grouped, group, 0)
        lax.fori_loop(n_grouped * SWEEP_UNROLL, i + 1, single, 0)

    def scores(j):
        dist = jnp.minimum(i - j, N_DIST - 1)
        kt = kft[j]
        for hd in range(n_hd):
            s = _dot(qa[hd], kt) + bias_ref[hd, dist]
            sbuf[hd, j] = s
            red[hd] = jnp.maximum(red[hd], jnp.maximum(s[:, :LANES], s[:, LANES:]))

    sweep(scores)
    for hd in range(n_hd):
        stat[hd] = rep(jnp.max(red[hd], axis=-1, keepdims=True))
        red[hd] = jnp.zeros((blk, LANES), F32)

    def exps(j):
        for hd in range(n_hd):
            e = jnp.exp(sbuf[hd, j] - wide(stat[hd]))
            sbuf[hd, j] = e
            red[hd] += e[:, :LANES] + e[:, LANES:]

    sweep(exps)
    for hd in range(n_hd):
        stat[hd] = rep(1.0 / jnp.sum(red[hd], axis=-1, keepdims=True))
        acc_scr[hd] = jnp.zeros((blk, LANES), F32)

    def weighted(j):
        vj = vb[pl.ds(pl.multiple_of(j * blk, blk), blk), :]
        for hd in range(n_hd):
            p = (sbuf[hd, j] * wide(stat[hd])).astype(BF16)
            acc_scr[hd] += _dot(p, vj)

    sweep(weighted)
    o = acc_scr[0]
    for hd in range(1, n_hd):
        o = jnp.where(lane // HEAD_DIM == hd, acc_scr[hd], o)
    o_ref[0] = o.astype(BF16)


def _moba_prompt(uq, k_all, v_all, layer, bias_tiles):
    b, s, n = uq.shape
    w = n // 2
    hp = w // LANES
    heads_per = LANES // HEAD_DIM
    blk = MOBA_BLOCK
    assert s % blk == 0 and s // blk <= LANES
    return pl.pallas_call(
        _moba_kernel, name="moba_prompt",
        grid=(b, hp, s // blk),
        in_specs=[
            pl.BlockSpec((1, blk, LANES), lambda bi, h, i: (bi, i, hp + h)),
            pl.BlockSpec((1, s, LANES), lambda bi, h, i: (bi, 0, layer * hp + h)),
            pl.BlockSpec((1, s, LANES), lambda bi, h, i: (bi, 0, layer * hp + h)),
            pl.BlockSpec((heads_per, N_DIST, blk, blk), lambda bi, h, i: (h, 0, 0, 0)),
        ],
        out_specs=pl.BlockSpec((1, blk, LANES), lambda bi, h, i: (bi, i, h)),
        out_shape=jax.ShapeDtypeStruct((b, s, w), BF16),
        scratch_shapes=[
            pltpu.VMEM((s // blk, 2 * LANES, blk), BF16), pltpu.VMEM((s, LANES), BF16),
            pltpu.VMEM((s // blk, LANES), F32), pltpu.VMEM((heads_per, blk, 2 * LANES), BF16),
            pltpu.VMEM((heads_per, s // blk, blk, blk), F32), pltpu.VMEM((heads_per, blk, LANES), F32),
            pltpu.VMEM((heads_per, blk, LANES), F32), pltpu.VMEM((heads_per, blk, LANES), F32),
        ],
        compiler_params=_cparams("parallel", "parallel", "arbitrary"),
    )(uq, k_all, v_all, bias_tiles)


def _rnd(a):
    return a.astype(BF16).astype(F32)


DEC_GROUP = 32
PAGES_PER_BLOCK = 2


def _dec_scores_kernel(pt_ref, q_ref, *refs):
    del pt_ref
    k_refs, (bias_ref, s_ref, ks_ref) = refs[:-3], refs[-3:]
    nh = s_ref.shape[2]
    hd = nh * HEAD_DIM
    page = k_refs[0].shape[-1]
    q = (q_ref[0] * HEAD_DIM ** -0.5).astype(BF16)
    ones = jnp.ones((SUBLANES, page), BF16)
    for g in range(s_ref.shape[1]):
        kk = [r[0, 0].reshape(hd, page) for r in k_refs[PAGES_PER_BLOCK * g:PAGES_PER_BLOCK * (g + 1)]]
        s_ref[0, g] = jnp.concatenate([_dot(q, k.astype(BF16)) for k in kk], axis=1) + bias_ref[g]
        hi, lo = _split_bf16(kk[0] + kk[1])
        ks_ref[0, g] = (_dot_nt(ones, hi) + _dot_nt(ones, lo))[0:1]


def _dec_scores(qx, cache_kt, page_table, bias_dec, layer):
    b, nh, hd = qx.shape
    page = cache_kt.shape[4]
    assert MOBA_BLOCK == PAGES_PER_BLOCK * page
    nblk = page_table.shape[1] // PAGES_PER_BLOCK
    grp = math.gcd(DEC_GROUP, nblk)
    n_pg = PAGES_PER_BLOCK * grp
    pg = lambda t: pl.BlockSpec((1, 1, nh, HEAD_DIM, page),
                                lambda bi, j, pt: (pt[bi, n_pg * j + t], layer, 0, 0, 0))
    out = lambda rows, last: pl.BlockSpec((1, grp, rows, last), lambda bi, j, pt: (bi, j, 0, 0))
    return pl.pallas_call(
        _dec_scores_kernel, name="moba_dec_scores",
        grid_spec=pltpu.PrefetchScalarGridSpec(
            num_scalar_prefetch=1,
            grid=(b, nblk // grp),
            in_specs=[pl.BlockSpec((1, nh, hd), lambda bi, j, pt: (bi, 0, 0))]
            + [pg(t) for t in range(n_pg)]
            + [pl.BlockSpec((grp, nh, MOBA_BLOCK), lambda bi, j, pt: (j, 0, 0))],
            out_specs=[out(nh, MOBA_BLOCK), out(1, hd)],
        ),
        out_shape=[jax.ShapeDtypeStruct((b, nblk, nh, MOBA_BLOCK), F32), jax.ShapeDtypeStruct((b, nblk, 1, hd), F32)],
        compiler_params=_cparams("parallel", "arbitrary"),
    )(page_table, qx, *([cache_kt] * n_pg), bias_dec)


def _dec_select_kernel(q_ref, kn_ref, b0_ref, ks_ref, s_ref, p_ref, pown_ref, sel_ref):
    nblk = sel_ref.shape[1]
    blk = MOBA_BLOCK
    q = _rnd(q_ref[...] * HEAD_DIM ** -0.5)
    lane = lax.broadcasted_iota(jnp.int32, sel_ref.shape, 1)
    gate = jnp.zeros(sel_ref.shape, F32)
    for j in range(nblk):
        kmean = _rnd(ks_ref[:, j * HEAD_DIM:(j + 1) * HEAD_DIM] * (1.0 / blk))
        gate = jnp.where(lane == j, jnp.sum(q * kmean, axis=-1, keepdims=True), gate)
    sel = _rank_select(gate, nblk, lane, nblk, axis=1)
    sel_ref[...] = jnp.where(sel, 1.0, 0.0)
    s_own = jnp.sum(q * _rnd(kn_ref[...]), axis=-1, keepdims=True) + b0_ref[:, 0:1]
    m = s_own
    for j in range(nblk):
        mj = jnp.max(s_ref[:, j * blk:(j + 1) * blk], axis=-1, keepdims=True)
        m = jnp.maximum(m, jnp.where(sel[:, j:j + 1], mj, NEG))
    e_own = jnp.exp(s_own - m)
    den = e_own
    for j in range(nblk):
        e = jnp.where(sel[:, j:j + 1], jnp.exp(s_ref[:, j * blk:(j + 1) * blk] - m), 0.0)
        p_ref[:, j * blk:(j + 1) * blk] = e
        den = den + jnp.sum(e, axis=-1, keepdims=True)
    inv = 1.0 / den
    for j in range(nblk):
        p_ref[:, j * blk:(j + 1) * blk] = _rnd(p_ref[:, j * blk:(j + 1) * blk] * inv)
    pown_ref[...] = jnp.broadcast_to(_rnd(e_own * inv), pown_ref.shape)


def _dec_pv_kernel(pt_ref, src_ref, need_ref, p_ref, *refs):
    del pt_ref, src_ref
    v_refs, (pown_ref, vn_ref, o_ref) = refs[:-3], refs[-3:]
    bi = pl.program_id(0)
    j = pl.program_id(1)
    grp = p_ref.shape[1]
    nh = p_ref.shape[2]
    hd = nh * HEAD_DIM
    page = v_refs[0].shape[-1]

    @pl.when(j == 0)
    def _():
        o_ref[0] = pown_ref[0] * _rnd(vn_ref[0])

    for g in range(grp):
        @pl.when(need_ref[bi, j * grp + g] == 1)
        def _(g=g):
            pb = p_ref[0, g].astype(BF16)
            vv = [r[0, 0].reshape(hd, page).astype(BF16) for r in v_refs[PAGES_PER_BLOCK * g:PAGES_PER_BLOCK * (g + 1)]]
            o_all = _dot_nt(pb[:, :page], vv[0]) + _dot_nt(pb[:, page:], vv[1])
            col = lax.broadcasted_iota(jnp.int32, o_all.shape, 1)
            row = lax.broadcasted_iota(jnp.int32, o_all.shape, 0)
            o_ref[0] += jnp.sum(jnp.where(col // HEAD_DIM == row, o_all, 0.0), axis=0, keepdims=True)


def _moba_decode(proj_s, cache_kt, cache_vt, page_table, rel_bias, layer):
    b, n = proj_s.shape
    w = n // 4
    nh = w // HEAD_DIM
    page = cache_kt.shape[4]
    past_len = page_table.shape[1] * page
    assert past_len % MOBA_BLOCK == 0
    nblk = past_len // MOBA_BLOCK
    q = proj_s[:, w:2 * w].reshape(b, nh, HEAD_DIM)
    qx = jnp.einsum('bhd,hg->bhgd', q, jnp.eye(nh, dtype=F32)).reshape(b, nh, w)
    kpos = np.arange(past_len).reshape(nblk, MOBA_BLOCK)
    bias = rel_bias.astype(F32).T[:, _t5_bucket(jnp.asarray(past_len - kpos, jnp.int32))]
    s_p, ks_p = _dec_scores(qx, cache_kt, page_table, jnp.transpose(bias, (1, 0, 2)), layer)
    rows = b * nh
    blk = MOBA_BLOCK
    to_rows = lambda a, last: jnp.transpose(a.reshape(b, nblk, nh, last), (0, 2, 1, 3)).reshape(rows, nblk * last)
    b0 = jnp.broadcast_to(jnp.tile(rel_bias.astype(F32)[_t5_bucket(jnp.zeros((), jnp.int32))], b)[:, None],
                          (rows, LANES))
    f = jax.ShapeDtypeStruct
    p_rows, p_own, sel = pl.pallas_call(
        _dec_select_kernel, name="moba_dec_select",
        out_shape=[f((rows, nblk * blk), F32), f((rows, LANES), F32), f((rows, nblk), F32)],
        compiler_params=pltpu.CompilerParams(vmem_limit_bytes=VMEM_LIMIT),
    )(q.reshape(rows, HEAD_DIM), proj_s[:, 2 * w:3 * w].reshape(rows, HEAD_DIM), b0,
      to_rows(ks_p, HEAD_DIM), to_rows(s_p, blk))

    grp = math.gcd(DEC_GROUP, nblk)
    need = (jnp.max(sel.reshape(b, nh, nblk), axis=1) > 0.5).reshape(b, nblk // grp, grp)
    blk_id = jnp.arange(nblk, dtype=jnp.int32).reshape(1, nblk // grp, grp)
    last_needed = lax.cummax(jnp.where(need, blk_id, -1), axis=1)
    first_needed = jnp.min(jnp.where(need, blk_id, nblk), axis=1, keepdims=True)
    fallback = jnp.where(first_needed < nblk, first_needed, blk_id[:, :1])
    src = jnp.where(last_needed < 0, fallback, last_needed).astype(jnp.int32).reshape(b, nblk)
    p_blk = jnp.transpose(p_rows.reshape(b, nh, nblk, blk), (0, 2, 1, 3))
    p_own_x = jnp.repeat(p_own[:, 0].reshape(b, nh), HEAD_DIM, axis=1).reshape(b, 1, w)
    page_spec = lambda g, off: pl.BlockSpec(
        (1, 1, nh, HEAD_DIM, page),
        lambda bi, j, pt, sr, nd: (pt[bi, PAGES_PER_BLOCK * sr[bi, j * grp + g] + off], layer, 0, 0, 0))
    seq_spec = pl.BlockSpec((1, 1, w), lambda bi, j, pt, sr, nd: (bi, 0, 0))
    att = pl.pallas_call(
        _dec_pv_kernel, name="moba_dec_pv",
        grid_spec=pltpu.PrefetchScalarGridSpec(
            num_scalar_prefetch=3,
            grid=(b, nblk // grp),
            in_specs=[pl.BlockSpec((1, grp, nh, blk), lambda bi, j, pt, sr, nd: (bi, j, 0, 0))]
            + [page_spec(g, off) for g in range(grp) for off in range(PAGES_PER_BLOCK)]
            + [seq_spec, seq_spec],
            out_specs=seq_spec,
        ),
        out_shape=f((b, 1, w), F32),
        compiler_params=_cparams("parallel", "arbitrary"),
    )(page_table, src, need.reshape(b, nblk).astype(jnp.int32), p_blk,
      *([cache_vt] * (grp * PAGES_PER_BLOCK)), p_own_x, proj_s[:, 3 * w:].reshape(b, 1, w))
    return att.reshape(b, w).astype(BF16)


def _tile(m, pref):
    t = min(m, pref)
    while m % t:
        t //= 2
    return t


def _ff_tile(ff, pref):
    best = LANES
    for t in range(LANES, pref + 1, LANES):
        if ff % t == 0:
            best = t
    return best


def kernel(x_prompt, x_sample, cache_k, cache_v, page_table, state_s5_re, state_s5_im, state_pool, rel_bias,
           norm_mix_e, w_in_e, q_norm_e, k_norm_e, s5_a_re, s5_a_im, s5_log_dt, s5_b_re, s5_b_im, s5_c_re,
           s5_c_im, s5_d, s5_w_glu, s5_b_glu, w_out_e, norm_ffn_e, ffn_w_gate, ffn_w_up, ffn_w_down,
           norm_mix_o, pool_w, pool_scale, norm_ffn_o, router_w, router_b, moe_w_gate, moe_w_up, moe_w_down):
    bp, seq, d = x_prompt.shape
    db = x_sample.shape[0]
    assert x_sample.shape[1] == 1
    depth = norm_mix_e.shape[0] + norm_mix_o.shape[0]
    wdt = w_in_e.shape[2] // 4
    nh = wdt // HEAD_DIM
    past_len = page_table.shape[1] * cache_k.shape[2]
    n_exp = router_w.shape[2]
    gn = s5_a_re.shape[1] * s5_a_re.shape[2]

    mp = bp * seq
    xp = x_prompt.reshape(mp, d)
    xs = x_sample.reshape(db, d)
    tm_p = _tile(mp, TOKEN_TILE)
    bias_tiles = _prompt_bias_tiles(rel_bias)
    row = lambda v: v.reshape(1, -1).astype(F32)
    cache_kt = jnp.transpose(cache_k, (0, 1, 3, 4, 2))
    cache_vt = jnp.transpose(cache_v, (0, 1, 3, 4, 2))

    n_even = norm_mix_e.shape[0]
    kp_all = jnp.zeros((mp, n_even * wdt), F32)
    vp_all = jnp.zeros((mp, n_even * wdt), F32)
    ks_all = jnp.zeros((db, n_even * wdt), F32)
    vs_all = jnp.zeros((db, n_even * wdt), F32)
    moe_wg, moe_wu, moe_wd = moe_w_gate.astype(BF16), moe_w_up.astype(BF16), moe_w_down.astype(BF16)
    s5p_l, s5s_re_l, s5s_im_l = [], [], []
    poolp_l, pools_l = [], []
    for layer in range(depth):
        if layer % 2 == 0:
            e = layer // 2
            w_in = w_in_e[e].astype(BF16)
            ones = jnp.ones((wdt,), F32)
            head_gain = jnp.stack([ones, jnp.tile(q_norm_e[e].astype(F32), nh),
                                   jnp.tile(k_norm_e[e].astype(F32), nh), ones]).reshape(4, 1, wdt)
            wb, tabs, lam, wc = _s5_weights(s5_a_re[e], s5_a_im[e], s5_log_dt[e], s5_b_re[e], s5_b_im[e],
                                            s5_c_re[e], s5_c_im[e])
            dsk = row(s5_d[e])
            wgl = s5_w_glu[e].astype(BF16)
            bgl = row(s5_b_glu[e])
            w_out = w_out_e[e].astype(BF16)
            wg, wu, wd = ffn_w_gate[e].astype(BF16), ffn_w_up[e].astype(BF16), ffn_w_down[e].astype(BF16)
            tf = _ff_tile(wg.shape[1], FFN_TILE)

            uq, kp_all, vp_all = _in_proj(xp, row(norm_mix_e[e]), w_in, head_gain, kp_all, vp_all, e, tm_p)
            uq3 = uq.reshape(bp, seq, 2 * wdt)
            s5_out, hfin = _s5_scan(uq3, jnp.zeros((bp, 1, 2 * gn), F32), wb, tabs, wc, dsk, wgl, bgl,
                                    _tile(seq, SCAN_TILE))
            att = _moba_prompt(uq3, kp_all.reshape(bp, seq, -1), vp_all.reshape(bp, seq, -1), e, bias_tiles)
            xp = _mix_ffn(xp, s5_out.reshape(mp, wdt), att.reshape(mp, wdt), w_out, row(norm_ffn_e[e]),
                          wg, wu, wd, tm_p, tf)
            s5p_l.append(hfin.reshape(bp, 2, -1, S5_STATE))

            uq_s, ks_all, vs_all = _in_proj(xs, row(norm_mix_e[e]), w_in, head_gain, ks_all, vs_all, e, db)
            proj_s = jnp.concatenate([uq_s, ks_all[:, e * wdt:(e + 1) * wdt], vs_all[:, e * wdt:(e + 1) * wdt]],
                                     axis=1)
            s5_out_s, hr_s, hi_s = _s5_step(proj_s, state_s5_re[:, e].reshape(db, gn).astype(F32),
                                            state_s5_im[:, e].reshape(db, gn).astype(F32),
                                            wb, lam, wc, dsk, wgl, bgl)
            att_s = _moba_decode(proj_s, cache_kt, cache_vt, page_table, rel_bias, e)
            xs = _mix_ffn(xs, s5_out_s, att_s, w_out, row(norm_ffn_e[e]), wg, wu, wd, db, tf)
            s5s_re_l.append(hr_s.reshape(db, -1, S5_STATE))
            s5s_im_l.append(hi_s.reshape(db, -1, S5_STATE))
        else:
            o = layer // 2
            wp = pool_w[o].astype(BF16)
            sc = row(pool_scale[o])
            wr = jnp.zeros((d, LANES), BF16).at[:, :n_exp].set(router_w[o].astype(BF16))
            br = jnp.zeros((1, LANES), F32).at[:, :n_exp].set(router_b[o].astype(F32))
            tf = _ff_tile(moe_wg.shape[3], EXPERT_TILE)

            xp3, hist_p = _pool_prompt(xp.reshape(bp, seq, d), row(norm_mix_o[o]), wp, sc, _tile(seq, TOKEN_TILE))
            xp = _moe_sparse(xp3.reshape(mp, d), row(norm_ffn_o[o]), wr, br, moe_wg, moe_wu, moe_wd, o, tm_p, tf // 2)
            poolp_l.append(hist_p[:, HALO - POOL_HIST:])

            hist_s = state_pool[:, o].astype(F32)
            xs, hn_s = _pool_step(xs, row(norm_mix_o[o]), jnp.transpose(hist_s, (1, 0, 2)), wp, sc,
                                  min(POOL_HIST, past_len))
            xs = _moe_dense(xs, row(norm_ffn_o[o]), wr, br, moe_wg, moe_wu, moe_wd, o, db, tf)
            pools_l.append(jnp.concatenate([hist_s[:, 1:], hn_s[:, None]], axis=1))

    s5p = jnp.stack(s5p_l, axis=1)

    def per_head(a, lead):
        parts = [a[:, l * wdt:(l + 1) * wdt].reshape(lead + (nh, HEAD_DIM)) for l in range(n_even)]
        return jnp.stack(parts, axis=len(lead))

    return (xp.reshape(bp, seq, d), xs.reshape(db, 1, d),
            per_head(kp_all, (bp, seq)), per_head(vp_all, (bp, seq)),
            per_head(ks_all, (db, 1)), per_head(vs_all, (db, 1)),
            s5p[:, :, 0], s5p[:, :, 1],
            jnp.stack(s5s_re_l, axis=1), jnp.stack(s5s_im_l, axis=1),
            jnp.stack(poolp_l, axis=1), jnp.stack(pools_l, axis=1))
```

```python
import functools
import math

import jax
import jax.numpy as jnp
import numpy as np
from jax import lax
from jax.experimental import pallas as pl
from jax.experimental.pallas import tpu as pltpu

F32 = jnp.float32
BF16 = jnp.bfloat16
EPS = 1e-6
NEG = -1e30

S5_STATE = 64
HEAD_DIM = 64
MOBA_BLOCK = 256
MOBA_TOPK = 3
REL_BUCKETS = 32
REL_MAX_DIST = 1024
POOL_WINDOWS = (2, 4, 8, 16)
POOL_HIST = max(POOL_WINDOWS) - 1
TOP_K_EXPERTS = 2

LANES = 128
SUBLANES = 8
VMEM_LIMIT = 48 * 1024 * 1024
EXPERT_VMEM_LIMIT = 58 * 1024 * 1024

TOKEN_TILE = 512
SCAN_TILE = 1024
FFN_TILE = 1536
EXPERT_TILE = 1792


def _cparams(*sem):
    return pltpu.CompilerParams(dimension_semantics=sem, vmem_limit_bytes=VMEM_LIMIT)


def _rmsnorm(x, g):
    ms = jnp.mean(x * x, axis=-1, keepdims=True)
    return x * lax.rsqrt(ms + EPS) * g


def _split_bf16(x):
    hi = x.astype(BF16)
    lo = (x - hi.astype(F32)).astype(BF16)
    return hi, lo


def _dot(a, b):
    return jnp.dot(a, b, preferred_element_type=F32)


def _dot_nt(a, b):
    return lax.dot_general(a, b, (((1,), (1,)), ((), ())), preferred_element_type=F32)


def _silu(x):
    return x * jax.nn.sigmoid(x)


def _in_proj_kernel(x_ref, g_ref, w_ref, hn_ref, bd_ref, k_in, v_in, uq_ref, k_ref, v_ref, h_scr):
    del k_in, v_in
    j = pl.program_id(1)

    @pl.when(j == 0)
    def _():
        h_scr[...] = _rmsnorm(x_ref[...], g_ref[...]).astype(BF16)

    y = _dot(h_scr[...], w_ref[...])

    def head_normed():
        hi, lo = _split_bf16(y * y)
        ms = _dot(hi, bd_ref[...]) + _dot(lo, bd_ref[...])
        return y * lax.rsqrt(ms + EPS) * hn_ref[0]

    @pl.when(j == 0)
    def _():
        uq_ref[...] = y

    @pl.when(j == 1)
    def _():
        uq_ref[...] = head_normed()

    @pl.when(j == 2)
    def _():
        k_ref[...] = head_normed()

    @pl.when(j == 3)
    def _():
        v_ref[...] = y


def _in_proj(x, g, w_bf, head_gain, k_all, v_all, layer, tm):
    m, d = x.shape
    wdt = w_bf.shape[1] // 4
    bd = np.kron(np.eye(wdt // HEAD_DIM), np.full((HEAD_DIM, HEAD_DIM), 1.0 / HEAD_DIM))
    bd = jnp.asarray(bd, BF16)
    kv_spec = pl.BlockSpec((tm, wdt), lambda i, j: (i, layer))
    return pl.pallas_call(
        _in_proj_kernel, name="in_proj",
        grid=(m // tm, 4),
        in_specs=[
            pl.BlockSpec((tm, d), lambda i, j: (i, 0)),
            pl.BlockSpec((1, d), lambda i, j: (0, 0)),
            pl.BlockSpec((d, wdt), lambda i, j: (0, j)),
            pl.BlockSpec((1, 1, wdt), lambda i, j: (j, 0, 0)),
            pl.BlockSpec((wdt, wdt), lambda i, j: (0, 0)),
            pl.BlockSpec(memory_space=pl.ANY),
            pl.BlockSpec(memory_space=pl.ANY),
        ],
        out_specs=[pl.BlockSpec((tm, wdt), lambda i, j: (i, jnp.minimum(j, 1))), kv_spec, kv_spec],
        out_shape=[jax.ShapeDtypeStruct((m, 2 * wdt), F32), jax.ShapeDtypeStruct(k_all.shape, F32),
                   jax.ShapeDtypeStruct(v_all.shape, F32)],
        input_output_aliases={5: 1, 6: 2},
        scratch_shapes=[pltpu.VMEM((tm, d), BF16)],
        compiler_params=_cparams("parallel", "arbitrary"),
    )(x, g, w_bf, head_gain, bd, k_all, v_all)


def _mix_ffn_kernel(x_ref, a_ref, b_ref, wo_ref, g_ref, wg_ref, wu_ref, wd_ref, o_ref,
                    x1_scr, h_scr, acc_scr):
    f = pl.program_id(1)
    half = a_ref.shape[1]

    @pl.when(f == 0)
    def _():
        x1 = (x_ref[...] + _dot(a_ref[...], wo_ref[:half, :]) + _dot(b_ref[...], wo_ref[half:, :]))
        x1_scr[...] = x1
        h_scr[...] = _rmsnorm(x1, g_ref[...]).astype(BF16)
        acc_scr[...] = jnp.zeros_like(acc_scr)

    h = h_scr[...]
    act = (_silu(_dot(h, wg_ref[...])) * _dot(h, wu_ref[...])).astype(BF16)
    acc_scr[...] += _dot(act, wd_ref[...])

    @pl.when(f == pl.num_programs(1) - 1)
    def _():
        o_ref[...] = x1_scr[...] + acc_scr[...]


def _mix_ffn(x, a, b, wo_bf, g, wg_bf, wu_bf, wd_bf, tm, tf):
    m, d = x.shape
    half = a.shape[1]
    ff = wg_bf.shape[1]
    return pl.pallas_call(
        _mix_ffn_kernel, name="mix_ffn",
        grid=(m // tm, ff // tf),
        in_specs=[
            pl.BlockSpec((tm, d), lambda i, f: (i, 0)),
            pl.BlockSpec((tm, half), lambda i, f: (i, 0)),
            pl.BlockSpec((tm, half), lambda i, f: (i, 0)),
            pl.BlockSpec((2 * half, d), lambda i, f: (0, 0)),
            pl.BlockSpec((1, d), lambda i, f: (0, 0)),
            pl.BlockSpec((d, tf), lambda i, f: (0, f)),
            pl.BlockSpec((d, tf), lambda i, f: (0, f)),
            pl.BlockSpec((tf, d), lambda i, f: (f, 0)),
        ],
        out_specs=pl.BlockSpec((tm, d), lambda i, f: (i, 0)),
        out_shape=jax.ShapeDtypeStruct((m, d), F32),
        scratch_shapes=[pltpu.VMEM((tm, d), F32), pltpu.VMEM((tm, d), BF16), pltpu.VMEM((tm, d), F32)],
        compiler_params=_cparams("parallel", "arbitrary"),
    )(x, a, b, wo_bf, g, wg_bf, wu_bf, wd_bf)


def _route(h, wr, br, n_exp):
    logits = _dot(h.astype(BF16), wr) + br
    lane = lax.broadcasted_iota(jnp.int32, logits.shape, 1)
    logits = jnp.where(lane < n_exp, logits, NEG)
    m1 = jnp.max(logits, axis=-1, keepdims=True)
    i1 = jnp.min(jnp.where(logits == m1, lane, LANES), axis=-1, keepdims=True)
    rest = jnp.where(lane == i1, NEG, logits)
    m2 = jnp.max(rest, axis=-1, keepdims=True)
    i2 = jnp.min(jnp.where(rest == m2, lane, LANES), axis=-1, keepdims=True)
    e2 = jnp.exp(m2 - m1)
    g1 = 1.0 / (1.0 + e2)
    g2 = e2 / (1.0 + e2)
    gate = jnp.where(lane == i1, g1, 0.0) + jnp.where(lane == i2, g2, 0.0)
    return gate, jnp.logical_or(lane == i1, lane == i2)


def _moe_dense_kernel(x_ref, g_ref, wr_ref, br_ref, wg_ref, wu_ref, wd_ref, o_ref,
                      h_scr, gate_scr, acc_scr, *, n_exp):
    e = pl.program_id(1)
    f = pl.program_id(2)

    @pl.when(jnp.logical_and(e == 0, f == 0))
    def _():
        h = _rmsnorm(x_ref[...], g_ref[...])
        h_scr[...] = h.astype(BF16)
        gate_scr[...] = _route(h, wr_ref[...], br_ref[...], n_exp)[0]
        acc_scr[...] = jnp.zeros_like(acc_scr)

    h = h_scr[...]
    lane = lax.broadcasted_iota(jnp.int32, gate_scr.shape, 1)
    w_e = jnp.sum(jnp.where(lane == e, gate_scr[...], 0.0), axis=-1, keepdims=True)
    act = (_silu(_dot(h, wg_ref[0, 0])) * _dot(h, wu_ref[0, 0])).astype(BF16)
    acc_scr[...] += w_e * _dot(act, wd_ref[0, 0])

    @pl.when(jnp.logical_and(e == n_exp - 1, f == pl.num_programs(2) - 1))
    def _():
        o_ref[...] = x_ref[...] + acc_scr[...]


def _moe_dense(x, g, wr_pad, br_pad, wg_bf, wu_bf, wd_bf, layer, tm, tf):
    m, d = x.shape
    _, n_exp, _, ff = wg_bf.shape
    return pl.pallas_call(
        functools.partial(_moe_dense_kernel, n_exp=n_exp), name="moe_dense",
        grid=(m // tm, n_exp, ff // tf),
        in_specs=[
            pl.BlockSpec((tm, d), lambda i, e, f: (i, 0)),
            pl.BlockSpec((1, d), lambda i, e, f: (0, 0)),
            pl.BlockSpec((d, LANES), lambda i, e, f: (0, 0)),
            pl.BlockSpec((1, LANES), lambda i, e, f: (0, 0)),
            pl.BlockSpec((1, 1, d, tf), lambda i, e, f: (layer, e, 0, f)),
            pl.BlockSpec((1, 1, d, tf), lambda i, e, f: (layer, e, 0, f)),
            pl.BlockSpec((1, 1, tf, d), lambda i, e, f: (layer, e, f, 0)),
        ],
        out_specs=pl.BlockSpec((tm, d), lambda i, e, f: (i, 0)),
        out_shape=jax.ShapeDtypeStruct((m, d), F32),
        scratch_shapes=[pltpu.VMEM((tm, d), BF16), pltpu.VMEM((tm, LANES), F32), pltpu.VMEM((tm, d), F32)],
        compiler_params=_cparams("parallel", "arbitrary", "arbitrary"),
    )(x, g, wr_pad, br_pad, wg_bf, wu_bf, wd_bf)


def _moe_route_kernel(x_ref, g_ref, wr_ref, br_ref, h_ref, gate_ref, pos_ref, post_ref, cb_ref, tot_ref, carry,
                      *, n_exp):
    i = pl.program_id(0)
    tm = x_ref.shape[0]

    @pl.when(i == 0)
    def _():
        carry[...] = jnp.zeros_like(carry)

    h = _rmsnorm(x_ref[...], g_ref[...])
    h_ref[...] = h.astype(BF16)
    gate, chosen = _route(h, wr_ref[...], br_ref[...], n_exp)
    gate_ref[...] = gate
    onehot = jnp.where(chosen, 1.0, 0.0)
    r = lax.broadcasted_iota(jnp.int32, (tm, tm), 0)
    c = lax.broadcasted_iota(jnp.int32, (tm, tm), 1)
    before = _dot(jnp.where(c < r, 1.0, 0.0).astype(BF16), onehot.astype(BF16))
    start = carry[...]
    pos = jnp.where(chosen, before + start, -1.0)
    pos_ref[...] = pos
    post_ref[...] = pos.T[:post_ref.shape[0], :]
    cb_ref[0] = start
    total = start + jnp.sum(onehot, axis=0, keepdims=True)
    carry[...] = total
    tot_ref[...] = total


def _moe_route(x, g, wr_bf, br, n_exp, tm):
    m, d = x.shape
    nt = m // tm
    ne8 = -(-n_exp // SUBLANES) * SUBLANES
    f = jax.ShapeDtypeStruct
    return pl.pallas_call(
        functools.partial(_moe_route_kernel, n_exp=n_exp), name="moe_route",
        grid=(nt,),
        in_specs=[
            pl.BlockSpec((tm, d), lambda i: (i, 0)),
            pl.BlockSpec((1, d), lambda i: (0, 0)),
            pl.BlockSpec((d, LANES), lambda i: (0, 0)),
            pl.BlockSpec((1, LANES), lambda i: (0, 0)),
        ],
        out_specs=[
            pl.BlockSpec((tm, d), lambda i: (i, 0)),
            pl.BlockSpec((tm, LANES), lambda i: (i, 0)),
            pl.BlockSpec((tm, LANES), lambda i: (i, 0)),
            pl.BlockSpec((ne8, tm), lambda i: (0, i)),
            pl.BlockSpec((1, 1, LANES), lambda i: (i, 0, 0)),
            pl.BlockSpec((1, LANES), lambda i: (0, 0)),
        ],
        out_shape=[f((m, d), BF16), f((m, LANES), F32), f((m, LANES), F32), f((ne8, m), F32),
                   f((nt, 1, LANES), F32), f((1, LANES), F32)],
        scratch_shapes=[pltpu.VMEM((1, LANES), F32)],
        compiler_params=_cparams("arbitrary"),
    )(x, g, wr_bf, br)


def _moe_gather_kernel(pc_ref, ps_ref, pe_ref, pfirst_ref, pvalid_ref, off_ref, h_ref, post_ref, o_ref, acc):
    k = pl.program_id(0)
    rc, tm = o_ref.shape[0], h_ref.shape[0]

    @pl.when(pfirst_ref[k] == 1)
    def _():
        acc[...] = jnp.zeros_like(acc)

    @pl.when(pvalid_ref[k] == 1)
    def _():
        e = pe_ref[k]
        dest = post_ref[pl.ds(e, 1), :] + off_ref[e].astype(F32)
        rows = (pc_ref[k] * rc + lax.broadcasted_iota(jnp.int32, (rc, tm), 0)).astype(F32)
        acc[...] += _dot(jnp.where(dest == rows, 1.0, 0.0).astype(BF16), h_ref[...])

    o_ref[...] = acc[...].astype(BF16)


def _moe_ffn_kernel(ce_ref, cv_ref, x_ref, wg_ref, wu_ref, wd_ref, y_ref, *, tf):
    c = pl.program_id(0)

    @pl.when(cv_ref[c] == 1)
    def _():
        x = x_ref[...]
        acc = jnp.zeros(y_ref.shape, F32)
        for f0 in range(0, wg_ref.shape[3], tf):
            act = (_silu(_dot(x, wg_ref[0, 0, :, f0:f0 + tf])) * _dot(x, wu_ref[0, 0, :, f0:f0 + tf])).astype(BF16)
            acc = acc + _dot(act, wd_ref[0, 0, f0:f0 + tf, :])
        y_ref[...] = acc.astype(BF16)

    @pl.when(cv_ref[c] == 0)
    def _():
        y_ref[...] = jnp.zeros_like(y_ref)


def _moe_combine_kernel(qs_ref, qc_ref, qe_ref, qfirst_ref, qvalid_ref, off_ref, x_ref, y_ref, pos_ref, gate_ref,
                        o_ref):
    k = pl.program_id(0)
    tm, rc = x_ref.shape[0], y_ref.shape[0]

    @pl.when(qfirst_ref[k] == 1)
    def _():
        o_ref[...] = x_ref[...]

    @pl.when(qvalid_ref[k] == 1)
    def _():
        e = qe_ref[k]
        lane = lax.broadcasted_iota(jnp.int32, pos_ref.shape, 1)
        pos_e = jnp.sum(jnp.where(lane == e, pos_ref[...], 0.0), axis=-1, keepdims=True)
        gate_e = jnp.sum(jnp.where(lane == e, gate_ref[...], 0.0), axis=-1, keepdims=True)
        dest = pos_e + off_ref[e].astype(F32)
        cols = (qc_ref[k] * rc + lax.broadcasted_iota(jnp.int32, (tm, rc), 1)).astype(F32)
        pick = jnp.where(dest == cols, 1.0, 0.0).astype(BF16)
        o_ref[...] += gate_e * _dot(pick, y_ref[...])


def _pair_list(inter, n_pairs_max):
    nb = inter.shape[1]
    flat = inter.reshape(-1)
    n = jnp.sum(flat.astype(jnp.int32))
    idx = jnp.nonzero(flat, size=n_pairs_max, fill_value=0)[0].astype(jnp.int32)
    k = jnp.arange(n_pairs_max, dtype=jnp.int32)
    idx = jnp.where(k < n, idx, idx[jnp.maximum(n - 1, 0)])
    a, b = idx // nb, idx % nb
    valid = k < n
    first = jnp.logical_and(valid, jnp.logical_or(k == 0, a != jnp.roll(a, 1)))
    return a, b, valid.astype(jnp.int32), first.astype(jnp.int32)


ROW_CHUNK = 256


def _moe_sparse(x, g, wr_bf, br, wg_bf, wu_bf, wd_bf, layer, tm, tf):
    m, d = x.shape
    _, n_exp, _, ff = wg_bf.shape
    nt = m // tm
    rc = ROW_CHUNK
    h, gate, pos, pos_t, cb, tot = _moe_route(x, g, wr_bf, br, n_exp, tm)

    cnt = tot[0, :n_exp].astype(jnp.int32)
    cbi = jnp.concatenate([cb[:, 0, :n_exp], tot[:, :n_exp]], axis=0).astype(jnp.int32)
    gsz = (cnt + rc - 1) // rc * rc
    ends = jnp.cumsum(gsz)
    off = (ends - gsz).astype(jnp.int32)
    n_chunks = TOP_K_EXPERTS * m // rc + n_exp
    cstart = jnp.arange(n_chunks, dtype=jnp.int32) * rc
    ce = jnp.minimum(jnp.searchsorted(ends, cstart, side='right'), n_exp - 1).astype(jnp.int32)
    cv = cstart < ends[-1]
    per_tile = cbi.T[ce]
    start = off[ce][:, None] + per_tile[:, :-1]
    end = off[ce][:, None] + per_tile[:, 1:]
    inter = (cv[:, None] & (end > start) & (start < cstart[:, None] + rc) & (end > cstart[:, None]))
    n_pairs = n_chunks + n_exp * nt
    unused = jnp.logical_and(jnp.logical_not(cv)[:, None], jnp.arange(nt)[None, :] == 0)
    pc, ps, plisted, pfirst = _pair_list(jnp.logical_or(inter, unused), n_pairs + n_exp)
    pvalid = plisted * inter[pc, ps].astype(jnp.int32)
    qs, qc, qvalid, qfirst = _pair_list(inter.T, n_pairs)

    xs = pl.pallas_call(
        _moe_gather_kernel, name="moe_gather",
        grid_spec=pltpu.PrefetchScalarGridSpec(
            num_scalar_prefetch=6,
            grid=(n_pairs + n_exp,),
            in_specs=[
                pl.BlockSpec((tm, d), lambda k, pc, ps, *_: (ps[k], 0)),
                pl.BlockSpec((pos_t.shape[0], tm), lambda k, pc, ps, *_: (0, ps[k])),
            ],
            out_specs=pl.BlockSpec((rc, d), lambda k, pc, *_: (pc[k], 0)),
            scratch_shapes=[pltpu.VMEM((rc, d), F32)],
        ),
        out_shape=jax.ShapeDtypeStruct((n_chunks * rc, d), BF16),
        compiler_params=_cparams("arbitrary"),
    )(pc, ps, ce[pc], pfirst, pvalid, off, h, pos_t)

    cvi = cv.astype(jnp.int32)
    wspec = lambda shape: pl.BlockSpec((1,) + shape, lambda c, ce, cv: (layer, ce[c], 0, 0))
    ys = pl.pallas_call(
        functools.partial(_moe_ffn_kernel, tf=tf), name="moe_ffn",
        grid_spec=pltpu.PrefetchScalarGridSpec(
            num_scalar_prefetch=2,
            grid=(n_chunks,),
            in_specs=[
                pl.BlockSpec((rc, d), lambda c, ce, cv: (c * cv[c], 0)),
                wspec((1, d, ff)), wspec((1, d, ff)), wspec((1, ff, d)),
            ],
            out_specs=pl.BlockSpec((rc, d), lambda c, ce, cv: (c, 0)),
        ),
        out_shape=jax.ShapeDtypeStruct((n_chunks * rc, d), BF16),
        compiler_params=pltpu.CompilerParams(dimension_semantics=("arbitrary",),
                                             vmem_limit_bytes=EXPERT_VMEM_LIMIT),
    )(ce, cvi, xs, wg_bf, wu_bf, wd_bf)

    return pl.pallas_call(
        _moe_combine_kernel, name="moe_combine",
        grid_spec=pltpu.PrefetchScalarGridSpec(
            num_scalar_prefetch=6,
            grid=(n_pairs,),
            in_specs=[
                pl.BlockSpec((tm, d), lambda k, qs, qc, *_: (qs[k], 0)),
                pl.BlockSpec((rc, d), lambda k, qs, qc, *_: (qc[k], 0)),
                pl.BlockSpec((tm, LANES), lambda k, qs, *_: (qs[k], 0)),
                pl.BlockSpec((tm, LANES), lambda k, qs, *_: (qs[k], 0)),
            ],
            out_specs=pl.BlockSpec((tm, d), lambda k, qs, *_: (qs[k], 0)),
        ),
        out_shape=jax.ShapeDtypeStruct((m, d), F32),
        compiler_params=_cparams("arbitrary"),
    )(qs, qc, ce[qc], qfirst, qvalid, off, x, ys, pos, gate)


HALO = 16


def _pool_kernel(x_ref, g_ref, wp_ref, sc_ref, o_ref, hist_ref, hbuf):
    t = pl.program_id(1)
    tm = x_ref.shape[1]
    gw = wp_ref.shape[1]

    @pl.when(t == 0)
    def _():
        hbuf[0:HALO, :] = jnp.zeros((HALO, hbuf.shape[1]), F32)

    @pl.when(t > 0)
    def _():
        hbuf[0:HALO, :] = hbuf[tm:tm + HALO, :]

    x = x_ref[0]
    hn = _rmsnorm(x, g_ref[...])
    hbuf[HALO:HALO + tm, :] = hn
    pos = t * tm + lax.broadcasted_iota(jnp.int32, (tm, 1), 0)
    ys = []
    for gi, w in enumerate(POOL_WINDOWS):
        sl = slice(gi * gw, (gi + 1) * gw)
        tot = hn[:, sl]
        for i in range(1, w):
            tot = tot + hbuf[HALO - i:HALO - i + tm, sl]
        cnt = jnp.minimum(pos + 1, w).astype(F32)
        pooled = tot / cnt - hn[:, sl]
        ys.append(_dot(pooled.astype(BF16), wp_ref[gi]))
    o_ref[0] = x + jnp.concatenate(ys, axis=1) * sc_ref[...]

    @pl.when(t == pl.num_programs(1) - 1)
    def _():
        hist_ref[0] = hbuf[tm:tm + HALO, :]


def _pool_prompt(x, g, wp_bf, sc, tm):
    b, s, d = x.shape
    ng, gw, _ = wp_bf.shape
    return pl.pallas_call(
        _pool_kernel, name="pool_prompt",
        grid=(b, s // tm),
        in_specs=[
            pl.BlockSpec((1, tm, d), lambda i, t: (i, t, 0)),
            pl.BlockSpec((1, d), lambda i, t: (0, 0)),
            pl.BlockSpec((ng, gw, gw), lambda i, t: (0, 0, 0)),
            pl.BlockSpec((1, d), lambda i, t: (0, 0)),
        ],
        out_specs=[
            pl.BlockSpec((1, tm, d), lambda i, t: (i, t, 0)),
            pl.BlockSpec((1, HALO, d), lambda i, t: (i, 0, 0)),
        ],
        out_shape=[jax.ShapeDtypeStruct((b, s, d), F32), jax.ShapeDtypeStruct((b, HALO, d), F32)],
        scratch_shapes=[pltpu.VMEM((HALO + tm, d), F32)],
        compiler_params=_cparams("parallel", "arbitrary"),
    )(x, g, wp_bf, sc)


def _pool_step_kernel(x_ref, g_ref, hist_ref, wp_ref, sc_ref, o_ref, hn_ref, *, n_valid):
    gw = wp_ref.shape[1]
    x = x_ref[...]
    hn = _rmsnorm(x, g_ref[...])
    hn_ref[...] = hn
    ys = []
    for gi, w in enumerate(POOL_WINDOWS):
        sl = slice(gi * gw, (gi + 1) * gw)
        tot = hn[:, sl]
        for i in range(1, min(w, n_valid + 1)):
            tot = tot + hist_ref[POOL_HIST - i][:, sl]
        pooled = tot / float(min(w, n_valid + 1)) - hn[:, sl]
        ys.append(_dot(pooled.astype(BF16), wp_ref[gi]))
    o_ref[...] = x + jnp.concatenate(ys, axis=1) * sc_ref[...]


def _pool_step(x, g, hist_t, wp_bf, sc, n_valid):
    b, d = x.shape
    return pl.pallas_call(
        functools.partial(_pool_step_kernel, n_valid=n_valid), name="pool_step",
        out_shape=[jax.ShapeDtypeStruct((b, d), F32), jax.ShapeDtypeStruct((b, d), F32)],
        compiler_params=pltpu.CompilerParams(vmem_limit_bytes=VMEM_LIMIT),
    )(x, g, hist_t, wp_bf, sc)


N_POW = SUBLANES


def _s5_prep_kernel(ar_ref, ai_ref, ldt_ref, br_ref, bi_ref, pwr_ref, pwi_ref, bbr_ref, bbi_ref):
    ar = ar_ref[...]
    ai = ai_ref[...]
    dt = jnp.exp(ldt_ref[...])
    mag = jnp.exp(ar * dt)
    lr = mag * jnp.cos(ai * dt)
    li = mag * jnp.sin(ai * dt)
    den = ar * ar + ai * ai
    rr = ((lr - 1.0) * ar + li * ai) / den
    ri = (li * ar - (lr - 1.0) * ai) / den
    br = br_ref[...]
    bi = bi_ref[...]
    bbr_ref[...] = rr * br - ri * bi
    bbi_ref[...] = rr * bi + ri * br
    cr, ci = lr, li
    for k in range(N_POW):
        pwr_ref[k] = cr
        pwi_ref[k] = ci
        cr, ci = cr * lr - ci * li, cr * li + ci * lr


def _s5_prep(a_re, a_im, log_dt, b_re, b_im):
    g, n = a_re.shape
    p = b_re.shape[2]
    f = jax.ShapeDtypeStruct
    return pl.pallas_call(
        _s5_prep_kernel, name="s5_prep",
        out_shape=[f((N_POW, g, 1, n), F32), f((N_POW, g, 1, n), F32), f((g, p, n), F32), f((g, p, n), F32)],
    )(a_re[:, None, :], a_im[:, None, :], log_dt[:, None, None],
      jnp.transpose(b_re, (0, 2, 1)), jnp.transpose(b_im, (0, 2, 1)))


def _s5_weights(a_re, a_im, log_dt, b_re, b_im, c_re, c_im):
    g, n = a_re.shape
    p = b_re.shape[2]
    gn = g * n
    pwr, pwi, bbr, bbi = _s5_prep(a_re, a_im, log_dt, b_re, b_im)
    pwr = pwr.reshape(N_POW, gn)
    pwi = pwi.reshape(N_POW, gn)
    row = np.arange(SUBLANES)[:, None]
    tabs = []
    for k in (1, 2, 4):
        tabs += [jnp.where(row >= k, pwr[k - 1][None], 0.0), jnp.where(row >= k, pwi[k - 1][None], 0.0)]
    tabs += [pwr, pwi]
    tabs = jnp.stack(tabs)
    lam = jnp.stack([pwr[0], pwi[0]])
    n_tiles = gn // LANES
    g_tile = LANES // n
    g_chunk = LANES // p
    place = np.zeros((n_tiles, g_chunk, g_tile), np.float32)
    for j in range(n_tiles):
        for i in range(g_tile):
            place[j, (j * g_tile + i) % g_chunk, i] = 1.0
    bb = jnp.stack([bbr, bbi]).reshape(2, n_tiles, g_tile, p, n)
    wb = jnp.einsum('cjipn,jli->jlpcin', bb, place).reshape(n_tiles, LANES, 2 * LANES).astype(BF16)
    n_out = g * p // LANES
    cc = jnp.stack([c_re, -c_im]).reshape(2, n_out, g_chunk, p, n)
    wc = jnp.einsum('cmlpn,lk->mclnkp', cc, np.eye(g_chunk, dtype=np.float32))
    wc = wc.reshape(n_out, 2 * g_chunk * n, LANES).astype(BF16)
    return wb, tabs, lam, wc


def _s5_drive(u, wb_ref, bur, bui, row0):
    t = u.shape[0]
    ub = u.astype(BF16)
    n_tiles = wb_ref.shape[0]
    u_per_tile = u.shape[1] // n_tiles
    for j in range(n_tiles):
        c0 = (j * u_per_tile) // LANES * LANES
        r = _dot(ub[:, c0:c0 + LANES], wb_ref[j])
        bur[row0:row0 + t, j * LANES:(j + 1) * LANES] = r[:, :LANES]
        bui[row0:row0 + t, j * LANES:(j + 1) * LANES] = r[:, LANES:]


def _s5_readout(u, hr, hi, wc_ref, d_ref, wgl_ref, bgl_ref):
    hrb = hr.astype(BF16)
    hib = hi.astype(BF16)
    kk = wc_ref.shape[1] // 2
    ys = []
    for m in range(wc_ref.shape[0]):
        ys.append(_dot(hrb[:, m * kk:(m + 1) * kk], wc_ref[m, :kk, :])
                  + _dot(hib[:, m * kk:(m + 1) * kk], wc_ref[m, kk:, :]))
    y = jnp.concatenate(ys, axis=1) + d_ref[...] * u
    z = jax.nn.gelu(y, approximate=True)
    return z * jax.nn.sigmoid(_dot(z.astype(BF16), wgl_ref[...]) + bgl_ref[...])


def _s5_scan_kernel(u_ref, h0_ref, wb_ref, tab_ref, wc_ref, d_ref, wgl_ref, bgl_ref, o_ref, hf_ref, bur, bui):
    t = pl.program_id(1)
    tt = u_ref.shape[1]
    gn = bur.shape[1]
    c0 = SUBLANES

    @pl.when(t == 0)
    def _():
        bur[0:c0, :] = jnp.broadcast_to(h0_ref[0, :, :gn], (c0, gn))
        bui[0:c0, :] = jnp.broadcast_to(h0_ref[0, :, gn:], (c0, gn))

    @pl.when(t > 0)
    def _():
        bur[0:c0, :] = bur[tt:tt + c0, :]
        bui[0:c0, :] = bui[tt:tt + c0, :]

    u = u_ref[0]
    _s5_drive(u, wb_ref, bur, bui, c0)

    def block(b, carry):
        base = pl.multiple_of(b * SUBLANES, SUBLANES)
        cr = bur[pl.ds(base + c0 - 1, 1), :]
        ci = bui[pl.ds(base + c0 - 1, 1), :]
        xr = bur[pl.ds(base + c0, SUBLANES), :]
        xi = bui[pl.ds(base + c0, SUBLANES), :]
        for s, k in enumerate((1, 2, 4)):
            ar = tab_ref[2 * s]
            ai = tab_ref[2 * s + 1]
            sr = pltpu.roll(xr, k, axis=0)
            si = pltpu.roll(xi, k, axis=0)
            xr, xi = xr + ar * sr - ai * si, xi + ar * si + ai * sr
        pr = tab_ref[6]
        pi = tab_ref[7]
        bur[pl.ds(base + c0, SUBLANES), :] = xr + pr * cr - pi * ci
        bui[pl.ds(base + c0, SUBLANES), :] = xi + pr * ci + pi * cr
        return carry

    lax.fori_loop(0, tt // SUBLANES, block, 0)
    o_ref[0] = _s5_readout(u, bur[c0:c0 + tt, :], bui[c0:c0 + tt, :], wc_ref, d_ref, wgl_ref, bgl_ref).astype(BF16)

    @pl.when(t == pl.num_programs(1) - 1)
    def _():
        hf_ref[0] = jnp.concatenate([bur[tt + c0 - 1:tt + c0, :], bui[tt + c0 - 1:tt + c0, :]], axis=1)


def _s5_scan(proj, h0, wb, tabs, wc, dsk, wgl_bf, bgl, tt):
    b, s, _ = proj.shape
    w = dsk.shape[1]
    gn = tabs.shape[2]
    full = lambda a: pl.BlockSpec(a.shape, lambda i, t: (0,) * a.ndim)
    return pl.pallas_call(
        _s5_scan_kernel, name="s5_scan",
        grid=(b, s // tt),
        in_specs=[
            pl.BlockSpec((1, tt, w), lambda i, t: (i, t, 0)),
            pl.BlockSpec((1, 1, 2 * gn), lambda i, t: (i, 0, 0)),
            full(wb), full(tabs), full(wc), full(dsk), full(wgl_bf), full(bgl),
        ],
        out_specs=[
            pl.BlockSpec((1, tt, w), lambda i, t: (i, t, 0)),
            pl.BlockSpec((1, 1, 2 * gn), lambda i, t: (i, 0, 0)),
        ],
        out_shape=[jax.ShapeDtypeStruct((b, s, w), BF16), jax.ShapeDtypeStruct((b, 1, 2 * gn), F32)],
        scratch_shapes=[pltpu.VMEM((tt + SUBLANES, gn), F32), pltpu.VMEM((tt + SUBLANES, gn), F32)],
        compiler_params=_cparams("parallel", "arbitrary"),
    )(proj, h0, wb, tabs, wc, dsk, wgl_bf, bgl)


def _s5_step_kernel(u_ref, h0r_ref, h0i_ref, wb_ref, lam_ref, wc_ref, d_ref, wgl_ref, bgl_ref,
                    o_ref, hr_ref, hi_ref, bur, bui):
    u = u_ref[...]
    _s5_drive(u, wb_ref, bur, bui, 0)
    lr = lam_ref[0:1, :]
    li = lam_ref[1:2, :]
    h0r = h0r_ref[...]
    h0i = h0i_ref[...]
    hr = bur[...] + lr * h0r - li * h0i
    hi = bui[...] + lr * h0i + li * h0r
    hr_ref[...] = hr
    hi_ref[...] = hi
    o_ref[...] = _s5_readout(u, hr, hi, wc_ref, d_ref, wgl_ref, bgl_ref).astype(BF16)


def _s5_step(proj, h0r, h0i, wb, lam, wc, dsk, wgl_bf, bgl):
    b = proj.shape[0]
    w = dsk.shape[1]
    gn = lam.shape[1]
    full = lambda a: pl.BlockSpec(a.shape, lambda i: (0,) * a.ndim)
    return pl.pallas_call(
        _s5_step_kernel, name="s5_step",
        grid=(1,),
        in_specs=[pl.BlockSpec((b, w), lambda i: (0, 0)), full(h0r), full(h0i), full(wb), full(lam), full(wc),
                  full(dsk), full(wgl_bf), full(bgl)],
        out_specs=[pl.BlockSpec((b, w), lambda i: (0, 0)), pl.BlockSpec((b, gn), lambda i: (0, 0)),
                   pl.BlockSpec((b, gn), lambda i: (0, 0))],
        out_shape=[jax.ShapeDtypeStruct((b, w), BF16), jax.ShapeDtypeStruct((b, gn), F32),
                   jax.ShapeDtypeStruct((b, gn), F32)],
        scratch_shapes=[pltpu.VMEM((b, gn), F32), pltpu.VMEM((b, gn), F32)],
        compiler_params=_cparams("arbitrary"),
    )(proj, h0r, h0i, wb, lam, wc, dsk, wgl_bf, bgl)


def _t5_bucket(rel):
    n = jnp.maximum(rel, 0)
    max_exact = REL_BUCKETS // 2
    nf = jnp.maximum(n, 1).astype(F32)
    large = max_exact + (jnp.log(nf / max_exact) / math.log(REL_MAX_DIST / max_exact)
                         * (REL_BUCKETS - max_exact)).astype(jnp.int32)
    large = jnp.minimum(large, REL_BUCKETS - 1)
    return jnp.where(n < max_exact, n, large)


N_DIST = -(-(REL_MAX_DIST + MOBA_BLOCK - 1) // MOBA_BLOCK) + 1


def _prompt_bias_tiles(rel_bias):
    blk = MOBA_BLOCK
    span = 2 * blk - 1
    rel = jnp.asarray(np.arange(-(blk - 1), N_DIST * blk), jnp.int32)
    line = jnp.where(rel >= 0, rel_bias.astype(F32).T[:, _t5_bucket(rel)], NEG)
    diag = jnp.stack([line[:, d * blk:d * blk + span] for d in range(N_DIST)], axis=1)
    line_rc = jnp.roll(diag[..., ::-1], -(blk - 1), axis=-1)
    rep = jnp.tile(line_rc, (1, 1, blk))[..., :blk * (span - 1)]
    return rep.reshape(diag.shape[0], N_DIST, blk, span - 1)[..., :blk]


def _rank_select(g, n_valid, idx, n_cand, axis):
    cnt = jnp.zeros(g.shape, jnp.int32)
    for jp in range(n_cand):
        gp = lax.slice_in_dim(g, jp, jp + 1, axis=axis)
        beats = jnp.logical_or(gp > g, jnp.logical_and(gp == g, jp < idx))
        cnt = cnt + jnp.where(jnp.logical_and(beats, jp < n_valid), 1, 0)
    return jnp.logical_and(idx < n_valid, cnt < MOBA_TOPK)


SWEEP_UNROLL = 4


def _moba_kernel(q_ref, k_ref, v_ref, bias_ref, o_ref, kft, vb, km, qa, sbuf, red, stat, acc_scr):
    i = pl.program_id(2)
    blk = MOBA_BLOCK
    nb = k_ref.shape[1] // blk
    n_hd = LANES // HEAD_DIM
    scale = HEAD_DIM ** -0.5

    @pl.when(i == 0)
    def _():
        feat = lax.broadcasted_iota(jnp.int32, (LANES, blk), 0)
        for j in range(nb):
            kj = k_ref[0, j * blk:(j + 1) * blk, :]
            kft[j, :LANES, :] = kj.T.astype(BF16)
            kft[j, LANES:, :] = jnp.where(feat == j, 1.0, 0.0).astype(BF16)
            km[j:j + 1, :] = jnp.mean(kj, axis=0, keepdims=True)
        vb[...] = v_ref[0].astype(BF16)

    lane = lax.broadcasted_iota(jnp.int32, (blk, LANES), 1)
    wide = lambda a: jnp.concatenate([a, a], axis=1)
    rep = lambda a: jnp.broadcast_to(a, (blk, LANES))
    for hd in range(n_hd):
        qm = (jnp.where(lane // HEAD_DIM == hd, q_ref[0], 0.0) * scale).astype(BF16)
        g = _dot_nt(km[...].astype(BF16), qm)
        jrow = lax.broadcasted_iota(jnp.int32, g.shape, 0)
        keep = jnp.logical_or(_rank_select(g, i, jrow, nb, axis=0), jrow == i)
        pen = jnp.concatenate([jnp.where(keep, 0.0, NEG), jnp.zeros((LANES - nb, blk), F32)], axis=0)
        qa[hd, :, :LANES] = qm
        qa[hd, :, LANES:] = pen.T.astype(BF16)
        red[hd] = jnp.full((blk, LANES), NEG, F32)

    def sweep(step):
        def group(t, carry):
            for u in range(SWEEP_UNROLL):
                step(SWEEP_UNROLL * t + u)
            return carry

        def single(j, carry):
            step(j)
            return carry

        n_grouped = (i + 1) // SWEEP_UNROLL
        lax.fori_loop(0, n_grouped, group, 0)
        lax.fori_loop(n_grouped * SWEEP_UNROLL, i + 1, single, 0)

    def scores(j):
        dist = jnp.minimum(i - j, N_DIST - 1)
        kt = kft[j]
        for hd in range(n_hd):
            s = _dot(qa[hd], kt) + bias_ref[hd, dist]
            sbuf[hd, j] = s
            red[hd] = jnp.maximum(red[hd], jnp.maximum(s[:, :LANES], s[:, LANES:]))

    sweep(scores)
    for hd in range(n_hd):
        stat[hd] = rep(jnp.max(red[hd], axis=-1, keepdims=True))
        red[hd] = jnp.zeros((blk, LANES), F32)

    def exps(j):
        for hd in range(n_hd):
            e = jnp.exp(sbuf[hd, j] - wide(stat[hd]))
            sbuf[hd, j] = e
            red[hd] += e[:, :LANES] + e[:, LANES:]

    sweep(exps)
    for hd in range(n_hd):
        stat[hd] = rep(1.0 / jnp.sum(red[hd], axis=-1, keepdims=True))
        acc_scr[hd] = jnp.zeros((blk, LANES), F32)

    def weighted(j):
        vj = vb[pl.ds(pl.multiple_of(j * blk, blk), blk), :]
        for hd in range(n_hd):
            p = (sbuf[hd, j] * wide(stat[hd])).astype(BF16)
            acc_scr[hd] += _dot(p, vj)

    sweep(weighted)
    o = acc_scr[0]
    for hd in range(1, n_hd):
        o = jnp.where(lane // HEAD_DIM == hd, acc_scr[hd], o)
    o_ref[0] = o.astype(BF16)


def _moba_prompt(uq, k_all, v_all, layer, bias_tiles):
    b, s, n = uq.shape
    w = n // 2
    hp = w // LANES
    heads_per = LANES // HEAD_DIM
    blk = MOBA_BLOCK
    assert s % blk == 0 and s // blk <= LANES
    return pl.pallas_call(
        _moba_kernel, name="moba_prompt",
        grid=(b, hp, s // blk),
        in_specs=[
            pl.BlockSpec((1, blk, LANES), lambda bi, h, i: (bi, i, hp + h)),
            pl.BlockSpec((1, s, LANES), lambda bi, h, i: (bi, 0, layer * hp + h)),
            pl.BlockSpec((1, s, LANES), lambda bi, h, i: (bi, 0, layer * hp + h)),
            pl.BlockSpec((heads_per, N_DIST, blk, blk), lambda bi, h, i: (h, 0, 0, 0)),
        ],
        out_specs=pl.BlockSpec((1, blk, LANES), lambda bi, h, i: (bi, i, h)),
        out_shape=jax.ShapeDtypeStruct((b, s, w), BF16),
        scratch_shapes=[
            pltpu.VMEM((s // blk, 2 * LANES, blk), BF16), pltpu.VMEM((s, LANES), BF16),
            pltpu.VMEM((s // blk, LANES), F32), pltpu.VMEM((heads_per, blk, 2 * LANES), BF16),
            pltpu.VMEM((heads_per, s // blk, blk, blk), F32), pltpu.VMEM((heads_per, blk, LANES), F32),
            pltpu.VMEM((heads_per, blk, LANES), F32), pltpu.VMEM((heads_per, blk, LANES), F32),
        ],
        compiler_params=_cparams("parallel", "parallel", "arbitrary"),
    )(uq, k_all, v_all, bias_tiles)


def _rnd(a):
    return a.astype(BF16).astype(F32)


DEC_GROUP = 32
PAGES_PER_BLOCK = 2


def _dec_scores_kernel(pt_ref, q_ref, *refs):
    del pt_ref
    k_refs, (bias_ref, s_ref, ks_ref) = refs[:-3], refs[-3:]
    nh = s_ref.shape[2]
    hd = nh * HEAD_DIM
    page = k_refs[0].shape[-1]
    q = (q_ref[0] * HEAD_DIM ** -0.5).astype(BF16)
    ones = jnp.ones((SUBLANES, page), BF16)
    for g in range(s_ref.shape[1]):
        kk = [r[0, 0].reshape(hd, page) for r in k_refs[PAGES_PER_BLOCK * g:PAGES_PER_BLOCK * (g + 1)]]
        s_ref[0, g] = jnp.concatenate([_dot(q, k.astype(BF16)) for k in kk], axis=1) + bias_ref[g]
        hi, lo = _split_bf16(kk[0] + kk[1])
        ks_ref[0, g] = (_dot_nt(ones, hi) + _dot_nt(ones, lo))[0:1]


def _dec_scores(qx, cache_kt, page_table, bias_dec, layer):
    b, nh, hd = qx.shape
    page = cache_kt.shape[4]
    assert MOBA_BLOCK == PAGES_PER_BLOCK * page
    nblk = page_table.shape[1] // PAGES_PER_BLOCK
    grp = math.gcd(DEC_GROUP, nblk)
    n_pg = PAGES_PER_BLOCK * grp
    pg = lambda t: pl.BlockSpec((1, 1, nh, HEAD_DIM, page),
                                lambda bi, j, pt: (pt[bi, n_pg * j + t], layer, 0, 0, 0))
    out = lambda rows, last: pl.BlockSpec((1, grp, rows, last), lambda bi, j, pt: (bi, j, 0, 0))
    return pl.pallas_call(
        _dec_scores_kernel, name="moba_dec_scores",
        grid_spec=pltpu.PrefetchScalarGridSpec(
            num_scalar_prefetch=1,
            grid=(b, nblk // grp),
            in_specs=[pl.BlockSpec((1, nh, hd), lambda bi, j, pt: (bi, 0, 0))]
            + [pg(t) for t in range(n_pg)]
            + [pl.BlockSpec((grp, nh, MOBA_BLOCK), lambda bi, j, pt: (j, 0, 0))],
            out_specs=[out(nh, MOBA_BLOCK), out(1, hd)],
        ),
        out_shape=[jax.ShapeDtypeStruct((b, nblk, nh, MOBA_BLOCK), F32), jax.ShapeDtypeStruct((b, nblk, 1, hd), F32)],
        compiler_params=_cparams("parallel", "arbitrary"),
    )(page_table, qx, *([cache_kt] * n_pg), bias_dec)


def _dec_select_kernel(q_ref, kn_ref, b0_ref, ks_ref, s_ref, p_ref, pown_ref, sel_ref):
    nblk = sel_ref.shape[1]
    blk = MOBA_BLOCK
    q = _rnd(q_ref[...] * HEAD_DIM ** -0.5)
    lane = lax.broadcasted_iota(jnp.int32, sel_ref.shape, 1)
    gate = jnp.zeros(sel_ref.shape, F32)
    for j in range(nblk):
        kmean = _rnd(ks_ref[:, j * HEAD_DIM:(j + 1) * HEAD_DIM] * (1.0 / blk))
        gate = jnp.where(lane == j, jnp.sum(q * kmean, axis=-1, keepdims=True), gate)
    sel = _rank_select(gate, nblk, lane, nblk, axis=1)
    sel_ref[...] = jnp.where(sel, 1.0, 0.0)
    s_own = jnp.sum(q * _rnd(kn_ref[...]), axis=-1, keepdims=True) + b0_ref[:, 0:1]
    m = s_own
    for j in range(nblk):
        mj = jnp.max(s_ref[:, j * blk:(j + 1) * blk], axis=-1, keepdims=True)
        m = jnp.maximum(m, jnp.where(sel[:, j:j + 1], mj, NEG))
    e_own = jnp.exp(s_own - m)
    den = e_own
    for j in range(nblk):
        e = jnp.where(sel[:, j:j + 1], jnp.exp(s_ref[:, j * blk:(j + 1) * blk] - m), 0.0)
        p_ref[:, j * blk:(j + 1) * blk] = e
        den = den + jnp.sum(e, axis=-1, keepdims=True)
    inv = 1.0 / den
    for j in range(nblk):
        p_ref[:, j * blk:(j + 1) * blk] = _rnd(p_ref[:, j * blk:(j + 1) * blk] * inv)
    pown_ref[...] = jnp.broadcast_to(_rnd(e_own * inv), pown_ref.shape)


def _dec_pv_kernel(pt_ref, src_ref, need_ref, p_ref, *refs):
    del pt_ref, src_ref
    v_refs, (pown_ref, vn_ref, o_ref) = refs[:-3], refs[-3:]
    bi = pl.program_id(0)
    j = pl.program_id(1)
    grp = p_ref.shape[1]
    nh = p_ref.shape[2]
    hd = nh * HEAD_DIM
    page = v_refs[0].shape[-1]

    @pl.when(j == 0)
    def _():
        o_ref[0] = pown_ref[0] * _rnd(vn_ref[0])

    for g in range(grp):
        @pl.when(need_ref[bi, j * grp + g] == 1)
        def _(g=g):
            pb = p_ref[0, g].astype(BF16)
            vv = [r[0, 0].reshape(hd, page).astype(BF16) for r in v_refs[PAGES_PER_BLOCK * g:PAGES_PER_BLOCK * (g + 1)]]
            o_all = _dot_nt(pb[:, :page], vv[0]) + _dot_nt(pb[:, page:], vv[1])
            col = lax.broadcasted_iota(jnp.int32, o_all.shape, 1)
            row = lax.broadcasted_iota(jnp.int32, o_all.shape, 0)
            o_ref[0] += jnp.sum(jnp.where(col // HEAD_DIM == row, o_all, 0.0), axis=0, keepdims=True)


def _moba_decode(proj_s, cache_kt, cache_vt, page_table, rel_bias, layer):
    b, n = proj_s.shape
    w = n // 4
    nh = w // HEAD_DIM
    page = cache_kt.shape[4]
    past_len = page_table.shape[1] * page
    assert past_len % MOBA_BLOCK == 0
    nblk = past_len // MOBA_BLOCK
    q = proj_s[:, w:2 * w].reshape(b, nh, HEAD_DIM)
    qx = jnp.einsum('bhd,hg->bhgd', q, jnp.eye(nh, dtype=F32)).reshape(b, nh, w)
    kpos = np.arange(past_len).reshape(nblk, MOBA_BLOCK)
    bias = rel_bias.astype(F32).T[:, _t5_bucket(jnp.asarray(past_len - kpos, jnp.int32))]
    s_p, ks_p = _dec_scores(qx, cache_kt, page_table, jnp.transpose(bias, (1, 0, 2)), layer)
    rows = b * nh
    blk = MOBA_BLOCK
    to_rows = lambda a, last: jnp.transpose(a.reshape(b, nblk, nh, last), (0, 2, 1, 3)).reshape(rows, nblk * last)
    b0 = jnp.broadcast_to(jnp.tile(rel_bias.astype(F32)[_t5_bucket(jnp.zeros((), jnp.int32))], b)[:, None],
                          (rows, LANES))
    f = jax.ShapeDtypeStruct
    p_rows, p_own, sel = pl.pallas_call(
        _dec_select_kernel, name="moba_dec_select",
        out_shape=[f((rows, nblk * blk), F32), f((rows, LANES), F32), f((rows, nblk), F32)],
        compiler_params=pltpu.CompilerParams(vmem_limit_bytes=VMEM_LIMIT),
    )(q.reshape(rows, HEAD_DIM), proj_s[:, 2 * w:3 * w].reshape(rows, HEAD_DIM), b0,
      to_rows(ks_p, HEAD_DIM), to_rows(s_p, blk))

    grp = math.gcd(DEC_GROUP, nblk)
    need = (jnp.max(sel.reshape(b, nh, nblk), axis=1) > 0.5).reshape(b, nblk // grp, grp)
    blk_id = jnp.arange(nblk, dtype=jnp.int32).reshape(1, nblk // grp, grp)
    last_needed = lax.cummax(jnp.where(need, blk_id, -1), axis=1)
    first_needed = jnp.min(jnp.where(need, blk_id, nblk), axis=1, keepdims=True)
    fallback = jnp.where(first_needed < nblk, first_needed, blk_id[:, :1])
    src = jnp.where(last_needed < 0, fallback, last_needed).astype(jnp.int32).reshape(b, nblk)
    p_blk = jnp.transpose(p_rows.reshape(b, nh, nblk, blk), (0, 2, 1, 3))
    p_own_x = jnp.repeat(p_own[:, 0].reshape(b, nh), HEAD_DIM, axis=1).reshape(b, 1, w)
    page_spec = lambda g, off: pl.BlockSpec(
        (1, 1, nh, HEAD_DIM, page),
        lambda bi, j, pt, sr, nd: (pt[bi, PAGES_PER_BLOCK * sr[bi, j * grp + g] + off], layer, 0, 0, 0))
    seq_spec = pl.BlockSpec((1, 1, w), lambda bi, j, pt, sr, nd: (bi, 0, 0))
    att = pl.pallas_call(
        _dec_pv_kernel, name="moba_dec_pv",
        grid_spec=pltpu.PrefetchScalarGridSpec(
            num_scalar_prefetch=3,
            grid=(b, nblk // grp),
            in_specs=[pl.BlockSpec((1, grp, nh, blk), lambda bi, j, pt, sr, nd: (bi, j, 0, 0))]
            + [page_spec(g, off) for g in range(grp) for off in range(PAGES_PER_BLOCK)]
            + [seq_spec, seq_spec],
            out_specs=seq_spec,
        ),
        out_shape=f((b, 1, w), F32),
        compiler_params=_cparams("parallel", "arbitrary"),
    )(page_table, src, need.reshape(b, nblk).astype(jnp.int32), p_blk,
      *([cache_vt] * (grp * PAGES_PER_BLOCK)), p_own_x, proj_s[:, 3 * w:].reshape(b, 1, w))
    return att.reshape(b, w).astype(BF16)


def _tile(m, pref):
    t = min(m, pref)
    while m % t:
        t //= 2
    return t


def _ff_tile(ff, pref):
    best = LANES
    for t in range(LANES, pref + 1, LANES):
        if ff % t == 0:
            best = t
    return best


def kernel(x_prompt, x_sample, cache_k, cache_v, page_table, state_s5_re, state_s5_im, state_pool, rel_bias,
           norm_mix_e, w_in_e, q_norm_e, k_norm_e, s5_a_re, s5_a_im, s5_log_dt, s5_b_re, s5_b_im, s5_c_re,
           s5_c_im, s5_d, s5_w_glu, s5_b_glu, w_out_e, norm_ffn_e, ffn_w_gate, ffn_w_up, ffn_w_down,
           norm_mix_o, pool_w, pool_scale, norm_ffn_o, router_w, router_b, moe_w_gate, moe_w_up, moe_w_down):
    bp, seq, d = x_prompt.shape
    db = x_sample.shape[0]
    assert x_sample.shape[1] == 1
    depth = norm_mix_e.shape[0] + norm_mix_o.shape[0]
    wdt = w_in_e.shape[2] // 4
    nh = wdt // HEAD_DIM
    past_len = page_table.shape[1] * cache_k.shape[2]
    n_exp = router_w.shape[2]
    gn = s5_a_re.shape[1] * s5_a_re.shape[2]

    mp = bp * seq
    xp = x_prompt.reshape(mp, d)
    xs = x_sample.reshape(db, d)
    tm_p = _tile(mp, TOKEN_TILE)
    bias_tiles = _prompt_bias_tiles(rel_bias)
    row = lambda v: v.reshape(1, -1).astype(F32)
    cache_kt = jnp.transpose(cache_k, (0, 1, 3, 4, 2))
    cache_vt = jnp.transpose(cache_v, (0, 1, 3, 4, 2))

    n_even = norm_mix_e.shape[0]
    kp_all = jnp.zeros((mp, n_even * wdt), F32)
    vp_all = jnp.zeros((mp, n_even * wdt), F32)
    ks_all = jnp.zeros((db, n_even * wdt), F32)
    vs_all = jnp.zeros((db, n_even * wdt), F32)
    moe_wg, moe_wu, moe_wd = moe_w_gate.astype(BF16), moe_w_up.astype(BF16), moe_w_down.astype(BF16)
    s5p_l, s5s_re_l, s5s_im_l = [], [], []
    poolp_l, pools_l = [], []
    for layer in range(depth):
        if layer % 2 == 0:
            e = layer // 2
            w_in = w_in_e[e].astype(BF16)
            ones = jnp.ones((wdt,), F32)
            head_gain = jnp.stack([ones, jnp.tile(q_norm_e[e].astype(F32), nh),
                                   jnp.tile(k_norm_e[e].astype(F32), nh), ones]).reshape(4, 1, wdt)
            wb, tabs, lam, wc = _s5_weights(s5_a_re[e], s5_a_im[e], s5_log_dt[e], s5_b_re[e], s5_b_im[e],
                                            s5_c_re[e], s5_c_im[e])
            dsk = row(s5_d[e])
            wgl = s5_w_glu[e].astype(BF16)
            bgl = row(s5_b_glu[e])
            w_out = w_out_e[e].astype(BF16)
            wg, wu, wd = ffn_w_gate[e].astype(BF16), ffn_w_up[e].astype(BF16), ffn_w_down[e].astype(BF16)
            tf = _ff_tile(wg.shape[1], FFN_TILE)

            uq, kp_all, vp_all = _in_proj(xp, row(norm_mix_e[e]), w_in, head_gain, kp_all, vp_all, e, tm_p)
            uq3 = uq.reshape(bp, seq, 2 * wdt)
            s5_out, hfin = _s5_scan(uq3, jnp.zeros((bp, 1, 2 * gn), F32), wb, tabs, wc, dsk, wgl, bgl,
                                    _tile(seq, SCAN_TILE))
            att = _moba_prompt(uq3, kp_all.reshape(bp, seq, -1), vp_all.reshape(bp, seq, -1), e, bias_tiles)
            xp = _mix_ffn(xp, s5_out.reshape(mp, wdt), att.reshape(mp, wdt), w_out, row(norm_ffn_e[e]),
                          wg, wu, wd, tm_p, tf)
            s5p_l.append(hfin.reshape(bp, 2, -1, S5_STATE))

            uq_s, ks_all, vs_all = _in_proj(xs, row(norm_mix_e[e]), w_in, head_gain, ks_all, vs_all, e, db)
            proj_s = jnp.concatenate([uq_s, ks_all[:, e * wdt:(e + 1) * wdt], vs_all[:, e * wdt:(e + 1) * wdt]],
                                     axis=1)
            s5_out_s, hr_s, hi_s = _s5_step(proj_s, state_s5_re[:, e].reshape(db, gn).astype(F32),
                                            state_s5_im[:, e].reshape(db, gn).astype(F32),
                                            wb, lam, wc, dsk, wgl, bgl)
            att_s = _moba_decode(proj_s, cache_kt, cache_vt, page_table, rel_bias, e)
            xs = _mix_ffn(xs, s5_out_s, att_s, w_out, row(norm_ffn_e[e]), wg, wu, wd, db, tf)
            s5s_re_l.append(hr_s.reshape(db, -1, S5_STATE))
            s5s_im_l.append(hi_s.reshape(db, -1, S5_STATE))
        else:
            o = layer // 2
            wp = pool_w[o].astype(BF16)
            sc = row(pool_scale[o])
            wr = jnp.zeros((d, LANES), BF16).at[:, :n_exp].set(router_w[o].astype(BF16))
            br = jnp.zeros((1, LANES), F32).at[:, :n_exp].set(router_b[o].astype(F32))
            tf = _ff_tile(moe_wg.shape[3], EXPERT_TILE)

            xp3, hist_p = _pool_prompt(xp.reshape(bp, seq, d), row(norm_mix_o[o]), wp, sc, _tile(seq, TOKEN_TILE))
            xp = _moe_sparse(xp3.reshape(mp, d), row(norm_ffn_o[o]), wr, br, moe_wg, moe_wu, moe_wd, o, tm_p, tf // 2)
            poolp_l.append(hist_p[:, HALO - POOL_HIST:])

            hist_s = state_pool[:, o].astype(F32)
            xs, hn_s = _pool_step(xs, row(norm_mix_o[o]), jnp.transpose(hist_s, (1, 0, 2)), wp, sc,
                                  min(POOL_HIST, past_len))
            xs = _moe_dense(xs, row(norm_ffn_o[o]), wr, br, moe_wg, moe_wu, moe_wd, o, db, tf)
            pools_l.append(jnp.concatenate([hist_s[:, 1:], hn_s[:, None]], axis=1))

    s5p = jnp.stack(s5p_l, axis=1)

    def per_head(a, lead):
        parts = [a[:, l * wdt:(l + 1) * wdt].reshape(lead + (nh, HEAD_DIM)) for l in range(n_even)]
        return jnp.stack(parts, axis=len(lead))

    return (xp.reshape(bp, seq, d), xs.reshape(db, 1, d),
            per_head(kp_all, (bp, seq)), per_head(vp_all, (bp, seq)),
            per_head(ks_all, (db, 1)), per_head(vs_all, (db, 1)),
            s5p[:, :, 0], s5p[:, :, 1],
            jnp.stack(s5s_re_l, axis=1), jnp.stack(s5s_im_l, axis=1),
            jnp.stack(poolp_l, axis=1), jnp.stack(pools_l, axis=1))
```
